```python
import jax, jax.numpy as jnp
from jax import lax
import numpy as np

D_MODEL = 1024
BATCH = 16
SEQ = 2048
DEPTH = 1
DEC_BATCH = 32
DEC_SEQ = 4
PAST_LEN = 16384
PAGE_SIZE = 128

HEAD_DIM = 64
N_HEADS = 8
D_ATTN = N_HEADS * HEAD_DIM
D_CONV = D_MODEL - D_ATTN
CONV_WIDTH = 3
WINDOWS = (128, 512, 2048)
DILATIONS = (1, 4, 16)
MAX_WINDOW = max(WINDOWS)
Q_BLOCK = 128
D_IN = 3 * D_ATTN + 3 * D_CONV
D_FF = -(-8 * D_MODEL // (3 * 256)) * 256
RMS_EPS = 1e-6
ATTN_SCALE = HEAD_DIM ** -0.5
NEG_INF = -1e30

kernel_name = "hymba_longnet_shortconv_step"


def rms_norm(x, g):
    xf = x.astype(jnp.float32)
    y = xf * lax.rsqrt(jnp.mean(xf * xf, axis=-1, keepdims=True) + RMS_EPS)
    return (y * g.astype(jnp.float32)).astype(x.dtype)


def alibi_slopes():
    return jnp.exp2(-8.0 * jnp.arange(1, N_HEADS + 1, dtype=jnp.float32) / N_HEADS)


def _project(xn, w_in):
    b, t, _ = xn.shape
    z = xn @ w_in
    q, k, v, hc, gb, gc = jnp.split(
        z, [D_ATTN, 2 * D_ATTN, 3 * D_ATTN, 3 * D_ATTN + D_CONV, 3 * D_ATTN + 2 * D_CONV], axis=-1)
    heads = lambda a: a.reshape(b, t, N_HEADS, HEAD_DIM)
    return heads(q), heads(k), heads(v), hc, gb, gc


def _short_conv(u_ext, w_conv, t):
    acc = w_conv[0] * u_ext[:, 0:t]
    for i in range(1, CONV_WIDTH):
        acc = acc + w_conv[i] * u_ext[:, i:i + t]
    return acc


def _dilated_prompt(q, k, v, window, dilation, slopes):
    b, s, h, hd = q.shape
    nk = window // dilation
    n_sub = s // dilation
    nb = -(-n_sub // Q_BLOCK)
    lp = nb * Q_BLOCK

    def classes(a, front):
        a = a.reshape(b, n_sub, dilation, h, hd).transpose(0, 2, 1, 3, 4)
        return jnp.pad(a, ((0, 0), (0, 0), (front, lp - n_sub), (0, 0), (0, 0)))

    qb = classes(q, 0).reshape(b, dilation, nb, Q_BLOCK, h, hd)
    kidx = jnp.arange(nb)[:, None] * Q_BLOCK + jnp.arange(Q_BLOCK + nk)[None, :]
    kb = jnp.take(classes(k, nk), kidx, axis=2)
    vb = jnp.take(classes(v, nk), kidx, axis=2)
    s_qk = jnp.einsum('brnqhd,brnkhd->brnhqk', qb, kb) * ATTN_SCALE
    dist = jnp.arange(Q_BLOCK)[:, None] + nk - jnp.arange(Q_BLOCK + nk)[None, :]
    key_sub = kidx - nk
    valid = (dist >= 0)[None] & (dist <= nk)[None] & (key_sub >= 0)[:, None, :]
    bias = -slopes[:, None, None] * (dilation * dist).astype(jnp.float32)[None]
    s_qk = jnp.where(valid[None, None, :, None], s_qk + bias, NEG_INF)
    m = s_qk.max(-1)
    p = jnp.exp(s_qk - m[..., None])
    l = p.sum(-1)
    o = jnp.einsum('brnhqk,brnkhd->brnqhd', p, vb) / jnp.swapaxes(l, -1, -2)[..., None]

    def back(a):
        a = a.reshape((b, dilation, lp) + a.shape[4:])[:, :, :n_sub]
        return jnp.swapaxes(a, 1, 2).reshape((b, s) + a.shape[3:])

    return back(jnp.swapaxes(m, -1, -2)), back(jnp.swapaxes(l, -1, -2)), back(o)


def _dilated_sample(q, k_all, v_all, n_past, window, dilation, slopes):
    t = q.shape[1]
    nk = window // dilation
    j = jnp.arange(nk + 1)
    idx = n_past + jnp.arange(t)[:, None] - dilation * j[None, :]
    valid = idx >= 0
    idx = jnp.maximum(idx, 0)
    kg = k_all[:, idx]
    vg = v_all[:, idx]
    s_qk = (jnp.einsum('bthd,btjhd->bhtj', q, kg) * ATTN_SCALE
            - slopes[:, None, None] * (dilation * j).astype(jnp.float32))
    s_qk = jnp.where(valid, s_qk, NEG_INF)
    m = s_qk.max(-1)
    p = jnp.exp(s_qk - m[..., None])
    l = p.sum(-1)
    o = jnp.einsum('bhtj,btjhd->bthd', p, vg) / jnp.swapaxes(l, 1, 2)[..., None]
    return jnp.swapaxes(m, 1, 2), jnp.swapaxes(l, 1, 2), o


def _merge_branches(parts):
    m = jnp.stack([p[0] for p in parts])
    l = jnp.stack([p[1] for p in parts])
    o = jnp.stack([p[2] for p in parts])
    w = l * jnp.exp(m - m.max(0, keepdims=True))
    return (w[..., None] * o).sum(0) / w.sum(0)[..., None]


def _finish(x, attn_o, conv_o, w_out, norm_ffn_g, w_gate, w_up, w_down):
    b, t, _ = x.shape
    mix = jnp.concatenate([attn_o.reshape(b, t, D_ATTN).astype(x.dtype), conv_o], axis=-1)
    h = x + mix @ w_out
    hn = rms_norm(h, norm_ffn_g)
    return h + (jax.nn.silu(hn @ w_gate) * (hn @ w_up)) @ w_down


def _layer_prompt(x, norm_mix_g, w_in, w_conv, w_out, norm_ffn_g, w_gate, w_up, w_down):
    t = x.shape[1]
    xn = rms_norm(x, norm_mix_g)
    q, k, v, hc, gb, gc = _project(xn, w_in)
    slopes = alibi_slopes()
    qf, kf, vf = q.astype(jnp.float32), k.astype(jnp.float32), v.astype(jnp.float32)
    parts = [_dilated_prompt(qf, kf, vf, w, d, slopes) for w, d in zip(WINDOWS, DILATIONS)]
    attn_o = _merge_branches(parts)
    u = gc * hc
    u_ext = jnp.pad(u, ((0, 0), (CONV_WIDTH - 1, 0), (0, 0)))
    conv_o = gb * _short_conv(u_ext, w_conv, t)
    y = _finish(x, attn_o, conv_o, w_out, norm_ffn_g, w_gate, w_up, w_down)
    keep = min(MAX_WINDOW, t)
    return y, k[:, t - keep:], v[:, t - keep:], u_ext[:, -(CONV_WIDTH - 1):]


def _layer_sample(x, k_buf, v_buf, conv_buf, norm_mix_g, w_in, w_conv, w_out,
                  norm_ffn_g, w_gate, w_up, w_down):
    t = x.shape[1]
    xn = rms_norm(x, norm_mix_g)
    q, k, v, hc, gb, gc = _project(xn, w_in)
    slopes = alibi_slopes()
    n_past = k_buf.shape[1]
    k_all = jnp.concatenate([k_buf.astype(k.dtype), k], axis=1).astype(jnp.float32)
    v_all = jnp.concatenate([v_buf.astype(v.dtype), v], axis=1).astype(jnp.float32)
    qf = q.astype(jnp.float32)
    parts = [_dilated_sample(qf, k_all, v_all, n_past, w, d, slopes)
             for w, d in zip(WINDOWS, DILATIONS)]
    attn_o = _merge_branches(parts)
    u = gc * hc
    u_ext = jnp.concatenate([conv_buf.astype(u.dtype), u], axis=1)
    conv_o = gb * _short_conv(u_ext, w_conv, t)
    y = _finish(x, attn_o, conv_o, w_out, norm_ffn_g, w_gate, w_up, w_down)
    return y, k, v, u_ext[:, -(CONV_WIDTH - 1):]


def setup_inputs(seed: int = 0) -> dict:
    key = jax.random.key(seed)
    ks = jax.random.split(key, 16)
    f32 = jnp.float32
    n_buf = min(MAX_WINDOW, PAST_LEN)

    def nrm(k, shape, scale):
        return jax.random.normal(k, shape, f32) * scale

    return {
        'x_prompt': nrm(ks[0], (BATCH, SEQ, D_MODEL), 1.0),
        'x_sample': nrm(ks[1], (DEC_BATCH, DEC_SEQ, D_MODEL), 1.0),
        'state_attn_k': nrm(ks[2], (DEPTH, DEC_BATCH, n_buf, N_HEADS, HEAD_DIM), 1.0),
        'state_attn_v': nrm(ks[3], (DEPTH, DEC_BATCH, n_buf, N_HEADS, HEAD_DIM), 1.0),
        'state_conv': nrm(ks[4], (DEPTH, DEC_BATCH, CONV_WIDTH - 1, D_CONV), 1.0),
        'norm_mix_g': 1.0 + nrm(ks[5], (DEPTH, D_MODEL), 0.01),
        'w_in': nrm(ks[6], (DEPTH, D_MODEL, D_IN), D_MODEL ** -0.5),
        'w_conv': nrm(ks[7], (DEPTH, CONV_WIDTH, D_CONV), CONV_WIDTH ** -0.5),
        'w_out': nrm(ks[8], (DEPTH, D_ATTN + D_CONV, D_MODEL), (D_ATTN + D_CONV) ** -0.5),
        'norm_ffn_g': 1.0 + nrm(ks[9], (DEPTH, D_MODEL), 0.01),
        'w_gate': nrm(ks[10], (DEPTH, D_MODEL, D_FF), D_MODEL ** -0.5),
        'w_up': nrm(ks[11], (DEPTH, D_MODEL, D_FF), D_MODEL ** -0.5),
        'w_down': nrm(ks[12], (DEPTH, D_FF, D_MODEL), D_FF ** -0.5),
        'norm_final_g': 1.0 + nrm(ks[13], (D_MODEL,), 0.01),
    }


def reference(x_prompt, x_sample, state_attn_k, state_attn_v, state_conv, norm_mix_g, w_in,
              w_conv, w_out, norm_ffn_g, w_gate, w_up, w_down, norm_final_g):
    yp, ys = x_prompt, x_sample
    kp_l, vp_l, cp_l, ks_l, vs_l, cs_l = [], [], [], [], [], []
    for layer in range(DEPTH):
        yp, kp, vp, cp = _layer_prompt(yp, norm_mix_g[layer], w_in[layer], w_conv[layer],
                                       w_out[layer], norm_ffn_g[layer], w_gate[layer],
                                       w_up[layer], w_down[layer])
        ys, ks, vs, cs = _layer_sample(ys, state_attn_k[layer], state_attn_v[layer],
                                       state_conv[layer], norm_mix_g[layer], w_in[layer],
                                       w_conv[layer], w_out[layer], norm_ffn_g[layer],
                                       w_gate[layer], w_up[layer], w_down[layer])
        kp_l.append(kp); vp_l.append(vp); cp_l.append(cp)
        ks_l.append(ks); vs_l.append(vs); cs_l.append(cs)
    y_prompt = rms_norm(yp, norm_final_g)
    y_sample = rms_norm(ys, norm_final_g)
    new_k_prompt = jnp.stack(kp_l)
    new_v_prompt = jnp.stack(vp_l)
    new_conv_prompt = jnp.stack(cp_l)
    new_k_sample = jnp.stack(ks_l)
    new_v_sample = jnp.stack(vs_l)
    new_conv_sample = jnp.stack(cs_l)
    return (y_prompt, y_sample, new_k_prompt, new_v_prompt, new_conv_prompt,
            new_k_sample, new_v_sample, new_conv_sample)
```

```python
import functools

import jax
import jax.numpy as jnp
from jax import lax
from jax.experimental import pallas as pl
from jax.experimental.pallas import tpu as pltpu

F32 = jnp.float32
BF16 = jnp.bfloat16

HEAD_DIM = 64
N_HEADS = 8
D_ATTN = N_HEADS * HEAD_DIM
CONV_WIDTH = 3
WINDOWS = (128, 512, 2048)
DILATIONS = (1, 4, 16)
RMS_EPS = 1e-6
ATTN_SCALE = HEAD_DIM ** -0.5
NEG_INF = -1e30

V7X_LANES = 128
V7X_SUBLANES = 8
V7X_VMEM_BYTES = 64 * 1024 * 1024

HEADS_PER_STEP = V7X_LANES // HEAD_DIM
TQ = 128
TK = 256
N_DELTA_TILES = (WINDOWS[1] + TK) // TQ + 1


def _rms(x, g):
    y = x * lax.rsqrt(jnp.mean(x * x, axis=-1, keepdims=True) + RMS_EPS)
    return y * g


def _branch_count(delta):
    nonneg = delta >= 0
    c = jnp.zeros(delta.shape, F32)
    for w, d in zip(WINDOWS, DILATIONS):
        hit = nonneg & ((delta & (d - 1)) == 0) & (delta <= w)
        c = c + hit.astype(F32)
    return c


def _dot_nt(a, b):
    return lax.dot_general(a, b, (((1,), (1,)), ((), ())), preferred_element_type=F32)


def _project(x, g, w):
    d_conv = (w.shape[1] - 3 * D_ATTN) // 3
    xn = _rms(x, g)
    z = jnp.dot(xn.astype(BF16), w, preferred_element_type=F32)
    q = z[:, 0:D_ATTN] * ATTN_SCALE
    k = z[:, D_ATTN:2 * D_ATTN]
    v = z[:, 2 * D_ATTN:3 * D_ATTN]
    o = 3 * D_ATTN
    hc = z[:, o:o + d_conv]
    gb = z[:, o + d_conv:o + 2 * d_conv]
    gc = z[:, o + 2 * d_conv:o + 3 * d_conv]
    return q, k, v, hc, gb, gc


def _conv3(wc, u2, u1, u0):
    acc = wc[0:1, :] * u2
    acc = acc + wc[1:2, :] * u1
    return acc + wc[2:3, :] * u0


def _inproj_prompt_kernel(x_ref, g_ref, w_ref, wc_ref, q_ref, k_ref, v_ref, co_ref, ulast_ref,
                          uext_ref, *, tiles_per_seq):
    tm = x_ref.shape[0]
    j = pl.program_id(0) % tiles_per_seq
    q, k, v, hc, gb, gc = _project(x_ref[...], g_ref[...], w_ref[...])
    q_ref[...] = q
    k_ref[...] = k
    v_ref[...] = v
    u = gc * hc

    @pl.when(j == 0)
    def _():
        uext_ref[0:V7X_SUBLANES, :] = jnp.zeros((V7X_SUBLANES, u.shape[1]), F32)

    uext_ref[V7X_SUBLANES:V7X_SUBLANES + tm, :] = u
    u1 = uext_ref[V7X_SUBLANES - 1:V7X_SUBLANES - 1 + tm, :]
    u2 = uext_ref[V7X_SUBLANES - 2:V7X_SUBLANES - 2 + tm, :]
    co_ref[...] = (gb * _conv3(wc_ref[...], u2, u1, u)).astype(co_ref.dtype)
    tail = u[tm - V7X_SUBLANES:tm, :]
    uext_ref[0:V7X_SUBLANES, :] = tail

    @pl.when(j == tiles_per_seq - 1)
    def _():
        ulast_ref[0] = tail


def _inproj_sample_kernel(x_ref, g_ref, w_ref, wc_ref, p1_ref, p2_ref, q_ref, k_ref, v_ref,
                          co_ref, u_ref, uext_ref, *, dec_seq):
    rows = x_ref.shape[0]
    q, k, v, hc, gb, gc = _project(x_ref[...], g_ref[...], w_ref[...])
    q_ref[...] = q
    k_ref[...] = k
    v_ref[...] = v
    u = gc * hc
    u_ref[...] = u
    uext_ref[0:V7X_SUBLANES, :] = jnp.zeros((V7X_SUBLANES, u.shape[1]), F32)
    uext_ref[V7X_SUBLANES:V7X_SUBLANES + rows, :] = u
    t = lax.broadcasted_iota(jnp.int32, u.shape, 0) % dec_seq
    u1 = jnp.where(t >= 1, uext_ref[V7X_SUBLANES - 1:V7X_SUBLANES - 1 + rows, :], p1_ref[...])
    u2 = jnp.where(t >= 2, uext_ref[V7X_SUBLANES - 2:V7X_SUBLANES - 2 + rows, :], p2_ref[...])
    co_ref[...] = (gb * _conv3(wc_ref[...], u2, u1, u)).astype(co_ref.dtype)


def _attn_prompt_kernel(slopes_ref, q_ref, k_ref, v_ref, o_ref,
                        qb_ref, kb_ref, vb_ref, cnt_ref, bm_ref, acc_ref):
    seq = q_ref.shape[0]
    pair = pl.program_id(1)
    slope = [slopes_ref[HEADS_PER_STEP * pair + h] for h in range(HEADS_PER_STEP)]
    lane = lax.broadcasted_iota(jnp.int32, (1, V7X_LANES), 1)
    first_head = lane < HEAD_DIM

    qb_ref[...] = q_ref[...].astype(BF16)
    kb_ref[...] = k_ref[...].astype(BF16)
    vb_ref[...] = v_ref[...].astype(BF16)

    ii = lax.broadcasted_iota(jnp.int32, (TQ, TK), 0)
    cc = lax.broadcasted_iota(jnp.int32, (TQ, TK), 1)
    for di in range(N_DELTA_TILES):
        delta = TQ * di + ii - cc
        c = _branch_count(delta)
        cnt_ref[di] = c
        masked = jnp.where(c > 0.0, 0.0, NEG_INF)
        for h in range(HEADS_PER_STEP):
            bm_ref[h, di] = masked - slope[h] * delta.astype(F32)

    def q_block(qi, carry):
        q0 = pl.multiple_of(qi * TQ, TQ)
        q = qb_ref[pl.ds(q0, TQ), :]
        zero = jnp.zeros_like(q)
        qh = [jnp.where(first_head, q, zero), jnp.where(first_head, zero, q)]
        acc_ref[...] = jnp.zeros(acc_ref.shape, F32)

        def kv_block(kj, ml):
            k0 = pl.multiple_of(kj * TK, TK)
            k = kb_ref[pl.ds(k0, TK), :]
            v = vb_ref[pl.ds(k0, TK), :]
            dblk = qi - (TK // TQ) * kj
            di = jnp.minimum(dblk, N_DELTA_TILES - 1)
            far = ((dblk - di) * TQ).astype(F32)
            cnt = cnt_ref[di]
            new_ml, alphas, pvs = [], [], []
            for h in range(HEADS_PER_STEP):
                m_old, l_old = ml[2 * h], ml[2 * h + 1]
                s = _dot_nt(qh[h], k) + bm_ref[h, di] - slope[h] * far
                m_new = jnp.maximum(m_old, jnp.max(s, axis=-1, keepdims=True))
                alpha = jnp.exp(m_old - m_new)
                p = cnt * jnp.exp(s - m_new)
                l_new = alpha * l_old + jnp.sum(p, axis=-1, keepdims=True)
                pvs.append(jnp.dot(p.astype(BF16), v, preferred_element_type=F32))
                alphas.append(alpha)
                new_ml += [m_new, l_new]
            alpha2 = jnp.where(first_head, alphas[0], alphas[1])
            pv2 = jnp.where(first_head, pvs[0], pvs[1])
            acc_ref[...] = alpha2 * acc_ref[...] + pv2
            return tuple(new_ml)

        col = jnp.full((TQ, 1), NEG_INF, F32)
        zcol = jnp.zeros((TQ, 1), F32)
        ml = lax.fori_loop(0, qi // (TK // TQ) + 1, kv_block, (col, zcol, col, zcol))
        l2 = jnp.where(first_head, ml[1], ml[3])
        o_ref[pl.ds(q0, TQ), :] = (acc_ref[...] / l2).astype(o_ref.dtype)
        return carry

    lax.fori_loop(0, seq // TQ, q_block, 0)


def _attn_sample_kernel(slopes_ref, q_ref, kn_ref, vn_ref, ks_ref, vs_ref, o_ref,
                        knew_ref, vnew_ref):
    dec_seq = q_ref.shape[1]
    n_past = ks_ref.shape[1]
    width = q_ref.shape[2]
    n_rows = dec_seq * N_HEADS

    q = q_ref[0]
    head_of_lane = lax.broadcasted_iota(jnp.int32, (N_HEADS, width), 1) // HEAD_DIM
    head_of_row = lax.broadcasted_iota(jnp.int32, (N_HEADS, width), 0)
    head_mask = head_of_lane == head_of_row
    wt = jnp.concatenate(
        [jnp.where(head_mask, jnp.broadcast_to(q[t:t + 1, :], (N_HEADS, width)), 0.0)
         for t in range(dec_seq)], axis=0).astype(BF16)

    knew_ref[...] = jnp.zeros(knew_ref.shape, F32)
    vnew_ref[...] = jnp.zeros(vnew_ref.shape, F32)
    knew_ref[0:dec_seq, :] = kn_ref[0]
    vnew_ref[0:dec_seq, :] = vn_ref[0]

    s_past = _dot_nt(wt, ks_ref[0].astype(BF16))
    s_new = _dot_nt(wt, knew_ref[...].astype(BF16))

    row = lax.broadcasted_iota(jnp.int32, (n_rows, 1), 0)
    step = row // N_HEADS
    slope_col = jnp.zeros((n_rows, 1), F32)
    for h in range(N_HEADS):
        slope_col = jnp.where(row % N_HEADS == h, slopes_ref[h], slope_col)

    def weigh(s, key_pos):
        delta = n_past + step - key_pos
        c = _branch_count(delta)
        s = jnp.where(c > 0.0, s - slope_col * delta.astype(F32), NEG_INF)
        return s, c

    s_past, c_past = weigh(s_past, lax.broadcasted_iota(jnp.int32, (1, n_past), 1))
    s_new, c_new = weigh(s_new, n_past + lax.broadcasted_iota(jnp.int32, (1, V7X_LANES), 1))
    m = jnp.maximum(jnp.max(s_past, axis=-1, keepdims=True),
                    jnp.max(s_new, axis=-1, keepdims=True))
    p_past = c_past * jnp.exp(s_past - m)
    p_new = c_new * jnp.exp(s_new - m)
    l = jnp.sum(p_past, axis=-1, keepdims=True) + jnp.sum(p_new, axis=-1, keepdims=True)
    o_all = (jnp.dot(p_past.astype(BF16), vs_ref[0].astype(BF16), preferred_element_type=F32)
             + jnp.dot(p_new.astype(BF16), vnew_ref[...].astype(BF16),
                       preferred_element_type=F32)) / l
    outs = []
    for t in range(dec_seq):
        blk = o_all[t * N_HEADS:(t + 1) * N_HEADS, :]
        outs.append(jnp.sum(jnp.where(head_mask, blk, 0.0), axis=0, keepdims=True))
    o_ref[0] = jnp.concatenate(outs, axis=0).astype(o_ref.dtype)


def _ffn_kernel(x_ref, a_ref, c_ref, wo_ref, g2_ref, wg_ref, wu_ref, wd_ref, gf_ref, y_ref, *,
                final_norm):
    mix = jnp.concatenate([a_ref[...].astype(BF16), c_ref[...].astype(BF16)], axis=-1)
    h = x_ref[...] + jnp.dot(mix, wo_ref[...], preferred_element_type=F32)
    hn = _rms(h, g2_ref[...]).astype(BF16)
    gate = jnp.dot(hn, wg_ref[...], preferred_element_type=F32)
    up = jnp.dot(hn, wu_ref[...], preferred_element_type=F32)
    act = (gate * (1.0 / (1.0 + jnp.exp(-gate)))) * up
    y = h + jnp.dot(act.astype(BF16), wd_ref[...], preferred_element_type=F32)
    y_ref[...] = _rms(y, gf_ref[...]) if final_norm else y


def _nbytes(shape, dtype):
    n = 1
    for s in shape:
        n *= s
    return n * jnp.dtype(dtype).itemsize


def _vmem_limit(pipelined, resident, temporaries):
    need = 2 * sum(pipelined) + sum(resident) + sum(temporaries)
    return min(V7X_VMEM_BYTES, need + need // 4)


def _resident(shape):
    zeros = (0,) * len(shape)
    return pl.BlockSpec(shape, lambda *_: zeros, pipeline_mode=pl.Buffered(1))


def _row_tile(rows):
    for tm in (512, 256, 128):
        if rows % tm == 0:
            return tm
    raise ValueError(f"row count {rows} is not a multiple of 128")


def _inproj_prompt(x, g, w_bf, wc, seq):
    rows, d_model = x.shape
    d_in = w_bf.shape[1]
    d_conv = wc.shape[1]
    tm = _row_tile(seq)
    tiles_per_seq = seq // tm
    n_seq = rows // seq
    row_blk = lambda width: pl.BlockSpec((tm, width), lambda i: (i, 0))
    limit = _vmem_limit(
        [_nbytes((tm, d_model), F32), 3 * _nbytes((tm, D_ATTN), F32), _nbytes((tm, d_conv), BF16)],
        [_nbytes(w_bf.shape, BF16)],
        [_nbytes((tm, d_model), F32), _nbytes((tm, d_in), F32) * 2, _nbytes((tm, d_conv), F32) * 4])
    return pl.pallas_call(
        functools.partial(_inproj_prompt_kernel, tiles_per_seq=tiles_per_seq),
        grid=(rows // tm,),
        in_specs=[row_blk(d_model), _resident((1, d_model)), _resident(w_bf.shape),
                  _resident(wc.shape)],
        out_specs=[row_blk(D_ATTN), row_blk(D_ATTN), row_blk(D_ATTN), row_blk(d_conv),
                   pl.BlockSpec((1, V7X_SUBLANES, d_conv), lambda i: (i // tiles_per_seq, 0, 0))],
        out_shape=[jax.ShapeDtypeStruct((rows, D_ATTN), F32)] * 3
        + [jax.ShapeDtypeStruct((rows, d_conv), BF16),
           jax.ShapeDtypeStruct((n_seq, V7X_SUBLANES, d_conv), F32)],
        scratch_shapes=[pltpu.VMEM((tm + V7X_SUBLANES, d_conv), F32)],
        compiler_params=pltpu.CompilerParams(dimension_semantics=("arbitrary",),
                                             vmem_limit_bytes=limit),
        name="inproj_prompt",
    )(x, g, w_bf, wc)


def _inproj_sample(x, g, w_bf, wc, prev1, prev2, dec_seq):
    rows, d_model = x.shape
    d_conv = wc.shape[1]
    full = lambda shape: pl.BlockSpec(shape, lambda i: (0,) * len(shape))
    return pl.pallas_call(
        functools.partial(_inproj_sample_kernel, dec_seq=dec_seq),
        grid=(1,),
        in_specs=[full(x.shape), full((1, d_model)), full(w_bf.shape), full(wc.shape),
                  full(prev1.shape), full(prev2.shape)],
        out_specs=[full((rows, D_ATTN))] * 3 + [full((rows, d_conv))] * 2,
        out_shape=[jax.ShapeDtypeStruct((rows, D_ATTN), F32)] * 3
        + [jax.ShapeDtypeStruct((rows, d_conv), BF16), jax.ShapeDtypeStruct((rows, d_conv), F32)],
        scratch_shapes=[pltpu.VMEM((rows + V7X_SUBLANES, d_conv), F32)],
        compiler_params=pltpu.CompilerParams(dimension_semantics=("arbitrary",)),
        name="inproj_sample",
    )(x, g, w_bf, wc, prev1, prev2)


def _attn_prompt(slopes, q, k, v, seq):
    rows = q.shape[0]
    n_seq = rows // seq
    blk = pl.BlockSpec((seq, V7X_LANES), lambda b, p: (b, p))
    tile = _nbytes((TQ, TK), F32)
    limit = _vmem_limit(
        [3 * _nbytes((seq, V7X_LANES), F32), _nbytes((seq, V7X_LANES), BF16)],
        [3 * _nbytes((seq, V7X_LANES), BF16), (1 + HEADS_PER_STEP) * N_DELTA_TILES * tile],
        [16 * tile])
    return pl.pallas_call(
        _attn_prompt_kernel,
        grid=(n_seq, N_HEADS // HEADS_PER_STEP),
        in_specs=[pl.BlockSpec(memory_space=pltpu.SMEM), blk, blk, blk],
        out_specs=blk,
        out_shape=jax.ShapeDtypeStruct((rows, D_ATTN), BF16),
        scratch_shapes=[pltpu.VMEM((seq, V7X_LANES), BF16)] * 3
        + [pltpu.VMEM((N_DELTA_TILES, TQ, TK), F32),
           pltpu.VMEM((HEADS_PER_STEP, N_DELTA_TILES, TQ, TK), F32),
           pltpu.VMEM((TQ, V7X_LANES), F32)],
        compiler_params=pltpu.CompilerParams(dimension_semantics=("arbitrary", "arbitrary"),
                                             vmem_limit_bytes=limit),
        name="attn_prompt",
    )(slopes, q, k, v)


def _attn_sample(slopes, q, kn, vn, ks, vs):
    n_seq, dec_seq, width = q.shape
    n_past = ks.shape[1]
    new_blk = pl.BlockSpec((1, dec_seq, width), lambda b: (b, 0, 0))
    past_blk = pl.BlockSpec((1, n_past, width), lambda b: (b, 0, 0))
    limit = _vmem_limit(
        [2 * _nbytes((n_past, width), F32)], [],
        [2 * _nbytes((n_past, width), BF16), 8 * _nbytes((dec_seq * N_HEADS, n_past), F32)])
    return pl.pallas_call(
        _attn_sample_kernel,
        grid=(n_seq,),
        in_specs=[pl.BlockSpec(memory_space=pltpu.SMEM), new_blk, new_blk, new_blk,
                  past_blk, past_blk],
        out_specs=new_blk,
        out_shape=jax.ShapeDtypeStruct((n_seq, dec_seq, width), F32),
        scratch_shapes=[pltpu.VMEM((V7X_LANES, width), F32)] * 2,
        compiler_params=pltpu.CompilerParams(dimension_semantics=("arbitrary",),
                                             vmem_limit_bytes=limit),
        name="attn_sample",
    )(slopes, q, kn, vn, ks, vs)


def _ffn(x, attn_o, conv_o, wo, g2, wg, wu, wd, gf, final_norm, tm):
    rows, d_model = x.shape
    d_ff = wg.shape[1]
    row_blk = lambda width: pl.BlockSpec((tm, width), lambda i: (i, 0))
    limit = _vmem_limit(
        [2 * _nbytes((tm, d_model), F32), _nbytes((tm, attn_o.shape[1]), attn_o.dtype),
         _nbytes((tm, conv_o.shape[1]), conv_o.dtype)],
        [_nbytes(w.shape, BF16) for w in (wo, wg, wu, wd)],
        [4 * _nbytes((tm, d_model), F32), 3 * _nbytes((tm, d_ff), F32)])
    return pl.pallas_call(
        functools.partial(_ffn_kernel, final_norm=final_norm),
        grid=(rows // tm,),
        in_specs=[row_blk(d_model), row_blk(attn_o.shape[1]), row_blk(conv_o.shape[1]),
                  _resident(wo.shape), _resident((1, d_model)), _resident(wg.shape),
                  _resident(wu.shape), _resident(wd.shape), _resident((1, d_model))],
        out_specs=row_blk(d_model),
        out_shape=jax.ShapeDtypeStruct((rows, d_model), F32),
        compiler_params=pltpu.CompilerParams(dimension_semantics=("arbitrary",),
                                             vmem_limit_bytes=limit),
        name="ffn",
    )(x, attn_o, conv_o, wo, g2, wg, wu, wd, gf)


def _alibi_slopes():
    return jnp.exp2(-8.0 * jnp.arange(1, N_HEADS + 1, dtype=F32) / N_HEADS)


def kernel(x_prompt, x_sample, state_attn_k, state_attn_v, state_conv, norm_mix_g, w_in, w_conv,
           w_out, norm_ffn_g, w_gate, w_up, w_down, norm_final_g):
    depth = w_in.shape[0]
    batch, seq, d_model = x_prompt.shape
    dec_batch, dec_seq, _ = x_sample.shape
    d_conv = w_conv.shape[2]
    n_past = state_attn_k.shape[2]
    assert n_past >= max(WINDOWS) and seq <= max(WINDOWS) and seq % TK == 0
    assert CONV_WIDTH - 1 <= dec_seq <= V7X_LANES
    slopes = _alibi_slopes()
    gf = norm_final_g.reshape(1, d_model)

    yp = x_prompt.reshape(batch * seq, d_model)
    ys = x_sample.reshape(dec_batch * dec_seq, d_model)
    outs = [[] for _ in range(6)]
    for layer in range(depth):
        g1 = norm_mix_g[layer].reshape(1, d_model)
        g2 = norm_ffn_g[layer].reshape(1, d_model)
        w_in_bf = w_in[layer].astype(BF16)
        weights = [w.astype(BF16) for w in (w_out[layer], w_gate[layer], w_up[layer], w_down[layer])]
        wc = w_conv[layer]

        q, k, v, conv_o, u_last = _inproj_prompt(yp, g1, w_in_bf, wc, seq)
        attn_o = _attn_prompt(slopes, q, k, v, seq)
        y_layer = _ffn(yp, attn_o, conv_o, weights[0], g2, *weights[1:], gf,
                       layer == depth - 1, _row_tile(seq))
        outs[0].append(k.reshape(batch, seq, N_HEADS, HEAD_DIM))
        outs[1].append(v.reshape(batch, seq, N_HEADS, HEAD_DIM))
        outs[2].append(u_last[:, V7X_SUBLANES - (CONV_WIDTH - 1):, :])
        yp = y_layer

        st = state_conv[layer]
        prev1 = jnp.pad(st[:, 1:2], ((0, 0), (0, dec_seq - 1), (0, 0)))
        prev2 = jnp.pad(st, ((0, 0), (0, dec_seq - 2), (0, 0)))
        qs, ks, vs, conv_s, u_s = _inproj_sample(
            ys, g1, w_in_bf, wc, prev1.reshape(-1, d_conv), prev2.reshape(-1, d_conv), dec_seq)
        as3 = lambda a: a.reshape(dec_batch, dec_seq, D_ATTN)
        attn_s = _attn_sample(slopes, as3(qs), as3(ks), as3(vs),
                              state_attn_k[layer].reshape(dec_batch, n_past, D_ATTN),
                              state_attn_v[layer].reshape(dec_batch, n_past, D_ATTN))
        ys = _ffn(ys, attn_s.reshape(-1, D_ATTN), conv_s, weights[0], g2, *weights[1:], gf,
                  layer == depth - 1, ys.shape[0])
        outs[3].append(ks.reshape(dec_batch, dec_seq, N_HEADS, HEAD_DIM))
        outs[4].append(vs.reshape(dec_batch, dec_seq, N_HEADS, HEAD_DIM))
        outs[5].append(u_s.reshape(dec_batch, dec_seq, d_conv)[:, dec_seq - (CONV_WIDTH - 1):])

    y_prompt = yp.reshape(batch, seq, d_model)
    y_sample = ys.reshape(dec_batch, dec_seq, d_model)
    new_k_p, new_v_p, new_c_p, new_k_s, new_v_s, new_c_s = [jnp.stack(o) for o in outs]
    return (y_prompt, y_sample, new_k_p, new_v_p, new_c_p, new_k_s, new_v_s, new_c_s)
```

```python
import functools

import jax
import jax.numpy as jnp
from jax import lax
from jax.experimental import pallas as pl
from jax.experimental.pallas import tpu as pltpu

F32 = jnp.float32
BF16 = jnp.bfloat16

HEAD_DIM = 64
N_HEADS = 8
D_ATTN = N_HEADS * HEAD_DIM
CONV_WIDTH = 3
WINDOWS = (128, 512, 2048)
DILATIONS = (1, 4, 16)
RMS_EPS = 1e-6
ATTN_SCALE = HEAD_DIM ** -0.5
NEG_INF = -1e30

V7X_LANES = 128
V7X_SUBLANES = 8
V7X_VMEM_BYTES = 64 * 1024 * 1024

HEADS_PER_STEP = V7X_LANES // HEAD_DIM
TQ = 256


def _rms(x, g):
    y = x * lax.rsqrt(jnp.mean(x * x, axis=-1, keepdims=True) + RMS_EPS)
    return y * g


def _branch_count(delta):
    nonneg = delta >= 0
    c = jnp.zeros(delta.shape, F32)
    for w, d in zip(WINDOWS, DILATIONS):
        hit = nonneg & ((delta & (d - 1)) == 0) & (delta <= w)
        c = c + hit.astype(F32)
    return c


def _dot_nt(a, b):
    return lax.dot_general(a, b, (((1,), (1,)), ((), ())), preferred_element_type=F32)


def _project(x, g, w_qc):
    d_conv = (w_qc.shape[1] - D_ATTN) // 3
    xn = _rms(x, g).astype(BF16)
    z = jnp.dot(xn, w_qc, preferred_element_type=F32)
    q = z[:, 0:D_ATTN] * ATTN_SCALE
    hc = z[:, D_ATTN:D_ATTN + d_conv]
    gb = z[:, D_ATTN + d_conv:D_ATTN + 2 * d_conv]
    gc = z[:, D_ATTN + 2 * d_conv:D_ATTN + 3 * d_conv]
    return xn, q, hc, gb, gc


def _conv3(wc, u2, u1, u0):
    acc = wc[0:1, :] * u2
    acc = acc + wc[1:2, :] * u1
    return acc + wc[2:3, :] * u0


def _inproj_prompt_kernel(x_ref, g_ref, wqc_ref, wkvt_ref, wc_ref, q_ref, kt_ref, vt_ref, co_ref,
                          ulast_ref, uext_ref):
    tm = x_ref.shape[0]
    j = pl.program_id(1)
    tiles_per_seq = pl.num_programs(1)
    xn, q, hc, gb, gc = _project(x_ref[...], g_ref[...], wqc_ref[...])
    q_ref[...] = q.astype(q_ref.dtype)
    kvt = _dot_nt(wkvt_ref[...], xn)
    kt_ref[0] = kvt[0:D_ATTN, :]
    vt_ref[0] = kvt[D_ATTN:2 * D_ATTN, :]
    u = gc * hc

    @pl.when(j == 0)
    def _():
        uext_ref[0:V7X_SUBLANES, :] = jnp.zeros((V7X_SUBLANES, u.shape[1]), F32)

    uext_ref[V7X_SUBLANES:V7X_SUBLANES + tm, :] = u
    u1 = uext_ref[V7X_SUBLANES - 1:V7X_SUBLANES - 1 + tm, :]
    u2 = uext_ref[V7X_SUBLANES - 2:V7X_SUBLANES - 2 + tm, :]
    co_ref[...] = (gb * _conv3(wc_ref[...], u2, u1, u)).astype(co_ref.dtype)
    tail = u[tm - V7X_SUBLANES:tm, :]
    uext_ref[0:V7X_SUBLANES, :] = tail

    @pl.when(j == tiles_per_seq - 1)
    def _():
        ulast_ref[0] = tail


def _inproj_sample_kernel(x_ref, g_ref, wqc_ref, wkv_ref, wc_ref, p1_ref, p2_ref, q_ref, k_ref,
                          v_ref, co_ref, u_ref, uext_ref, *, dec_seq):
    rows = x_ref.shape[0]
    xn, q, hc, gb, gc = _project(x_ref[...], g_ref[...], wqc_ref[...])
    kv = jnp.dot(xn, wkv_ref[...], preferred_element_type=F32)
    q_ref[...] = q
    k_ref[...] = kv[:, 0:D_ATTN]
    v_ref[...] = kv[:, D_ATTN:2 * D_ATTN]
    u = gc * hc
    u_ref[...] = u
    uext_ref[0:V7X_SUBLANES, :] = jnp.zeros((V7X_SUBLANES, u.shape[1]), F32)
    uext_ref[V7X_SUBLANES:V7X_SUBLANES + rows, :] = u
    t = lax.broadcasted_iota(jnp.int32, u.shape, 0) % dec_seq
    u1 = jnp.where(t >= 1, uext_ref[V7X_SUBLANES - 1:V7X_SUBLANES - 1 + rows, :], p1_ref[...])
    u2 = jnp.where(t >= 2, uext_ref[V7X_SUBLANES - 2:V7X_SUBLANES - 2 + rows, :], p2_ref[...])
    co_ref[...] = (gb * _conv3(wc_ref[...], u2, u1, u)).astype(co_ref.dtype)


def _attn_tables(seq):
    i = lax.broadcasted_iota(jnp.int32, (TQ, seq), 0)
    x = lax.broadcasted_iota(jnp.int32, (TQ, seq), 1)
    delta = i + (seq - TQ) - x
    c = _branch_count(delta)
    logc = jnp.where(c > 0.0, jnp.log(jnp.maximum(c, 1.0)), NEG_INF)
    return delta.astype(F32), logc


def _attn_prompt_kernel(slopes_ref, dist_ref, logc_ref, q_ref, kt_ref, vt_ref, o_ref,
                        ktb_ref, vtb_ref, bm_ref):
    seq = q_ref.shape[0]
    pair = pl.program_id(1)
    slope = [slopes_ref[HEADS_PER_STEP * pair + h] for h in range(HEADS_PER_STEP)]
    lane = lax.broadcasted_iota(jnp.int32, (1, V7X_LANES), 1)
    first_head = lane < HEAD_DIM

    ktb_ref[...] = kt_ref[0].astype(BF16)
    vtb_ref[...] = vt_ref[0].astype(BF16)

    for h in range(HEADS_PER_STEP):
        bm_ref[h] = logc_ref[...] - slope[h] * dist_ref[...]

    for q0 in range(0, seq, TQ):
        kend = q0 + TQ
        col0 = seq - kend
        q = q_ref[q0:kend, :]
        zero = jnp.zeros_like(q)
        qh = [jnp.where(first_head, q, zero), jnp.where(first_head, zero, q)]
        outs = []
        for h in range(HEADS_PER_STEP):
            s = jnp.dot(qh[h], ktb_ref[:, 0:kend], preferred_element_type=F32)
            s = s + bm_ref[h, :, col0:seq]
            m = jnp.max(s, axis=-1, keepdims=True)
            p = jnp.exp(s - m)
            l = jnp.sum(p, axis=-1, keepdims=True)
            pv = _dot_nt(p.astype(BF16), vtb_ref[:, 0:kend])
            outs.append(pv / l)
        o_ref[q0:kend, :] = jnp.where(first_head, outs[0], outs[1]).astype(o_ref.dtype)


def _attn_sample_kernel(slopes_ref, q_ref, kn_ref, vn_ref, kst_ref, vst_ref, o_ref,
                        knew_ref, vnew_ref):
    dec_seq = q_ref.shape[1]
    n_past = kst_ref.shape[2]
    width = q_ref.shape[2]
    n_rows = dec_seq * N_HEADS

    q = q_ref[0]
    head_of_lane = lax.broadcasted_iota(jnp.int32, (N_HEADS, width), 1) // HEAD_DIM
    head_of_row = lax.broadcasted_iota(jnp.int32, (N_HEADS, width), 0)
    head_mask = head_of_lane == head_of_row
    wt = jnp.concatenate(
        [jnp.where(head_mask, jnp.broadcast_to(q[t:t + 1, :], (N_HEADS, width)), 0.0)
         for t in range(dec_seq)], axis=0).astype(BF16)

    knew_ref[...] = jnp.zeros(knew_ref.shape, F32)
    vnew_ref[...] = jnp.zeros(vnew_ref.shape, F32)
    knew_ref[0:dec_seq, :] = kn_ref[0]
    vnew_ref[0:dec_seq, :] = vn_ref[0]

    s_past = jnp.dot(wt, kst_ref[0].astype(BF16), preferred_element_type=F32)
    s_new = _dot_nt(wt, knew_ref[...].astype(BF16))

    row = lax.broadcasted_iota(jnp.int32, (n_rows, 1), 0)
    step = row // N_HEADS
    slope_col = jnp.zeros((n_rows, 1), F32)
    for h in range(N_HEADS):
        slope_col = jnp.where(row % N_HEADS == h, slopes_ref[h], slope_col)

    def weigh(s, key_pos):
        delta = n_past + step - key_pos
        c = _branch_count(delta)
        s = jnp.where(c > 0.0, s - slope_col * delta.astype(F32), NEG_INF)
        return s, c

    s_past, c_past = weigh(s_past, lax.broadcasted_iota(jnp.int32, (1, n_past), 1))
    s_new, c_new = weigh(s_new, n_past + lax.broadcasted_iota(jnp.int32, (1, V7X_LANES), 1))
    m = jnp.maximum(jnp.max(s_past, axis=-1, keepdims=True),
                    jnp.max(s_new, axis=-1, keepdims=True))
    p_past = c_past * jnp.exp(s_past - m)
    p_new = c_new * jnp.exp(s_new - m)
    l = jnp.sum(p_past, axis=-1, keepdims=True) + jnp.sum(p_new, axis=-1, keepdims=True)
    o_all = (_dot_nt(p_past.astype(BF16), vst_ref[0].astype(BF16))
             + jnp.dot(p_new.astype(BF16), vnew_ref[...].astype(BF16),
                       preferred_element_type=F32)) / l
    outs = []
    for t in range(dec_seq):
        blk = o_all[t * N_HEADS:(t + 1) * N_HEADS, :]
        outs.append(jnp.sum(jnp.where(head_mask, blk, 0.0), axis=0, keepdims=True))
    o_ref[0] = jnp.concatenate(outs, axis=0).astype(o_ref.dtype)


def _ffn_kernel(x_ref, a_ref, c_ref, wo_ref, g2_ref, wg_ref, wu_ref, wd_ref, gf_ref, y_ref, *,
                final_norm):
    mix = jnp.concatenate([a_ref[...].astype(BF16), c_ref[...].astype(BF16)], axis=-1)
    h = x_ref[...] + jnp.dot(mix, wo_ref[...], preferred_element_type=F32)
    hn = _rms(h, g2_ref[...]).astype(BF16)
    gate = jnp.dot(hn, wg_ref[...], preferred_element_type=F32)
    up = jnp.dot(hn, wu_ref[...], preferred_element_type=F32)
    act = (gate * (1.0 / (1.0 + jnp.exp(-gate)))) * up
    y = h + jnp.dot(act.astype(BF16), wd_ref[...], preferred_element_type=F32)
    y_ref[...] = _rms(y, gf_ref[...]) if final_norm else y


def _nbytes(shape, dtype):
    n = 1
    for s in shape:
        n *= s
    return n * jnp.dtype(dtype).itemsize


def _vmem_limit(pipelined, resident, temporaries):
    need = 2 * sum(pipelined) + sum(resident) + sum(temporaries)
    return min(V7X_VMEM_BYTES, need + need // 4)


def _resident(shape):
    zeros = (0,) * len(shape)
    return pl.BlockSpec(shape, lambda *_: zeros, pipeline_mode=pl.Buffered(1))


def _row_tile(rows):
    for tm in (512, 256, 128):
        if rows % tm == 0:
            return tm
    raise ValueError(f"row count {rows} is not a multiple of 128")


def _inproj_prompt(x, g, w_qc, w_kvt, wc, seq):
    rows, d_model = x.shape
    d_conv = wc.shape[1]
    tm = _row_tile(seq)
    tiles_per_seq = seq // tm
    n_seq = rows // seq
    row_blk = lambda width: pl.BlockSpec((tm, width), lambda b, j: (b * tiles_per_seq + j, 0))
    col_blk = pl.BlockSpec((1, D_ATTN, tm), lambda b, j: (b, 0, j))
    limit = _vmem_limit(
        [_nbytes((tm, d_model), F32), _nbytes((tm, D_ATTN), BF16), 2 * _nbytes((D_ATTN, tm), F32),
         _nbytes((tm, d_conv), BF16)],
        [_nbytes(w_qc.shape, BF16), _nbytes(w_kvt.shape, BF16)],
        [2 * _nbytes((tm, d_model), F32), 2 * _nbytes((tm, w_qc.shape[1]), F32),
         2 * _nbytes((w_kvt.shape[0], tm), F32), 4 * _nbytes((tm, d_conv), F32)])
    return pl.pallas_call(
        _inproj_prompt_kernel,
        grid=(n_seq, tiles_per_seq),
        in_specs=[row_blk(d_model), _resident((1, d_model)), _resident(w_qc.shape),
                  _resident(w_kvt.shape), _resident(wc.shape)],
        out_specs=[row_blk(D_ATTN), col_blk, col_blk, row_blk(d_conv),
                   pl.BlockSpec((1, V7X_SUBLANES, d_conv), lambda b, j: (b, 0, 0))],
        out_shape=[jax.ShapeDtypeStruct((rows, D_ATTN), BF16),
                   jax.ShapeDtypeStruct((n_seq, D_ATTN, seq), F32),
                   jax.ShapeDtypeStruct((n_seq, D_ATTN, seq), F32),
                   jax.ShapeDtypeStruct((rows, d_conv), BF16),
                   jax.ShapeDtypeStruct((n_seq, V7X_SUBLANES, d_conv), F32)],
        scratch_shapes=[pltpu.VMEM((tm + V7X_SUBLANES, d_conv), F32)],
        compiler_params=pltpu.CompilerParams(dimension_semantics=("arbitrary", "arbitrary"),
                                             vmem_limit_bytes=limit),
        name="inproj_prompt",
    )(x, g, w_qc, w_kvt, wc)


def _inproj_sample(x, g, w_qc, w_kv, wc, prev1, prev2, dec_seq):
    rows, d_model = x.shape
    d_conv = wc.shape[1]
    full = lambda shape: pl.BlockSpec(shape, lambda i: (0,) * len(shape))
    return pl.pallas_call(
        functools.partial(_inproj_sample_kernel, dec_seq=dec_seq),
        grid=(1,),
        in_specs=[full(x.shape), full((1, d_model)), full(w_qc.shape), full(w_kv.shape),
                  full(wc.shape), full(prev1.shape), full(prev2.shape)],
        out_specs=[full((rows, D_ATTN))] * 3 + [full((rows, d_conv))] * 2,
        out_shape=[jax.ShapeDtypeStruct((rows, D_ATTN), F32)] * 3
        + [jax.ShapeDtypeStruct((rows, d_conv), BF16), jax.ShapeDtypeStruct((rows, d_conv), F32)],
        scratch_shapes=[pltpu.VMEM((rows + V7X_SUBLANES, d_conv), F32)],
        compiler_params=pltpu.CompilerParams(dimension_semantics=("arbitrary",)),
        name="inproj_sample",
    )(x, g, w_qc, w_kv, wc, prev1, prev2)


def _attn_prompt(slopes, q, kt, vt, seq):
    rows = q.shape[0]
    n_seq = rows // seq
    q_blk = pl.BlockSpec((seq, V7X_LANES), lambda b, p: (b, p))
    kv_blk = pl.BlockSpec((1, V7X_LANES, seq), lambda b, p: (b, p, 0))
    dist, logc = _attn_tables(seq)
    table = _nbytes((TQ, seq), F32)
    limit = _vmem_limit(
        [2 * _nbytes((seq, V7X_LANES), F32), 2 * _nbytes((seq, V7X_LANES), BF16)],
        [2 * _nbytes((seq, V7X_LANES), BF16), (2 + HEADS_PER_STEP) * table],
        [3 * HEADS_PER_STEP * table])
    return pl.pallas_call(
        _attn_prompt_kernel,
        grid=(n_seq, N_HEADS // HEADS_PER_STEP),
        in_specs=[pl.BlockSpec(memory_space=pltpu.SMEM), _resident((TQ, seq)),
                  _resident((TQ, seq)), q_blk, kv_blk, kv_blk],
        out_specs=q_blk,
        out_shape=jax.ShapeDtypeStruct((rows, D_ATTN), BF16),
        scratch_shapes=[pltpu.VMEM((V7X_LANES, seq), BF16)] * 2
        + [pltpu.VMEM((HEADS_PER_STEP, TQ, seq), F32)],
        compiler_params=pltpu.CompilerParams(dimension_semantics=("arbitrary", "arbitrary"),
                                             vmem_limit_bytes=limit),
        name="attn_prompt",
    )(slopes, dist, logc, q, kt, vt)


def _attn_sample(slopes, q, kn, vn, kst, vst):
    n_seq, dec_seq, width = q.shape
    n_past = kst.shape[2]
    new_blk = pl.BlockSpec((1, dec_seq, width), lambda b: (b, 0, 0))
    past_blk = pl.BlockSpec((1, width, n_past), lambda b: (b, 0, 0))
    limit = _vmem_limit(
        [2 * _nbytes((n_past, width), F32)], [],
        [2 * _nbytes((n_past, width), BF16), 8 * _nbytes((dec_seq * N_HEADS, n_past), F32)])
    return pl.pallas_call(
        _attn_sample_kernel,
        grid=(n_seq,),
        in_specs=[pl.BlockSpec(memory_space=pltpu.SMEM), new_blk, new_blk, new_blk,
                  past_blk, past_blk],
        out_specs=new_blk,
        out_shape=jax.ShapeDtypeStruct((n_seq, dec_seq, width), F32),
        scratch_shapes=[pltpu.VMEM((V7X_LANES, width), F32)] * 2,
        compiler_params=pltpu.CompilerParams(dimension_semantics=("arbitrary",),
                                             vmem_limit_bytes=limit),
        name="attn_sample",
    )(slopes, q, kn, vn, kst, vst)


def _ffn(x, attn_o, conv_o, wo, g2, wg, wu, wd, gf, final_norm, tm):
    rows, d_model = x.shape
    d_ff = wg.shape[1]
    row_blk = lambda width: pl.BlockSpec((tm, width), lambda i: (i, 0))
    limit = _vmem_limit(
        [2 * _nbytes((tm, d_model), F32), _nbytes((tm, attn_o.shape[1]), attn_o.dtype),
         _nbytes((tm, conv_o.shape[1]), conv_o.dtype)],
        [_nbytes(w.shape, BF16) for w in (wo, wg, wu, wd)],
        [4 * _nbytes((tm, d_model), F32), 3 * _nbytes((tm, d_ff), F32)])
    return pl.pallas_call(
        functools.partial(_ffn_kernel, final_norm=final_norm),
        grid=(rows // tm,),
        in_specs=[row_blk(d_model), row_blk(attn_o.shape[1]), row_blk(conv_o.shape[1]),
                  _resident(wo.shape), _resident((1, d_model)), _resident(wg.shape),
                  _resident(wu.shape), _resident(wd.shape), _resident((1, d_model))],
        out_specs=row_blk(d_model),
        out_shape=jax.ShapeDtypeStruct((rows, d_model), F32),
        compiler_params=pltpu.CompilerParams(dimension_semantics=("arbitrary",),
                                             vmem_limit_bytes=limit),
        name="ffn",
    )(x, attn_o, conv_o, wo, g2, wg, wu, wd, gf)


def _alibi_slopes():
    return jnp.exp2(-8.0 * jnp.arange(1, N_HEADS + 1, dtype=F32) / N_HEADS)


def kernel(x_prompt, x_sample, state_attn_k, state_attn_v, state_conv, norm_mix_g, w_in, w_conv,
           w_out, norm_ffn_g, w_gate, w_up, w_down, norm_final_g):
    depth = w_in.shape[0]
    batch, seq, d_model = x_prompt.shape
    dec_batch, dec_seq, _ = x_sample.shape
    d_conv = w_conv.shape[2]
    n_past = state_attn_k.shape[2]
    assert n_past >= max(WINDOWS) and seq <= max(WINDOWS) and seq % TQ == 0
    assert CONV_WIDTH - 1 <= dec_seq <= V7X_LANES
    slopes = _alibi_slopes()
    gf = norm_final_g.reshape(1, d_model)

    yp = x_prompt.reshape(batch * seq, d_model)
    ys = x_sample.reshape(dec_batch * dec_seq, d_model)
    outs = [[] for _ in range(6)]
    for layer in range(depth):
        g1 = norm_mix_g[layer].reshape(1, d_model)
        g2 = norm_ffn_g[layer].reshape(1, d_model)
        w_in_bf = w_in[layer].astype(BF16)
        w_qc = jnp.concatenate([w_in_bf[:, :D_ATTN], w_in_bf[:, 3 * D_ATTN:]], axis=1)
        w_kv = w_in_bf[:, D_ATTN:3 * D_ATTN]
        weights = [w.astype(BF16) for w in (w_out[layer], w_gate[layer], w_up[layer], w_down[layer])]
        wc = w_conv[layer]

        q, kt, vt, conv_o, u_last = _inproj_prompt(yp, g1, w_qc, w_kv.T, wc, seq)
        attn_o = _attn_prompt(slopes, q, kt, vt, seq)
        y_layer = _ffn(yp, attn_o, conv_o, weights[0], g2, *weights[1:], gf,
                       layer == depth - 1, _row_tile(seq))
        to_heads = lambda a: a.reshape(batch, N_HEADS, HEAD_DIM, seq).transpose(0, 3, 1, 2)
        outs[0].append(to_heads(kt))
        outs[1].append(to_heads(vt))
        outs[2].append(u_last[:, V7X_SUBLANES - (CONV_WIDTH - 1):, :])
        yp = y_layer

        st = state_conv[layer]
        prev1 = jnp.pad(st[:, 1:2], ((0, 0), (0, dec_seq - 1), (0, 0)))
        prev2 = jnp.pad(st, ((0, 0), (0, dec_seq - 2), (0, 0)))
        qs, ks, vs, conv_s, u_s = _inproj_sample(
            ys, g1, w_qc, w_kv, wc, prev1.reshape(-1, d_conv), prev2.reshape(-1, d_conv), dec_seq)
        as3 = lambda a: a.reshape(dec_batch, dec_seq, D_ATTN)
        feature_major = lambda a: a.transpose(0, 2, 3, 1).reshape(dec_batch, D_ATTN, n_past)
        attn_s = _attn_sample(slopes, as3(qs), as3(ks), as3(vs),
                              feature_major(state_attn_k[layer]),
                              feature_major(state_attn_v[layer]))
        ys = _ffn(ys, attn_s.reshape(-1, D_ATTN), conv_s, weights[0], g2, *weights[1:], gf,
                  layer == depth - 1, ys.shape[0])
        outs[3].append(ks.reshape(dec_batch, dec_seq, N_HEADS, HEAD_DIM))
        outs[4].append(vs.reshape(dec_batch, dec_seq, N_HEADS, HEAD_DIM))
        outs[5].append(u_s.reshape(dec_batch, dec_seq, d_conv)[:, dec_seq - (CONV_WIDTH - 1):])

    y_prompt = yp.reshape(batch, seq, d_model)
    y_sample = ys.reshape(dec_batch, dec_seq, d_model)
    new_k_p, new_v_p, new_c_p, new_k_s, new_v_s, new_c_s = [jnp.stack(o) for o in outs]
    return (y_prompt, y_sample, new_k_p, new_v_p, new_c_p, new_k_s, new_v_s, new_c_s)
```

```python
import functools

import jax
import jax.numpy as jnp
from jax import lax
from jax.experimental import pallas as pl
from jax.experimental.pallas import tpu as pltpu

F32 = jnp.float32
BF16 = jnp.bfloat16

HEAD_DIM = 64
N_HEADS = 8
D_ATTN = N_HEADS * HEAD_DIM
CONV_WIDTH = 3
WINDOWS = (128, 512, 2048)
DILATIONS = (1, 4, 16)
RMS_EPS = 1e-6
ATTN_SCALE = HEAD_DIM ** -0.5
NEG_INF = -1e30

V7X_LANES = 128
V7X_SUBLANES = 8
V7X_VMEM_BYTES = 64 * 1024 * 1024

HEADS_PER_STEP = V7X_LANES // HEAD_DIM
TQ = 256
TK = 256
ROW_GROUPS = 2


def _rms(x, g):
    y = x * lax.rsqrt(jnp.mean(x * x, axis=-1, keepdims=True) + RMS_EPS)
    return y * g


def _branch_count(delta):
    nonneg = delta >= 0
    c = jnp.zeros(delta.shape, F32)
    for w, d in zip(WINDOWS, DILATIONS):
        hit = nonneg & ((delta & (d - 1)) == 0) & (delta <= w)
        c = c + hit.astype(F32)
    return c


def _dot_nt(a, b):
    return lax.dot_general(a, b, (((1,), (1,)), ((), ())), preferred_element_type=F32)


def _project(x, g, wqc_ref):
    d_conv = (wqc_ref.shape[1] - D_ATTN) // 3
    xn = _rms(x, g).astype(BF16)
    zc = jnp.dot(xn, wqc_ref[:, D_ATTN:], preferred_element_type=F32)
    hc = zc[:, 0:d_conv]
    gb = zc[:, d_conv:2 * d_conv]
    gc = zc[:, 2 * d_conv:3 * d_conv]
    q = jnp.dot(xn, wqc_ref[:, 0:D_ATTN], preferred_element_type=F32) * ATTN_SCALE
    return xn, q, hc, gb, gc


def _conv3(wc, u2, u1, u0):
    acc = wc[0:1, :] * u2
    acc = acc + wc[1:2, :] * u1
    return acc + wc[2:3, :] * u0


def _inproj_prompt_kernel(x_ref, g_ref, wqc_ref, wkvt_ref, wc_ref, q_ref, kt_ref, vt_ref, co_ref,
                          ulast_ref, uext_ref):
    tm = x_ref.shape[0]
    j = pl.program_id(1)
    tiles_per_seq = pl.num_programs(1)

    @pl.when(j == 0)
    def _():
        uext_ref[0:V7X_SUBLANES, :] = jnp.zeros((V7X_SUBLANES, uext_ref.shape[1]), F32)

    rg = tm // ROW_GROUPS
    for r0 in range(0, tm, rg):
        xn, q, hc, gb, gc = _project(x_ref[r0:r0 + rg, :], g_ref[...], wqc_ref)
        q_ref[r0:r0 + rg, :] = q.astype(q_ref.dtype)
        kvt = _dot_nt(wkvt_ref[...], xn)
        kt_ref[0, :, r0:r0 + rg] = kvt[0:D_ATTN, :]
        vt_ref[0, :, r0:r0 + rg] = kvt[D_ATTN:2 * D_ATTN, :]
        u = gc * hc
        uext_ref[V7X_SUBLANES:V7X_SUBLANES + rg, :] = u
        u1 = uext_ref[V7X_SUBLANES - 1:V7X_SUBLANES - 1 + rg, :]
        u2 = uext_ref[V7X_SUBLANES - 2:V7X_SUBLANES - 2 + rg, :]
        co_ref[r0:r0 + rg, :] = (gb * _conv3(wc_ref[...], u2, u1, u)).astype(co_ref.dtype)
        tail = u[rg - V7X_SUBLANES:rg, :]
        uext_ref[0:V7X_SUBLANES, :] = tail

    @pl.when(j == tiles_per_seq - 1)
    def _():
        ulast_ref[0] = tail


def _inproj_sample_kernel(x_ref, g_ref, wqc_ref, wkv_ref, wc_ref, p1_ref, p2_ref, q_ref, k_ref,
                          v_ref, co_ref, u_ref, uext_ref, *, dec_seq):
    rows = x_ref.shape[0]
    xn, q, hc, gb, gc = _project(x_ref[...], g_ref[...], wqc_ref)
    kv = jnp.dot(xn, wkv_ref[...], preferred_element_type=F32)
    q_ref[...] = q
    k_ref[...] = kv[:, 0:D_ATTN]
    v_ref[...] = kv[:, D_ATTN:2 * D_ATTN]
    u = gc * hc
    u_ref[...] = u
    uext_ref[0:V7X_SUBLANES, :] = jnp.zeros((V7X_SUBLANES, u.shape[1]), F32)
    uext_ref[V7X_SUBLANES:V7X_SUBLANES + rows, :] = u
    t = lax.broadcasted_iota(jnp.int32, u.shape, 0) % dec_seq
    u1 = jnp.where(t >= 1, uext_ref[V7X_SUBLANES - 1:V7X_SUBLANES - 1 + rows, :], p1_ref[...])
    u2 = jnp.where(t >= 2, uext_ref[V7X_SUBLANES - 2:V7X_SUBLANES - 2 + rows, :], p2_ref[...])
    co_ref[...] = (gb * _conv3(wc_ref[...], u2, u1, u)).astype(co_ref.dtype)


def _attn_tables(seq):
    x = lax.broadcasted_iota(jnp.int32, (seq, TQ), 0)
    i = lax.broadcasted_iota(jnp.int32, (seq, TQ), 1)
    delta = i + (seq - TQ) - x
    c = _branch_count(delta)
    logc = jnp.where(c > 0.0, jnp.log(jnp.maximum(c, 1.0)), NEG_INF)
    return delta.astype(F32), logc


def _attn_prompt_kernel(slopes_ref, dist_ref, logc_ref, q_ref, kt_ref, vt_ref, o_ref,
                        kb_ref, vtb_ref, bm_ref):
    seq = q_ref.shape[0]
    pair = pl.program_id(1)
    slope = [slopes_ref[HEADS_PER_STEP * pair + h] for h in range(HEADS_PER_STEP)]
    lane = lax.broadcasted_iota(jnp.int32, (1, V7X_LANES), 1)
    first_head = lane < HEAD_DIM

    kb_ref[...] = kt_ref[0].T.astype(BF16)
    vtb_ref[...] = vt_ref[0].astype(BF16)

    for h in range(HEADS_PER_STEP):
        bm_ref[h] = logc_ref[...] - slope[h] * dist_ref[...]

    groups = TK // V7X_SUBLANES

    for q0 in range(0, seq, TQ):
        kend = q0 + TQ
        row0 = seq - kend
        q = q_ref[q0:kend, :]
        zero = jnp.zeros_like(q)
        qh = [jnp.where(first_head, q, zero), jnp.where(first_head, zero, q)]
        scores, maxes = [], []
        for h in range(HEADS_PER_STEP):
            chunks, m8 = [], None
            for k0 in range(0, kend, TK):
                s = _dot_nt(kb_ref[k0:k0 + TK, :], qh[h]) + bm_ref[h, row0 + k0:row0 + k0 + TK, :]
                chunks.append(s)
                c8 = jnp.max(s.reshape(groups, V7X_SUBLANES, TQ), axis=0)
                m8 = c8 if m8 is None else jnp.maximum(m8, c8)
            scores.append(chunks)
            maxes.append(jnp.max(m8, axis=0, keepdims=True))
        outs = []
        for h in range(HEADS_PER_STEP):
            l8 = jnp.zeros((V7X_SUBLANES, TQ), F32)
            acc = jnp.zeros((HEAD_DIM, TQ), F32)
            for c, k0 in enumerate(range(0, kend, TK)):
                p = jnp.exp(scores[h][c] - maxes[h])
                l8 = l8 + jnp.sum(p.reshape(groups, V7X_SUBLANES, TQ), axis=0)
                vt_h = vtb_ref[h * HEAD_DIM:(h + 1) * HEAD_DIM, k0:k0 + TK]
                acc = acc + jnp.dot(vt_h, p.astype(BF16), preferred_element_type=F32)
            outs.append(acc / jnp.sum(l8, axis=0, keepdims=True))
        o_ref[q0:kend, :] = jnp.concatenate(outs, axis=0).T.astype(o_ref.dtype)


def _attn_sample_kernel(slopes_ref, q_ref, kn_ref, vn_ref, kst_ref, vst_ref, o_ref,
                        knew_ref, vnew_ref):
    dec_seq = q_ref.shape[1]
    n_past = kst_ref.shape[2]
    width = q_ref.shape[2]
    n_rows = dec_seq * N_HEADS

    q = q_ref[0]
    head_of_lane = lax.broadcasted_iota(jnp.int32, (N_HEADS, width), 1) // HEAD_DIM
    head_of_row = lax.broadcasted_iota(jnp.int32, (N_HEADS, width), 0)
    head_mask = head_of_lane == head_of_row
    wt = jnp.concatenate(
        [jnp.where(head_mask, jnp.broadcast_to(q[t:t + 1, :], (N_HEADS, width)), 0.0)
         for t in range(dec_seq)], axis=0).astype(BF16)

    knew_ref[...] = jnp.zeros(knew_ref.shape, F32)
    vnew_ref[...] = jnp.zeros(vnew_ref.shape, F32)
    knew_ref[0:dec_seq, :] = kn_ref[0]
    vnew_ref[0:dec_seq, :] = vn_ref[0]

    s_past = jnp.dot(wt, kst_ref[0].astype(BF16), preferred_element_type=F32)
    s_new = _dot_nt(wt, knew_ref[...].astype(BF16))

    row = lax.broadcasted_iota(jnp.int32, (n_rows, 1), 0)
    step = row // N_HEADS
    slope_col = jnp.zeros((n_rows, 1), F32)
    for h in range(N_HEADS):
        slope_col = jnp.where(row % N_HEADS == h, slopes_ref[h], slope_col)

    def weigh(s, key_pos):
        delta = n_past + step - key_pos
        c = _branch_count(delta)
        s = jnp.where(c > 0.0, s - slope_col * delta.astype(F32), NEG_INF)
        return s, c

    s_past, c_past = weigh(s_past, lax.broadcasted_iota(jnp.int32, (1, n_past), 1))
    s_new, c_new = weigh(s_new, n_past + lax.broadcasted_iota(jnp.int32, (1, V7X_LANES), 1))
    m = jnp.maximum(jnp.max(s_past, axis=-1, keepdims=True),
                    jnp.max(s_new, axis=-1, keepdims=True))
    p_past = c_past * jnp.exp(s_past - m)
    p_new = c_new * jnp.exp(s_new - m)
    l = jnp.sum(p_past, axis=-1, keepdims=True) + jnp.sum(p_new, axis=-1, keepdims=True)
    o_all = (_dot_nt(p_past.astype(BF16), vst_ref[0].astype(BF16))
             + jnp.dot(p_new.astype(BF16), vnew_ref[...].astype(BF16),
                       preferred_element_type=F32)) / l
    outs = []
    for t in range(dec_seq):
        blk = o_all[t * N_HEADS:(t + 1) * N_HEADS, :]
        outs.append(jnp.sum(jnp.where(head_mask, blk, 0.0), axis=0, keepdims=True))
    o_ref[0] = jnp.concatenate(outs, axis=0).astype(o_ref.dtype)


def _ffn_kernel(x_ref, a_ref, c_ref, wo_ref, g2_ref, wg_ref, wu_ref, wd_ref, gf_ref, y_ref, *,
                final_norm):
    mix = jnp.concatenate([a_ref[...].astype(BF16), c_ref[...].astype(BF16)], axis=-1)
    h = x_ref[...] + jnp.dot(mix, wo_ref[...], preferred_element_type=F32)
    hn = _rms(h, g2_ref[...]).astype(BF16)
    gate = jnp.dot(hn, wg_ref[...], preferred_element_type=F32)
    up = jnp.dot(hn, wu_ref[...], preferred_element_type=F32)
    act = (gate * (1.0 / (1.0 + jnp.exp(-gate)))) * up
    y = h + jnp.dot(act.astype(BF16), wd_ref[...], preferred_element_type=F32)
    y_ref[...] = _rms(y, gf_ref[...]) if final_norm else y


def _nbytes(shape, dtype):
    n = 1
    for s in shape:
        n *= s
    return n * jnp.dtype(dtype).itemsize


def _vmem_limit(pipelined, resident, temporaries):
    need = 2 * sum(pipelined) + sum(resident) + sum(temporaries)
    return min(V7X_VMEM_BYTES, need + need // 4)


def _resident(shape):
    zeros = (0,) * len(shape)
    return pl.BlockSpec(shape, lambda *_: zeros, pipeline_mode=pl.Buffered(1))


def _row_tile(rows, largest=512):
    for tm in (largest, 512, 256, 128):
        if rows % tm == 0:
            return tm
    raise ValueError(f"row count {rows} is not a multiple of 128")


def _inproj_prompt(x, g, w_qc, w_kvt, wc, seq):
    rows, d_model = x.shape
    d_conv = wc.shape[1]
    tm = _row_tile(seq, largest=ROW_GROUPS * 512)
    tiles_per_seq = seq // tm
    n_seq = rows // seq
    row_blk = lambda width: pl.BlockSpec((tm, width), lambda b, j: (b * tiles_per_seq + j, 0))
    col_blk = pl.BlockSpec((1, D_ATTN, tm), lambda b, j: (b, 0, j))
    limit = _vmem_limit(
        [_nbytes((tm, d_model), F32), _nbytes((tm, D_ATTN), BF16), 2 * _nbytes((D_ATTN, tm), F32),
         _nbytes((tm, d_conv), BF16)],
        [_nbytes(w_qc.shape, BF16), _nbytes(w_kvt.shape, BF16)],
        [2 * _nbytes((tm, d_model), F32), 2 * _nbytes((tm, w_qc.shape[1]), F32),
         2 * _nbytes((w_kvt.shape[0], tm), F32), 4 * _nbytes((tm, d_conv), F32)])
    return pl.pallas_call(
        _inproj_prompt_kernel,
        grid=(n_seq, tiles_per_seq),
        in_specs=[row_blk(d_model), _resident((1, d_model)), _resident(w_qc.shape),
                  _resident(w_kvt.shape), _resident(wc.shape)],
        out_specs=[row_blk(D_ATTN), col_blk, col_blk, row_blk(d_conv),
                   pl.BlockSpec((1, V7X_SUBLANES, d_conv), lambda b, j: (b, 0, 0))],
        out_shape=[jax.ShapeDtypeStruct((rows, D_ATTN), BF16),
                   jax.ShapeDtypeStruct((n_seq, D_ATTN, seq), F32),
                   jax.ShapeDtypeStruct((n_seq, D_ATTN, seq), F32),
                   jax.ShapeDtypeStruct((rows, d_conv), BF16),
                   jax.ShapeDtypeStruct((n_seq, V7X_SUBLANES, d_conv), F32)],
        scratch_shapes=[pltpu.VMEM((tm + V7X_SUBLANES, d_conv), F32)],
        compiler_params=pltpu.CompilerParams(dimension_semantics=("arbitrary", "arbitrary"),
                                             vmem_limit_bytes=limit),
        name="inproj_prompt",
    )(x, g, w_qc, w_kvt, wc)


def _inproj_sample(x, g, w_qc, w_kv, wc, prev1, prev2, dec_seq):
    rows, d_model = x.shape
    d_conv = wc.shape[1]
    full = lambda shape: pl.BlockSpec(shape, lambda i: (0,) * len(shape))
    return pl.pallas_call(
        functools.partial(_inproj_sample_kernel, dec_seq=dec_seq),
        grid=(1,),
        in_specs=[full(x.shape), full((1, d_model)), full(w_qc.shape), full(w_kv.shape),
                  full(wc.shape), full(prev1.shape), full(prev2.shape)],
        out_specs=[full((rows, D_ATTN))] * 3 + [full((rows, d_conv))] * 2,
        out_shape=[jax.ShapeDtypeStruct((rows, D_ATTN), F32)] * 3
        + [jax.ShapeDtypeStruct((rows, d_conv), BF16), jax.ShapeDtypeStruct((rows, d_conv), F32)],
        scratch_shapes=[pltpu.VMEM((rows + V7X_SUBLANES, d_conv), F32)],
        compiler_params=pltpu.CompilerParams(dimension_semantics=("arbitrary",)),
        name="inproj_sample",
    )(x, g, w_qc, w_kv, wc, prev1, prev2)


def _attn_prompt(slopes, q, kt, vt, seq):
    rows = q.shape[0]
    n_seq = rows // seq
    q_blk = pl.BlockSpec((seq, V7X_LANES), lambda b, p: (b, p))
    kv_blk = pl.BlockSpec((1, V7X_LANES, seq), lambda b, p: (b, p, 0))
    dist, logc = _attn_tables(seq)
    table = _nbytes((TQ, seq), F32)
    limit = _vmem_limit(
        [2 * _nbytes((seq, V7X_LANES), F32), 2 * _nbytes((seq, V7X_LANES), BF16)],
        [2 * _nbytes((seq, V7X_LANES), BF16), (2 + HEADS_PER_STEP) * table],
        [3 * HEADS_PER_STEP * table])
    return pl.pallas_call(
        _attn_prompt_kernel,
        grid=(n_seq, N_HEADS // HEADS_PER_STEP),
        in_specs=[pl.BlockSpec(memory_space=pltpu.SMEM), _resident((seq, TQ)),
                  _resident((seq, TQ)), q_blk, kv_blk, kv_blk],
        out_specs=q_blk,
        out_shape=jax.ShapeDtypeStruct((rows, D_ATTN), BF16),
        scratch_shapes=[pltpu.VMEM((seq, V7X_LANES), BF16), pltpu.VMEM((V7X_LANES, seq), BF16),
                        pltpu.VMEM((HEADS_PER_STEP, seq, TQ), F32)],
        compiler_params=pltpu.CompilerParams(dimension_semantics=("arbitrary", "arbitrary"),
                                             vmem_limit_bytes=limit),
        name="attn_prompt",
    )(slopes, dist, logc, q, kt, vt)


def _attn_sample(slopes, q, kn, vn, kst, vst):
    n_seq, dec_seq, width = q.shape
    n_past = kst.shape[2]
    new_blk = pl.BlockSpec((1, dec_seq, width), lambda b: (b, 0, 0))
    past_blk = pl.BlockSpec((1, width, n_past), lambda b: (b, 0, 0))
    limit = _vmem_limit(
        [2 * _nbytes((n_past, width), F32)], [],
        [2 * _nbytes((n_past, width), BF16), 8 * _nbytes((dec_seq * N_HEADS, n_past), F32)])
    return pl.pallas_call(
        _attn_sample_kernel,
        grid=(n_seq,),
        in_specs=[pl.BlockSpec(memory_space=pltpu.SMEM), new_blk, new_blk, new_blk,
                  past_blk, past_blk],
        out_specs=new_blk,
        out_shape=jax.ShapeDtypeStruct((n_seq, dec_seq, width), F32),
        scratch_shapes=[pltpu.VMEM((V7X_LANES, width), F32)] * 2,
        compiler_params=pltpu.CompilerParams(dimension_semantics=("arbitrary",),
                                             vmem_limit_bytes=limit),
        name="attn_sample",
    )(slopes, q, kn, vn, kst, vst)


def _ffn(x, attn_o, conv_o, wo, g2, wg, wu, wd, gf, final_norm, tm):
    rows, d_model = x.shape
    d_ff = wg.shape[1]
    row_blk = lambda width: pl.BlockSpec((tm, width), lambda i: (i, 0))
    limit = _vmem_limit(
        [2 * _nbytes((tm, d_model), F32), _nbytes((tm, attn_o.shape[1]), attn_o.dtype),
         _nbytes((tm, conv_o.shape[1]), conv_o.dtype)],
        [_nbytes(w.shape, BF16) for w in (wo, wg, wu, wd)],
        [4 * _nbytes((tm, d_model), F32), 3 * _nbytes((tm, d_ff), F32)])
    return pl.pallas_call(
        functools.partial(_ffn_kernel, final_norm=final_norm),
        grid=(rows // tm,),
        in_specs=[row_blk(d_model), row_blk(attn_o.shape[1]), row_blk(conv_o.shape[1]),
                  _resident(wo.shape), _resident((1, d_model)), _resident(wg.shape),
                  _resident(wu.shape), _resident(wd.shape), _resident((1, d_model))],
        out_specs=row_blk(d_model),
        out_shape=jax.ShapeDtypeStruct((rows, d_model), F32),
        compiler_params=pltpu.CompilerParams(dimension_semantics=("arbitrary",),
                                             vmem_limit_bytes=limit),
        name="ffn",
    )(x, attn_o, conv_o, wo, g2, wg, wu, wd, gf)


def _alibi_slopes():
    return jnp.exp2(-8.0 * jnp.arange(1, N_HEADS + 1, dtype=F32) / N_HEADS)


def kernel(x_prompt, x_sample, state_attn_k, state_attn_v, state_conv, norm_mix_g, w_in, w_conv,
           w_out, norm_ffn_g, w_gate, w_up, w_down, norm_final_g):
    depth = w_in.shape[0]
    batch, seq, d_model = x_prompt.shape
    dec_batch, dec_seq, _ = x_sample.shape
    d_conv = w_conv.shape[2]
    n_past = state_attn_k.shape[2]
    assert n_past >= max(WINDOWS) and seq <= max(WINDOWS) and seq % TQ == 0
    assert CONV_WIDTH - 1 <= dec_seq <= V7X_LANES
    slopes = _alibi_slopes()
    gf = norm_final_g.reshape(1, d_model)

    yp = x_prompt.reshape(batch * seq, d_model)
    ys = x_sample.reshape(dec_batch * dec_seq, d_model)
    outs = [[] for _ in range(6)]
    for layer in range(depth):
        g1 = norm_mix_g[layer].reshape(1, d_model)
        g2 = norm_ffn_g[layer].reshape(1, d_model)
        w_in_bf = w_in[layer].astype(BF16)
        w_qc = jnp.concatenate([w_in_bf[:, :D_ATTN], w_in_bf[:, 3 * D_ATTN:]], axis=1)
        w_kv = w_in_bf[:, D_ATTN:3 * D_ATTN]
        weights = [w.astype(BF16) for w in (w_out[layer], w_gate[layer], w_up[layer], w_down[layer])]
        wc = w_conv[layer]

        q, kt, vt, conv_o, u_last = _inproj_prompt(yp, g1, w_qc, w_kv.T, wc, seq)
        attn_o = _attn_prompt(slopes, q, kt, vt, seq)
        y_layer = _ffn(yp, attn_o, conv_o, weights[0], g2, *weights[1:], gf,
                       layer == depth - 1, _row_tile(seq))
        to_heads = lambda a: a.reshape(batch, N_HEADS, HEAD_DIM, seq).transpose(0, 3, 1, 2)
        outs[0].append(to_heads(kt))
        outs[1].append(to_heads(vt))
        outs[2].append(u_last[:, V7X_SUBLANES - (CONV_WIDTH - 1):, :])
        yp = y_layer

        st = state_conv[layer]
        prev1 = jnp.pad(st[:, 1:2], ((0, 0), (0, dec_seq - 1), (0, 0)))
        prev2 = jnp.pad(st, ((0, 0), (0, dec_seq - 2), (0, 0)))
        qs, ks, vs, conv_s, u_s = _inproj_sample(
            ys, g1, w_qc, w_kv, wc, prev1.reshape(-1, d_conv), prev2.reshape(-1, d_conv), dec_seq)
        as3 = lambda a: a.reshape(dec_batch, dec_seq, D_ATTN)
        feature_major = lambda a: a.transpose(0, 2, 3, 1).reshape(dec_batch, D_ATTN, n_past)
        attn_s = _attn_sample(slopes, as3(qs), as3(ks), as3(vs),
                              feature_major(state_attn_k[layer]),
                              feature_major(state_attn_v[layer]))
        ys = _ffn(ys, attn_s.reshape(-1, D_ATTN), conv_s, weights[0], g2, *weights[1:], gf,
                  layer == depth - 1, ys.shape[0])
        outs[3].append(ks.reshape(dec_batch, dec_seq, N_HEADS, HEAD_DIM))
        outs[4].append(vs.reshape(dec_batch, dec_seq, N_HEADS, HEAD_DIM))
        outs[5].append(u_s.reshape(dec_batch, dec_seq, d_conv)[:, dec_seq - (CONV_WIDTH - 1):])

    y_prompt = yp.reshape(batch, seq, d_model)
    y_sample = ys.reshape(dec_batch, dec_seq, d_model)
    new_k_p, new_v_p, new_c_p, new_k_s, new_v_s, new_c_s = [jnp.stack(o) for o in outs]
    return (y_prompt, y_sample, new_k_p, new_v_p, new_c_p, new_k_s, new_v_s, new_c_s)
```

```python
import functools

import jax
import jax.numpy as jnp
from jax import lax
from jax.experimental import pallas as pl
from jax.experimental.pallas import tpu as pltpu

F32 = jnp.float32
BF16 = jnp.bfloat16

HEAD_DIM = 64
N_HEADS = 8
D_ATTN = N_HEADS * HEAD_DIM
CONV_WIDTH = 3
WINDOWS = (128, 512, 2048)
DILATIONS = (1, 4, 16)
RMS_EPS = 1e-6
ATTN_SCALE = HEAD_DIM ** -0.5
LOG2_E = 1.4426950408889634
NEG_INF = -1e30

V7X_LANES = 128
V7X_SUBLANES = 8
V7X_VMEM_BYTES = 64 * 1024 * 1024

HEADS_PER_STEP = V7X_LANES // HEAD_DIM
TQ = 256
TK = 256
ATTN_STREAMS = 4
ROW_GROUPS = 2


def _rms(x, g):
    y = x * lax.rsqrt(jnp.mean(x * x, axis=-1, keepdims=True) + RMS_EPS)
    return y * g


def _branch_count(delta):
    nonneg = delta >= 0
    c = jnp.zeros(delta.shape, F32)
    for w, d in zip(WINDOWS, DILATIONS):
        hit = nonneg & ((delta & (d - 1)) == 0) & (delta <= w)
        c = c + hit.astype(F32)
    return c


def _dot_nt(a, b):
    return lax.dot_general(a, b, (((1,), (1,)), ((), ())), preferred_element_type=F32)


def _project(x, g, wqc_ref, q_scale):
    d_conv = (wqc_ref.shape[1] - D_ATTN) // 3
    xn = _rms(x, g).astype(BF16)
    zc = jnp.dot(xn, wqc_ref[:, D_ATTN:], preferred_element_type=F32)
    hc = zc[:, 0:d_conv]
    gb = zc[:, d_conv:2 * d_conv]
    gc = zc[:, 2 * d_conv:3 * d_conv]
    q = jnp.dot(xn, wqc_ref[:, 0:D_ATTN], preferred_element_type=F32) * q_scale
    return xn, q, hc, gb, gc


def _conv3(wc, u2, u1, u0):
    acc = wc[0:1, :] * u2
    acc = acc + wc[1:2, :] * u1
    return acc + wc[2:3, :] * u0


def _inproj_prompt_kernel(x_ref, g_ref, wqc_ref, wkvt_ref, wc_ref, q_ref, kt_ref, vt_ref, co_ref,
                          ulast_ref, uext_ref):
    tm = x_ref.shape[0]
    j = pl.program_id(1)
    tiles_per_seq = pl.num_programs(1)

    @pl.when(j == 0)
    def _():
        uext_ref[0:V7X_SUBLANES, :] = jnp.zeros((V7X_SUBLANES, uext_ref.shape[1]), F32)

    rg = tm // ROW_GROUPS
    for r0 in range(0, tm, rg):
        xn, q, hc, gb, gc = _project(x_ref[r0:r0 + rg, :], g_ref[...], wqc_ref,
                                     ATTN_SCALE * LOG2_E)
        q_ref[r0:r0 + rg, :] = q.astype(q_ref.dtype)
        kvt = _dot_nt(wkvt_ref[...], xn)
        kt_ref[0, :, r0:r0 + rg] = kvt[0:D_ATTN, :]
        vt_ref[0, :, r0:r0 + rg] = kvt[D_ATTN:2 * D_ATTN, :]
        u = gc * hc
        uext_ref[V7X_SUBLANES:V7X_SUBLANES + rg, :] = u
        u1 = uext_ref[V7X_SUBLANES - 1:V7X_SUBLANES - 1 + rg, :]
        u2 = uext_ref[V7X_SUBLANES - 2:V7X_SUBLANES - 2 + rg, :]
        co_ref[r0:r0 + rg, :] = (gb * _conv3(wc_ref[...], u2, u1, u)).astype(co_ref.dtype)
        tail = u[rg - V7X_SUBLANES:rg, :]
        uext_ref[0:V7X_SUBLANES, :] = tail

    @pl.when(j == tiles_per_seq - 1)
    def _():
        ulast_ref[0] = tail


def _inproj_sample_kernel(x_ref, g_ref, wqc_ref, wkv_ref, wc_ref, p1_ref, p2_ref, q_ref, k_ref,
                          v_ref, co_ref, u_ref, uext_ref, *, dec_seq):
    rows = x_ref.shape[0]
    xn, q, hc, gb, gc = _project(x_ref[...], g_ref[...], wqc_ref, ATTN_SCALE)
    kv = jnp.dot(xn, wkv_ref[...], preferred_element_type=F32)
    q_ref[...] = q
    k_ref[...] = kv[:, 0:D_ATTN]
    v_ref[...] = kv[:, D_ATTN:2 * D_ATTN]
    u = gc * hc
    u_ref[...] = u
    uext_ref[0:V7X_SUBLANES, :] = jnp.zeros((V7X_SUBLANES, u.shape[1]), F32)
    uext_ref[V7X_SUBLANES:V7X_SUBLANES + rows, :] = u
    t = lax.broadcasted_iota(jnp.int32, u.shape, 0) % dec_seq
    u1 = jnp.where(t >= 1, uext_ref[V7X_SUBLANES - 1:V7X_SUBLANES - 1 + rows, :], p1_ref[...])
    u2 = jnp.where(t >= 2, uext_ref[V7X_SUBLANES - 2:V7X_SUBLANES - 2 + rows, :], p2_ref[...])
    co_ref[...] = (gb * _conv3(wc_ref[...], u2, u1, u)).astype(co_ref.dtype)


def _attn_tables(seq):
    x = lax.broadcasted_iota(jnp.int32, (seq, TQ), 0)
    i = lax.broadcasted_iota(jnp.int32, (seq, TQ), 1)
    delta = i + (seq - TQ) - x
    c = _branch_count(delta)
    logc = jnp.where(c > 0.0, jnp.log2(jnp.maximum(c, 1.0)), NEG_INF)
    return delta.astype(F32), logc


def _attn_prompt_kernel(slopes_ref, dist_ref, logc_ref, q_ref, kt_ref, vt_ref, o_ref,
                        kb_ref, vtb_ref, bm_ref):
    seq = q_ref.shape[0]
    pair = pl.program_id(1)
    slope = [slopes_ref[HEADS_PER_STEP * pair + h] for h in range(HEADS_PER_STEP)]
    lane = lax.broadcasted_iota(jnp.int32, (1, V7X_LANES), 1)
    first_head = lane < HEAD_DIM

    kb_ref[...] = kt_ref[0].T.astype(BF16)
    vtb_ref[...] = vt_ref[0].astype(BF16)

    for h in range(HEADS_PER_STEP):
        bm_ref[h] = logc_ref[...] - (slope[h] * LOG2_E) * dist_ref[...]

    groups = TK // V7X_SUBLANES

    def score_half(st, k0):
        h, row0 = st["h"], seq - (st["q0"] + TQ)
        s = _dot_nt(kb_ref[k0:k0 + TK, :], st["q"]) + bm_ref[h, row0 + k0:row0 + k0 + TK, :]
        c8 = jnp.max(s.reshape(groups, V7X_SUBLANES, TQ), axis=0)
        m_new = jnp.maximum(st["m"], jnp.max(c8, axis=0, keepdims=True))
        pending = dict(k0=k0, s=s, m=m_new, alpha=jnp.exp2(st["m"] - m_new))
        st["m"] = m_new
        return pending

    def value_half(st, pending):
        h, k0 = st["h"], pending["k0"]
        p = jnp.exp2(pending["s"] - pending["m"])
        st["l8"] = (pending["alpha"] * st["l8"]
                    + jnp.sum(p.reshape(groups, V7X_SUBLANES, TQ), axis=0))
        vt_h = vtb_ref[h * HEAD_DIM:(h + 1) * HEAD_DIM, k0:k0 + TK]
        st["acc"] = pending["alpha"] * st["acc"] + jnp.dot(vt_h, p.astype(BF16),
                                                           preferred_element_type=F32)

    blocks_per_group = ATTN_STREAMS // HEADS_PER_STEP
    for g0 in range(0, seq, TQ * blocks_per_group):
        streams = []
        for q0 in range(g0, g0 + TQ * blocks_per_group, TQ):
            q = q_ref[q0:q0 + TQ, :]
            zero = jnp.zeros_like(q)
            for h, qh in enumerate([jnp.where(first_head, q, zero), jnp.where(first_head, zero, q)]):
                streams.append(dict(q0=q0, h=h, q=qh, m=jnp.full((1, TQ), NEG_INF, F32),
                                    l8=jnp.zeros((V7X_SUBLANES, TQ), F32),
                                    acc=jnp.zeros((HEAD_DIM, TQ), F32)))
        waiting = []
        for back in range(0, g0 + TQ * blocks_per_group, TK):
            live = [(st, st["q0"] + TQ - TK - back) for st in streams
                    if st["q0"] + TQ - TK - back >= 0]
            issued = [(st, score_half(st, k0)) for st, k0 in live]
            for st, pending in waiting:
                value_half(st, pending)
            waiting = issued
        for st, pending in waiting:
            value_half(st, pending)
        for i in range(0, len(streams), HEADS_PER_STEP):
            outs = [st["acc"] / jnp.sum(st["l8"], axis=0, keepdims=True)
                    for st in streams[i:i + HEADS_PER_STEP]]
            q0 = streams[i]["q0"]
            o_ref[q0:q0 + TQ, :] = jnp.concatenate(outs, axis=0).T.astype(o_ref.dtype)


def _attn_sample_kernel(slopes_ref, q_ref, kn_ref, vn_ref, kst_ref, vst_ref, o_ref,
                        knew_ref, vnew_ref):
    dec_seq = q_ref.shape[1]
    n_past = kst_ref.shape[2]
    width = q_ref.shape[2]
    n_rows = dec_seq * N_HEADS

    q = q_ref[0]
    head_of_lane = lax.broadcasted_iota(jnp.int32, (N_HEADS, width), 1) // HEAD_DIM
    head_of_row = lax.broadcasted_iota(jnp.int32, (N_HEADS, width), 0)
    head_mask = head_of_lane == head_of_row
    wt = jnp.concatenate(
        [jnp.where(head_mask, jnp.broadcast_to(q[t:t + 1, :], (N_HEADS, width)), 0.0)
         for t in range(dec_seq)], axis=0).astype(BF16)

    knew_ref[...] = jnp.zeros(knew_ref.shape, F32)
    vnew_ref[...] = jnp.zeros(vnew_ref.shape, F32)
    knew_ref[0:dec_seq, :] = kn_ref[0]
    vnew_ref[0:dec_seq, :] = vn_ref[0]

    s_past = jnp.dot(wt, kst_ref[0].astype(BF16), preferred_element_type=F32)
    s_new = _dot_nt(wt, knew_ref[...].astype(BF16))

    row = lax.broadcasted_iota(jnp.int32, (n_rows, 1), 0)
    step = row // N_HEADS
    slope_col = jnp.zeros((n_rows, 1), F32)
    for h in range(N_HEADS):
        slope_col = jnp.where(row % N_HEADS == h, slopes_ref[h], slope_col)

    def weigh(s, key_pos):
        delta = n_past + step - key_pos
        c = _branch_count(delta)
        s = jnp.where(c > 0.0, s - slope_col * delta.astype(F32), NEG_INF)
        return s, c

    s_past, c_past = weigh(s_past, lax.broadcasted_iota(jnp.int32, (1, n_past), 1))
    s_new, c_new = weigh(s_new, n_past + lax.broadcasted_iota(jnp.int32, (1, V7X_LANES), 1))
    m = jnp.maximum(jnp.max(s_past, axis=-1, keepdims=True),
                    jnp.max(s_new, axis=-1, keepdims=True))
    p_past = c_past * jnp.exp(s_past - m)
    p_new = c_new * jnp.exp(s_new - m)
    l = jnp.sum(p_past, axis=-1, keepdims=True) + jnp.sum(p_new, axis=-1, keepdims=True)
    o_all = (_dot_nt(p_past.astype(BF16), vst_ref[0].astype(BF16))
             + jnp.dot(p_new.astype(BF16), vnew_ref[...].astype(BF16),
                       preferred_element_type=F32)) / l
    outs = []
    for t in range(dec_seq):
        blk = o_all[t * N_HEADS:(t + 1) * N_HEADS, :]
        outs.append(jnp.sum(jnp.where(head_mask, blk, 0.0), axis=0, keepdims=True))
    o_ref[0] = jnp.concatenate(outs, axis=0).astype(o_ref.dtype)


def _ffn_kernel(x_ref, a_ref, c_ref, wo_ref, g2_ref, wg_ref, wu_ref, wd_ref, gf_ref, y_ref, *,
                final_norm):
    mix = jnp.concatenate([a_ref[...].astype(BF16), c_ref[...].astype(BF16)], axis=-1)
    h = x_ref[...] + jnp.dot(mix, wo_ref[...], preferred_element_type=F32)
    hn = _rms(h, g2_ref[...]).astype(BF16)
    gate = jnp.dot(hn, wg_ref[...], preferred_element_type=F32)
    up = jnp.dot(hn, wu_ref[...], preferred_element_type=F32)
    act = (gate * (1.0 / (1.0 + jnp.exp(-gate)))) * up
    y = h + jnp.dot(act.astype(BF16), wd_ref[...], preferred_element_type=F32)
    y_ref[...] = _rms(y, gf_ref[...]) if final_norm else y


def _nbytes(shape, dtype):
    n = 1
    for s in shape:
        n *= s
    return n * jnp.dtype(dtype).itemsize


def _vmem_limit(pipelined, resident, temporaries):
    need = 2 * sum(pipelined) + sum(resident) + sum(temporaries)
    return min(V7X_VMEM_BYTES, need + need // 4)


def _resident(shape):
    zeros = (0,) * len(shape)
    return pl.BlockSpec(shape, lambda *_: zeros, pipeline_mode=pl.Buffered(1))


def _row_tile(rows, largest=512):
    for tm in (largest, 512, 256, 128):
        if rows % tm == 0:
            return tm
    raise ValueError(f"row count {rows} is not a multiple of 128")


def _inproj_prompt(x, g, w_qc, w_kvt, wc, seq):
    rows, d_model = x.shape
    d_conv = wc.shape[1]
    tm = _row_tile(seq, largest=ROW_GROUPS * 512)
    tiles_per_seq = seq // tm
    n_seq = rows // seq
    row_blk = lambda width: pl.BlockSpec((tm, width), lambda b, j: (b * tiles_per_seq + j, 0))
    col_blk = pl.BlockSpec((1, D_ATTN, tm), lambda b, j: (b, 0, j))
    limit = _vmem_limit(
        [_nbytes((tm, d_model), F32), _nbytes((tm, D_ATTN), BF16), 2 * _nbytes((D_ATTN, tm), F32),
         _nbytes((tm, d_conv), BF16)],
        [_nbytes(w_qc.shape, BF16), _nbytes(w_kvt.shape, BF16)],
        [2 * _nbytes((tm, d_model), F32), 2 * _nbytes((tm, w_qc.shape[1]), F32),
         2 * _nbytes((w_kvt.shape[0], tm), F32), 4 * _nbytes((tm, d_conv), F32)])
    return pl.pallas_call(
        _inproj_prompt_kernel,
        grid=(n_seq, tiles_per_seq),
        in_specs=[row_blk(d_model), _resident((1, d_model)), _resident(w_qc.shape),
                  _resident(w_kvt.shape), _resident(wc.shape)],
        out_specs=[row_blk(D_ATTN), col_blk, col_blk, row_blk(d_conv),
                   pl.BlockSpec((1, V7X_SUBLANES, d_conv), lambda b, j: (b, 0, 0))],
        out_shape=[jax.ShapeDtypeStruct((rows, D_ATTN), BF16),
                   jax.ShapeDtypeStruct((n_seq, D_ATTN, seq), F32),
                   jax.ShapeDtypeStruct((n_seq, D_ATTN, seq), F32),
                   jax.ShapeDtypeStruct((rows, d_conv), BF16),
                   jax.ShapeDtypeStruct((n_seq, V7X_SUBLANES, d_conv), F32)],
        scratch_shapes=[pltpu.VMEM((tm + V7X_SUBLANES, d_conv), F32)],
        compiler_params=pltpu.CompilerParams(dimension_semantics=("arbitrary", "arbitrary"),
                                             vmem_limit_bytes=limit),
        name="inproj_prompt",
    )(x, g, w_qc, w_kvt, wc)


def _inproj_sample(x, g, w_qc, w_kv, wc, prev1, prev2, dec_seq):
    rows, d_model = x.shape
    d_conv = wc.shape[1]
    full = lambda shape: pl.BlockSpec(shape, lambda i: (0,) * len(shape))
    return pl.pallas_call(
        functools.partial(_inproj_sample_kernel, dec_seq=dec_seq),
        grid=(1,),
        in_specs=[full(x.shape), full((1, d_model)), full(w_qc.shape), full(w_kv.shape),
                  full(wc.shape), full(prev1.shape), full(prev2.shape)],
        out_specs=[full((rows, D_ATTN))] * 3 + [full((rows, d_conv))] * 2,
        out_shape=[jax.ShapeDtypeStruct((rows, D_ATTN), F32)] * 3
        + [jax.ShapeDtypeStruct((rows, d_conv), BF16), jax.ShapeDtypeStruct((rows, d_conv), F32)],
        scratch_shapes=[pltpu.VMEM((rows + V7X_SUBLANES, d_conv), F32)],
        compiler_params=pltpu.CompilerParams(dimension_semantics=("arbitrary",)),
        name="inproj_sample",
    )(x, g, w_qc, w_kv, wc, prev1, prev2)


def _attn_prompt(slopes, q, kt, vt, seq):
    rows = q.shape[0]
    n_seq = rows // seq
    q_blk = pl.BlockSpec((seq, V7X_LANES), lambda b, p: (b, p))
    kv_blk = pl.BlockSpec((1, V7X_LANES, seq), lambda b, p: (b, p, 0))
    dist, logc = _attn_tables(seq)
    table = _nbytes((TQ, seq), F32)
    limit = _vmem_limit(
        [2 * _nbytes((seq, V7X_LANES), F32), 2 * _nbytes((seq, V7X_LANES), BF16)],
        [2 * _nbytes((seq, V7X_LANES), BF16), (2 + HEADS_PER_STEP) * table],
        [3 * HEADS_PER_STEP * table])
    return pl.pallas_call(
        _attn_prompt_kernel,
        grid=(n_seq, N_HEADS // HEADS_PER_STEP),
        in_specs=[pl.BlockSpec(memory_space=pltpu.SMEM), _resident((seq, TQ)),
                  _resident((seq, TQ)), q_blk, kv_blk, kv_blk],
        out_specs=q_blk,
        out_shape=jax.ShapeDtypeStruct((rows, D_ATTN), BF16),
        scratch_shapes=[pltpu.VMEM((seq, V7X_LANES), BF16), pltpu.VMEM((V7X_LANES, seq), BF16),
                        pltpu.VMEM((HEADS_PER_STEP, seq, TQ), F32)],
        compiler_params=pltpu.CompilerParams(dimension_semantics=("arbitrary", "arbitrary"),
                                             vmem_limit_bytes=limit),
        name="attn_prompt",
    )(slopes, dist, logc, q, kt, vt)


def _attn_sample(slopes, q, kn, vn, kst, vst):
    n_seq, dec_seq, width = q.shape
    n_past = kst.shape[2]
    new_blk = pl.BlockSpec((1, dec_seq, width), lambda b: (b, 0, 0))
    past_blk = pl.BlockSpec((1, width, n_past), lambda b: (b, 0, 0))
    limit = _vmem_limit(
        [2 * _nbytes((n_past, width), F32)], [],
        [2 * _nbytes((n_past, width), BF16), 8 * _nbytes((dec_seq * N_HEADS, n_past), F32)])
    return pl.pallas_call(
        _attn_sample_kernel,
        grid=(n_seq,),
        in_specs=[pl.BlockSpec(memory_space=pltpu.SMEM), new_blk, new_blk, new_blk,
                  past_blk, past_blk],
        out_specs=new_blk,
        out_shape=jax.ShapeDtypeStruct((n_seq, dec_seq, width), F32),
        scratch_shapes=[pltpu.VMEM((V7X_LANES, width), F32)] * 2,
        compiler_params=pltpu.CompilerParams(dimension_semantics=("arbitrary",),
                                             vmem_limit_bytes=limit),
        name="attn_sample",
    )(slopes, q, kn, vn, kst, vst)


def _ffn(x, attn_o, conv_o, wo, g2, wg, wu, wd, gf, final_norm, tm):
    rows, d_model = x.shape
    d_ff = wg.shape[1]
    row_blk = lambda width: pl.BlockSpec((tm, width), lambda i: (i, 0))
    limit = _vmem_limit(
        [2 * _nbytes((tm, d_model), F32), _nbytes((tm, attn_o.shape[1]), attn_o.dtype),
         _nbytes((tm, conv_o.shape[1]), conv_o.dtype)],
        [_nbytes(w.shape, BF16) for w in (wo, wg, wu, wd)],
        [4 * _nbytes((tm, d_model), F32), 3 * _nbytes((tm, d_ff), F32)])
    return pl.pallas_call(
        functools.partial(_ffn_kernel, final_norm=final_norm),
        grid=(rows // tm,),
        in_specs=[row_blk(d_model), row_blk(attn_o.shape[1]), row_blk(conv_o.shape[1]),
                  _resident(wo.shape), _resident((1, d_model)), _resident(wg.shape),
                  _resident(wu.shape), _resident(wd.shape), _resident((1, d_model))],
        out_specs=row_blk(d_model),
        out_shape=jax.ShapeDtypeStruct((rows, d_model), F32),
        compiler_params=pltpu.CompilerParams(dimension_semantics=("arbitrary",),
                                             vmem_limit_bytes=limit),
        name="ffn",
    )(x, attn_o, conv_o, wo, g2, wg, wu, wd, gf)


def _alibi_slopes():
    return jnp.exp2(-8.0 * jnp.arange(1, N_HEADS + 1, dtype=F32) / N_HEADS)


def kernel(x_prompt, x_sample, state_attn_k, state_attn_v, state_conv, norm_mix_g, w_in, w_conv,
           w_out, norm_ffn_g, w_gate, w_up, w_down, norm_final_g):
    depth = w_in.shape[0]
    batch, seq, d_model = x_prompt.shape
    dec_batch, dec_seq, _ = x_sample.shape
    d_conv = w_conv.shape[2]
    n_past = state_attn_k.shape[2]
    assert n_past >= max(WINDOWS) and seq <= max(WINDOWS) and seq % TQ == 0
    assert CONV_WIDTH - 1 <= dec_seq <= V7X_LANES
    slopes = _alibi_slopes()
    gf = norm_final_g.reshape(1, d_model)

    yp = x_prompt.reshape(batch * seq, d_model)
    ys = x_sample.reshape(dec_batch * dec_seq, d_model)
    outs = [[] for _ in range(6)]
    for layer in range(depth):
        g1 = norm_mix_g[layer].reshape(1, d_model)
        g2 = norm_ffn_g[layer].reshape(1, d_model)
        w_in_bf = w_in[layer].astype(BF16)
        w_qc = jnp.concatenate([w_in_bf[:, :D_ATTN], w_in_bf[:, 3 * D_ATTN:]], axis=1)
        w_kv = w_in_bf[:, D_ATTN:3 * D_ATTN]
        weights = [w.astype(BF16) for w in (w_out[layer], w_gate[layer], w_up[layer], w_down[layer])]
        wc = w_conv[layer]

        q, kt, vt, conv_o, u_last = _inproj_prompt(yp, g1, w_qc, w_kv.T, wc, seq)
        attn_o = _attn_prompt(slopes, q, kt, vt, seq)
        y_layer = _ffn(yp, attn_o, conv_o, weights[0], g2, *weights[1:], gf,
                       layer == depth - 1, _row_tile(seq))
        to_heads = lambda a: a.reshape(batch, N_HEADS, HEAD_DIM, seq).transpose(0, 3, 1, 2)
        outs[0].append(to_heads(kt))
        outs[1].append(to_heads(vt))
        outs[2].append(u_last[:, V7X_SUBLANES - (CONV_WIDTH - 1):, :])
        yp = y_layer

        st = state_conv[layer]
        prev1 = jnp.pad(st[:, 1:2], ((0, 0), (0, dec_seq - 1), (0, 0)))
        prev2 = jnp.pad(st, ((0, 0), (0, dec_seq - 2), (0, 0)))
        qs, ks, vs, conv_s, u_s = _inproj_sample(
            ys, g1, w_qc, w_kv, wc, prev1.reshape(-1, d_conv), prev2.reshape(-1, d_conv), dec_seq)
        as3 = lambda a: a.reshape(dec_batch, dec_seq, D_ATTN)
        feature_major = lambda a: a.transpose(0, 2, 3, 1).reshape(dec_batch, D_ATTN, n_past)
        attn_s = _attn_sample(slopes, as3(qs), as3(ks), as3(vs),
                              feature_major(state_attn_k[layer]),
                              feature_major(state_attn_v[layer]))
        ys = _ffn(ys, attn_s.reshape(-1, D_ATTN), conv_s, weights[0], g2, *weights[1:], gf,
                  layer == depth - 1, ys.shape[0])
        outs[3].append(ks.reshape(dec_batch, dec_seq, N_HEADS, HEAD_DIM))
        outs[4].append(vs.reshape(dec_batch, dec_seq, N_HEADS, HEAD_DIM))
        outs[5].append(u_s.reshape(dec_batch, dec_seq, d_conv)[:, dec_seq - (CONV_WIDTH - 1):])

    y_prompt = yp.reshape(batch, seq, d_model)
    y_sample = ys.reshape(dec_batch, dec_seq, d_model)
    new_k_p, new_v_p, new_c_p, new_k_s, new_v_s, new_c_s = [jnp.stack(o) for o in outs]
    return (y_prompt, y_sample, new_k_p, new_v_p, new_c_p, new_k_s, new_v_s, new_c_s)
```

```python
import functools

import jax
import jax.numpy as jnp
import numpy as np
from jax import lax
from jax.experimental import pallas as pl
from jax.experimental.pallas import tpu as pltpu

F32 = jnp.float32
BF16 = jnp.bfloat16

HEAD_DIM = 64
N_HEADS = 8
D_ATTN = N_HEADS * HEAD_DIM
CONV_WIDTH = 3
WINDOWS = (128, 512, 2048)
DILATIONS = (1, 4, 16)
RMS_EPS = 1e-6
ATTN_SCALE = HEAD_DIM ** -0.5
LOG2_E = 1.4426950408889634
NEG_INF = -1e30

V7X_LANES = 128
V7X_SUBLANES = 8
V7X_VMEM_BYTES = 64 * 1024 * 1024

HEADS_PER_STEP = V7X_LANES // HEAD_DIM
TQ = 256
TK = 256
ATTN_STREAMS = 4
ATTN_LOOKAHEAD = 1
ROW_GROUPS = 2


def _rms(x, g):
    y = x * lax.rsqrt(jnp.mean(x * x, axis=-1, keepdims=True) + RMS_EPS)
    return y * g


def _branch_count(delta):
    nonneg = delta >= 0
    c = jnp.zeros(delta.shape, F32)
    for w, d in zip(WINDOWS, DILATIONS):
        hit = nonneg & ((delta & (d - 1)) == 0) & (delta <= w)
        c = c + hit.astype(F32)
    return c


def _dot_nt(a, b):
    return lax.dot_general(a, b, (((1,), (1,)), ((), ())), preferred_element_type=F32)


def _project(x, g, wqc_ref, q_scale):
    d_conv = (wqc_ref.shape[1] - D_ATTN) // 3
    xn = _rms(x, g).astype(BF16)
    zc = jnp.dot(xn, wqc_ref[:, D_ATTN:], preferred_element_type=F32)
    hc = zc[:, 0:d_conv]
    gb = zc[:, d_conv:2 * d_conv]
    gc = zc[:, 2 * d_conv:3 * d_conv]
    q = jnp.dot(xn, wqc_ref[:, 0:D_ATTN], preferred_element_type=F32) * q_scale
    return xn, q, hc, gb, gc


def _conv3(wc, u2, u1, u0):
    acc = wc[0:1, :] * u2
    acc = acc + wc[1:2, :] * u1
    return acc + wc[2:3, :] * u0


def _inproj_prompt_kernel(x_ref, g_ref, wqc_ref, wkvt_ref, wc_ref, q_ref, kt_ref, vt_ref, co_ref,
                          ulast_ref, uext_ref):
    tm = x_ref.shape[0]
    j = pl.program_id(1)
    tiles_per_seq = pl.num_programs(1)

    @pl.when(j == 0)
    def _():
        uext_ref[0:V7X_SUBLANES, :] = jnp.zeros((V7X_SUBLANES, uext_ref.shape[1]), F32)

    rg = tm // ROW_GROUPS
    for r0 in range(0, tm, rg):
        xn, q, hc, gb, gc = _project(x_ref[r0:r0 + rg, :], g_ref[...], wqc_ref,
                                     ATTN_SCALE * LOG2_E)
        q_ref[r0:r0 + rg, :] = q.astype(q_ref.dtype)
        kvt = _dot_nt(wkvt_ref[...], xn)
        kt_ref[0, :, r0:r0 + rg] = kvt[0:D_ATTN, :]
        vt_ref[0, :, r0:r0 + rg] = kvt[D_ATTN:2 * D_ATTN, :]
        u = gc * hc
        uext_ref[V7X_SUBLANES:V7X_SUBLANES + rg, :] = u
        u1 = uext_ref[V7X_SUBLANES - 1:V7X_SUBLANES - 1 + rg, :]
        u2 = uext_ref[V7X_SUBLANES - 2:V7X_SUBLANES - 2 + rg, :]
        co_ref[r0:r0 + rg, :] = (gb * _conv3(wc_ref[...], u2, u1, u)).astype(co_ref.dtype)
        tail = u[rg - V7X_SUBLANES:rg, :]
        uext_ref[0:V7X_SUBLANES, :] = tail

    @pl.when(j == tiles_per_seq - 1)
    def _():
        ulast_ref[0] = tail


def _inproj_sample_kernel(x_ref, g_ref, wqc_ref, wkv_ref, wc_ref, p1_ref, p2_ref, q_ref, k_ref,
                          v_ref, co_ref, u_ref, uext_ref, *, dec_seq):
    rows = x_ref.shape[0]
    xn, q, hc, gb, gc = _project(x_ref[...], g_ref[...], wqc_ref, ATTN_SCALE)
    kv = jnp.dot(xn, wkv_ref[...], preferred_element_type=F32)
    q_ref[...] = q
    k_ref[...] = kv[:, 0:D_ATTN]
    v_ref[...] = kv[:, D_ATTN:2 * D_ATTN]
    u = gc * hc
    u_ref[...] = u
    uext_ref[0:V7X_SUBLANES, :] = jnp.zeros((V7X_SUBLANES, u.shape[1]), F32)
    uext_ref[V7X_SUBLANES:V7X_SUBLANES + rows, :] = u
    t = lax.broadcasted_iota(jnp.int32, u.shape, 0) % dec_seq
    u1 = jnp.where(t >= 1, uext_ref[V7X_SUBLANES - 1:V7X_SUBLANES - 1 + rows, :], p1_ref[...])
    u2 = jnp.where(t >= 2, uext_ref[V7X_SUBLANES - 2:V7X_SUBLANES - 2 + rows, :], p2_ref[...])
    co_ref[...] = (gb * _conv3(wc_ref[...], u2, u1, u)).astype(co_ref.dtype)


def _attn_tables(seq):
    x = np.arange(seq, dtype=np.int32)[:, None]
    i = np.arange(TQ, dtype=np.int32)[None, :]
    delta = i + (seq - TQ) - x
    c = np.zeros(delta.shape, np.float32)
    for w, d in zip(WINDOWS, DILATIONS):
        c += (delta >= 0) & (delta % d == 0) & (delta <= w)
    logc = np.where(c > 0.0, np.log2(np.maximum(c, 1.0)), NEG_INF).astype(np.float32)
    return delta.astype(np.float32), logc


def _attn_prompt_kernel(slopes_ref, dist_ref, logc_ref, q_ref, kt_ref, vt_ref, o_ref,
                        kb_ref, vtb_ref, bm_ref):
    seq = q_ref.shape[0]
    pair = pl.program_id(1)
    slope = [slopes_ref[HEADS_PER_STEP * pair + h] for h in range(HEADS_PER_STEP)]
    lane = lax.broadcasted_iota(jnp.int32, (1, V7X_LANES), 1)
    first_head = lane < HEAD_DIM

    kb_ref[...] = kt_ref[0].T.astype(BF16)
    vtb_ref[...] = vt_ref[0].astype(BF16)

    for h in range(HEADS_PER_STEP):
        bm_ref[h] = logc_ref[...] - (slope[h] * LOG2_E) * dist_ref[...]

    def score_half(st, chunk):
        k0, n = chunk
        h, row0 = st["h"], seq - (st["q0"] + TQ)
        s = _dot_nt(kb_ref[k0:k0 + n, :], st["q"]) + bm_ref[h, row0 + k0:row0 + k0 + n, :]
        c8 = jnp.max(s.reshape(n // V7X_SUBLANES, V7X_SUBLANES, TQ), axis=0)
        m_new = jnp.maximum(st["m"], jnp.max(c8, axis=0, keepdims=True))
        pending = dict(k0=k0, n=n, s=s, m=m_new, alpha=jnp.exp2(st["m"] - m_new))
        st["m"] = m_new
        return pending

    def value_half(st, pending):
        h, k0, n = st["h"], pending["k0"], pending["n"]
        p = jnp.exp2(pending["s"] - pending["m"])
        st["l8"] = (pending["alpha"] * st["l8"]
                    + jnp.sum(p.reshape(n // V7X_SUBLANES, V7X_SUBLANES, TQ), axis=0))
        vt_h = vtb_ref[h * HEAD_DIM:(h + 1) * HEAD_DIM, k0:k0 + n]
        st["acc"] = pending["alpha"] * st["acc"] + jnp.dot(vt_h, p.astype(BF16),
                                                           preferred_element_type=F32)

    blocks_per_group = ATTN_STREAMS // HEADS_PER_STEP
    for g0 in range(0, seq, TQ * blocks_per_group):
        streams = []
        for q0 in range(g0, g0 + TQ * blocks_per_group, TQ):
            q = q_ref[q0:q0 + TQ, :]
            zero = jnp.zeros_like(q)
            for h, qh in enumerate([jnp.where(first_head, q, zero), jnp.where(first_head, zero, q)]):
                chunks = [(q0, TQ)] + [(max(k1 - TK, 0), min(TK, k1)) for k1 in range(q0, 0, -TK)]
                streams.append(dict(q0=q0, h=h, q=qh, chunks=chunks,
                                    m=jnp.full((1, TQ), NEG_INF, F32),
                                    l8=jnp.zeros((V7X_SUBLANES, TQ), F32),
                                    acc=jnp.zeros((HEAD_DIM, TQ), F32)))
        waiting = []
        for t in range(max(len(st["chunks"]) for st in streams)):
            live = [(st, st["chunks"][t]) for st in streams if t < len(st["chunks"])]
            due = waiting.pop(0) if len(waiting) >= ATTN_LOOKAHEAD else []
            issued = []
            for i, (st, k0) in enumerate(live):
                issued.append((st, score_half(st, k0)))
                if i < len(due):
                    value_half(*due[i])
            for st, pending in due[len(live):]:
                value_half(st, pending)
            waiting.append(issued)
        for step in waiting:
            for st, pending in step:
                value_half(st, pending)
        for i in range(0, len(streams), HEADS_PER_STEP):
            outs = [st["acc"] / jnp.sum(st["l8"], axis=0, keepdims=True)
                    for st in streams[i:i + HEADS_PER_STEP]]
            q0 = streams[i]["q0"]
            o_ref[q0:q0 + TQ, :] = jnp.concatenate(outs, axis=0).T.astype(o_ref.dtype)


def _attn_sample_kernel(slopes_ref, q_ref, kn_ref, vn_ref, kst_ref, vst_ref, o_ref,
                        knew_ref, vnew_ref):
    dec_seq = q_ref.shape[1]
    n_past = kst_ref.shape[2]
    width = q_ref.shape[2]
    n_rows = dec_seq * N_HEADS

    q = q_ref[0]
    head_of_lane = lax.broadcasted_iota(jnp.int32, (N_HEADS, width), 1) // HEAD_DIM
    head_of_row = lax.broadcasted_iota(jnp.int32, (N_HEADS, width), 0)
    head_mask = head_of_lane == head_of_row
    wt = jnp.concatenate(
        [jnp.where(head_mask, jnp.broadcast_to(q[t:t + 1, :], (N_HEADS, width)), 0.0)
         for t in range(dec_seq)], axis=0).astype(BF16)

    knew_ref[...] = jnp.zeros(knew_ref.shape, F32)
    vnew_ref[...] = jnp.zeros(vnew_ref.shape, F32)
    knew_ref[0:dec_seq, :] = kn_ref[0]
    vnew_ref[0:dec_seq, :] = vn_ref[0]

    s_past = jnp.dot(wt, kst_ref[0].astype(BF16), preferred_element_type=F32)
    s_new = _dot_nt(wt, knew_ref[...].astype(BF16))

    row = lax.broadcasted_iota(jnp.int32, (n_rows, 1), 0)
    step = row // N_HEADS
    slope_col = jnp.zeros((n_rows, 1), F32)
    for h in range(N_HEADS):
        slope_col = jnp.where(row % N_HEADS == h, slopes_ref[h], slope_col)

    def weigh(s, key_pos):
        delta = n_past + step - key_pos
        c = _branch_count(delta)
        s = jnp.where(c > 0.0, s - slope_col * delta.astype(F32), NEG_INF)
        return s, c

    s_past, c_past = weigh(s_past, lax.broadcasted_iota(jnp.int32, (1, n_past), 1))
    s_new, c_new = weigh(s_new, n_past + lax.broadcasted_iota(jnp.int32, (1, V7X_LANES), 1))
    m = jnp.maximum(jnp.max(s_past, axis=-1, keepdims=True),
                    jnp.max(s_new, axis=-1, keepdims=True))
    p_past = c_past * jnp.exp(s_past - m)
    p_new = c_new * jnp.exp(s_new - m)
    l = jnp.sum(p_past, axis=-1, keepdims=True) + jnp.sum(p_new, axis=-1, keepdims=True)
    o_all = (_dot_nt(p_past.astype(BF16), vst_ref[0].astype(BF16))
             + jnp.dot(p_new.astype(BF16), vnew_ref[...].astype(BF16),
                       preferred_element_type=F32)) / l
    outs = []
    for t in range(dec_seq):
        blk = o_all[t * N_HEADS:(t + 1) * N_HEADS, :]
        outs.append(jnp.sum(jnp.where(head_mask, blk, 0.0), axis=0, keepdims=True))
    o_ref[0] = jnp.concatenate(outs, axis=0).astype(o_ref.dtype)


def _ffn_kernel(x_ref, a_ref, c_ref, wo_ref, g2_ref, wg_ref, wu_ref, wd_ref, gf_ref, y_ref, *,
                final_norm):
    mix = jnp.concatenate([a_ref[...].astype(BF16), c_ref[...].astype(BF16)], axis=-1)
    h = x_ref[...] + jnp.dot(mix, wo_ref[...], preferred_element_type=F32)
    hn = _rms(h, g2_ref[...]).astype(BF16)
    gate = jnp.dot(hn, wg_ref[...], preferred_element_type=F32)
    up = jnp.dot(hn, wu_ref[...], preferred_element_type=F32)
    act = (gate * (1.0 / (1.0 + jnp.exp(-gate)))) * up
    y = h + jnp.dot(act.astype(BF16), wd_ref[...], preferred_element_type=F32)
    y_ref[...] = _rms(y, gf_ref[...]) if final_norm else y


def _nbytes(shape, dtype):
    n = 1
    for s in shape:
        n *= s
    return n * jnp.dtype(dtype).itemsize


def _vmem_limit(pipelined, resident, temporaries):
    need = 2 * sum(pipelined) + sum(resident) + sum(temporaries)
    return min(V7X_VMEM_BYTES, need + need // 4)


def _resident(shape):
    zeros = (0,) * len(shape)
    return pl.BlockSpec(shape, lambda *_: zeros, pipeline_mode=pl.Buffered(1))


def _row_tile(rows, largest=512):
    for tm in (largest, 512, 256, 128):
        if rows % tm == 0:
            return tm
    raise ValueError(f"row count {rows} is not a multiple of 128")


def _inproj_prompt(x, g, w_qc, w_kvt, wc, seq):
    rows, d_model = x.shape
    d_conv = wc.shape[1]
    tm = _row_tile(seq, largest=ROW_GROUPS * 512)
    tiles_per_seq = seq // tm
    n_seq = rows // seq
    row_blk = lambda width: pl.BlockSpec((tm, width), lambda b, j: (b * tiles_per_seq + j, 0))
    col_blk = pl.BlockSpec((1, D_ATTN, tm), lambda b, j: (b, 0, j))
    limit = _vmem_limit(
        [_nbytes((tm, d_model), F32), _nbytes((tm, D_ATTN), BF16), 2 * _nbytes((D_ATTN, tm), F32),
         _nbytes((tm, d_conv), BF16)],
        [_nbytes(w_qc.shape, BF16), _nbytes(w_kvt.shape, BF16)],
        [2 * _nbytes((tm, d_model), F32), 2 * _nbytes((tm, w_qc.shape[1]), F32),
         2 * _nbytes((w_kvt.shape[0], tm), F32), 4 * _nbytes((tm, d_conv), F32)])
    return pl.pallas_call(
        _inproj_prompt_kernel,
        grid=(n_seq, tiles_per_seq),
        in_specs=[row_blk(d_model), _resident((1, d_model)), _resident(w_qc.shape),
                  _resident(w_kvt.shape), _resident(wc.shape)],
        out_specs=[row_blk(D_ATTN), col_blk, col_blk, row_blk(d_conv),
                   pl.BlockSpec((1, V7X_SUBLANES, d_conv), lambda b, j: (b, 0, 0))],
        out_shape=[jax.ShapeDtypeStruct((rows, D_ATTN), BF16),
                   jax.ShapeDtypeStruct((n_seq, D_ATTN, seq), F32),
                   jax.ShapeDtypeStruct((n_seq, D_ATTN, seq), F32),
                   jax.ShapeDtypeStruct((rows, d_conv), BF16),
                   jax.ShapeDtypeStruct((n_seq, V7X_SUBLANES, d_conv), F32)],
        scratch_shapes=[pltpu.VMEM((tm + V7X_SUBLANES, d_conv), F32)],
        compiler_params=pltpu.CompilerParams(dimension_semantics=("arbitrary", "arbitrary"),
                                             vmem_limit_bytes=limit),
        name="inproj_prompt",
    )(x, g, w_qc, w_kvt, wc)


def _inproj_sample(x, g, w_qc, w_kv, wc, prev1, prev2, dec_seq):
    rows, d_model = x.shape
    d_conv = wc.shape[1]
    full = lambda shape: pl.BlockSpec(shape, lambda i: (0,) * len(shape))
    return pl.pallas_call(
        functools.partial(_inproj_sample_kernel, dec_seq=dec_seq),
        grid=(1,),
        in_specs=[full(x.shape), full((1, d_model)), full(w_qc.shape), full(w_kv.shape),
                  full(wc.shape), full(prev1.shape), full(prev2.shape)],
        out_specs=[full((rows, D_ATTN))] * 3 + [full((rows, d_conv))] * 2,
        out_shape=[jax.ShapeDtypeStruct((rows, D_ATTN), F32)] * 3
        + [jax.ShapeDtypeStruct((rows, d_conv), BF16), jax.ShapeDtypeStruct((rows, d_conv), F32)],
        scratch_shapes=[pltpu.VMEM((rows + V7X_SUBLANES, d_conv), F32)],
        compiler_params=pltpu.CompilerParams(dimension_semantics=("arbitrary",)),
        name="inproj_sample",
    )(x, g, w_qc, w_kv, wc, prev1, prev2)


def _attn_prompt(slopes, q, kt, vt, seq):
    rows = q.shape[0]
    n_seq = rows // seq
    q_blk = pl.BlockSpec((seq, V7X_LANES), lambda b, p: (b, p))
    kv_blk = pl.BlockSpec((1, V7X_LANES, seq), lambda b, p: (b, p, 0))
    dist, logc = _attn_tables(seq)
    table = _nbytes((TQ, seq), F32)
    limit = _vmem_limit(
        [2 * _nbytes((seq, V7X_LANES), F32), 2 * _nbytes((seq, V7X_LANES), BF16)],
        [2 * _nbytes((seq, V7X_LANES), BF16), (2 + HEADS_PER_STEP) * table],
        [3 * HEADS_PER_STEP * table])
    return pl.pallas_call(
        _attn_prompt_kernel,
        grid=(n_seq, N_HEADS // HEADS_PER_STEP),
        in_specs=[pl.BlockSpec(memory_space=pltpu.SMEM), _resident((seq, TQ)),
                  _resident((seq, TQ)), q_blk, kv_blk, kv_blk],
        out_specs=q_blk,
        out_shape=jax.ShapeDtypeStruct((rows, D_ATTN), BF16),
        scratch_shapes=[pltpu.VMEM((seq, V7X_LANES), BF16), pltpu.VMEM((V7X_LANES, seq), BF16),
                        pltpu.VMEM((HEADS_PER_STEP, seq, TQ), F32)],
        compiler_params=pltpu.CompilerParams(dimension_semantics=("arbitrary", "arbitrary"),
                                             vmem_limit_bytes=limit),
        name="attn_prompt",
    )(slopes, dist, logc, q, kt, vt)


def _attn_sample(slopes, q, kn, vn, kst, vst):
    n_seq, dec_seq, width = q.shape
    n_past = kst.shape[2]
    new_blk = pl.BlockSpec((1, dec_seq, width), lambda b: (b, 0, 0))
    past_blk = pl.BlockSpec((1, width, n_past), lambda b: (b, 0, 0))
    limit = _vmem_limit(
        [2 * _nbytes((n_past, width), F32)], [],
        [2 * _nbytes((n_past, width), BF16), 8 * _nbytes((dec_seq * N_HEADS, n_past), F32)])
    return pl.pallas_call(
        _attn_sample_kernel,
        grid=(n_seq,),
        in_specs=[pl.BlockSpec(memory_space=pltpu.SMEM), new_blk, new_blk, new_blk,
                  past_blk, past_blk],
        out_specs=new_blk,
        out_shape=jax.ShapeDtypeStruct((n_seq, dec_seq, width), F32),
        scratch_shapes=[pltpu.VMEM((V7X_LANES, width), F32)] * 2,
        compiler_params=pltpu.CompilerParams(dimension_semantics=("arbitrary",),
                                             vmem_limit_bytes=limit),
        name="attn_sample",
    )(slopes, q, kn, vn, kst, vst)


def _ffn(x, attn_o, conv_o, wo, g2, wg, wu, wd, gf, final_norm, tm):
    rows, d_model = x.shape
    d_ff = wg.shape[1]
    row_blk = lambda width: pl.BlockSpec((tm, width), lambda i: (i, 0))
    limit = _vmem_limit(
        [2 * _nbytes((tm, d_model), F32), _nbytes((tm, attn_o.shape[1]), attn_o.dtype),
         _nbytes((tm, conv_o.shape[1]), conv_o.dtype)],
        [_nbytes(w.shape, BF16) for w in (wo, wg, wu, wd)],
        [4 * _nbytes((tm, d_model), F32), 3 * _nbytes((tm, d_ff), F32)])
    return pl.pallas_call(
        functools.partial(_ffn_kernel, final_norm=final_norm),
        grid=(rows // tm,),
        in_specs=[row_blk(d_model), row_blk(attn_o.shape[1]), row_blk(conv_o.shape[1]),
                  _resident(wo.shape), _resident((1, d_model)), _resident(wg.shape),
                  _resident(wu.shape), _resident(wd.shape), _resident((1, d_model))],
        out_specs=row_blk(d_model),
        out_shape=jax.ShapeDtypeStruct((rows, d_model), F32),
        compiler_params=pltpu.CompilerParams(dimension_semantics=("arbitrary",),
                                             vmem_limit_bytes=limit),
        name="ffn",
    )(x, attn_o, conv_o, wo, g2, wg, wu, wd, gf)


def _alibi_slopes():
    return jnp.exp2(-8.0 * jnp.arange(1, N_HEADS + 1, dtype=F32) / N_HEADS)


def kernel(x_prompt, x_sample, state_attn_k, state_attn_v, state_conv, norm_mix_g, w_in, w_conv,
           w_out, norm_ffn_g, w_gate, w_up, w_down, norm_final_g):
    depth = w_in.shape[0]
    batch, seq, d_model = x_prompt.shape
    dec_batch, dec_seq, _ = x_sample.shape
    d_conv = w_conv.shape[2]
    n_past = state_attn_k.shape[2]
    assert n_past >= max(WINDOWS) and seq <= max(WINDOWS) and seq % TQ == 0
    assert CONV_WIDTH - 1 <= dec_seq <= V7X_LANES
    slopes = _alibi_slopes()
    gf = norm_final_g.reshape(1, d_model)

    yp = x_prompt.reshape(batch * seq, d_model)
    ys = x_sample.reshape(dec_batch * dec_seq, d_model)
    outs = [[] for _ in range(6)]
    for layer in range(depth):
        g1 = norm_mix_g[layer].reshape(1, d_model)
        g2 = norm_ffn_g[layer].reshape(1, d_model)
        w_in_bf = w_in[layer].astype(BF16)
        w_qc = jnp.concatenate([w_in_bf[:, :D_ATTN], w_in_bf[:, 3 * D_ATTN:]], axis=1)
        w_kv = w_in_bf[:, D_ATTN:3 * D_ATTN]
        weights = [w.astype(BF16) for w in (w_out[layer], w_gate[layer], w_up[layer], w_down[layer])]
        wc = w_conv[layer]

        q, kt, vt, conv_o, u_last = _inproj_prompt(yp, g1, w_qc, w_kv.T, wc, seq)
        attn_o = _attn_prompt(slopes, q, kt, vt, seq)
        y_layer = _ffn(yp, attn_o, conv_o, weights[0], g2, *weights[1:], gf,
                       layer == depth - 1, _row_tile(seq))
        to_heads = lambda a: a.reshape(batch, N_HEADS, HEAD_DIM, seq).transpose(0, 3, 1, 2)
        outs[0].append(to_heads(kt))
        outs[1].append(to_heads(vt))
        outs[2].append(u_last[:, V7X_SUBLANES - (CONV_WIDTH - 1):, :])
        yp = y_layer

        st = state_conv[layer]
        prev1 = jnp.pad(st[:, 1:2], ((0, 0), (0, dec_seq - 1), (0, 0)))
        prev2 = jnp.pad(st, ((0, 0), (0, dec_seq - 2), (0, 0)))
        qs, ks, vs, conv_s, u_s = _inproj_sample(
            ys, g1, w_qc, w_kv, wc, prev1.reshape(-1, d_conv), prev2.reshape(-1, d_conv), dec_seq)
        as3 = lambda a: a.reshape(dec_batch, dec_seq, D_ATTN)
        feature_major = lambda a: a.transpose(0, 2, 3, 1).reshape(dec_batch, D_ATTN, n_past)
        attn_s = _attn_sample(slopes, as3(qs), as3(ks), as3(vs),
                              feature_major(state_attn_k[layer]),
                              feature_major(state_attn_v[layer]))
        ys = _ffn(ys, attn_s.reshape(-1, D_ATTN), conv_s, weights[0], g2, *weights[1:], gf,
                  layer == depth - 1, ys.shape[0])
        outs[3].append(ks.reshape(dec_batch, dec_seq, N_HEADS, HEAD_DIM))
        outs[4].append(vs.reshape(dec_batch, dec_seq, N_HEADS, HEAD_DIM))
        outs[5].append(u_s.reshape(dec_batch, dec_seq, d_conv)[:, dec_seq - (CONV_WIDTH - 1):])

    y_prompt = yp.reshape(batch, seq, d_model)
    y_sample = ys.reshape(dec_batch, dec_seq, d_model)
    new_k_p, new_v_p, new_c_p, new_k_s, new_v_s, new_c_s = [jnp.stack(o) for o in outs]
    return (y_prompt, y_sample, new_k_p, new_v_p, new_c_p, new_k_s, new_v_s, new_c_s)
```

```python
import functools

import jax
import jax.numpy as jnp
import numpy as np
from jax import lax
from jax.experimental import pallas as pl
from jax.experimental.pallas import tpu as pltpu

F32 = jnp.float32
BF16 = jnp.bfloat16

HEAD_DIM = 64
N_HEADS = 8
D_ATTN = N_HEADS * HEAD_DIM
CONV_WIDTH = 3
WINDOWS = (128, 512, 2048)
DILATIONS = (1, 4, 16)
RMS_EPS = 1e-6
ATTN_SCALE = HEAD_DIM ** -0.5
LOG2_E = 1.4426950408889634
NEG_INF = -1e30

V7X_LANES = 128
V7X_SUBLANES = 8
V7X_VMEM_BYTES = 64 * 1024 * 1024

HEADS_PER_STEP = V7X_LANES // HEAD_DIM
TQ = 256
TK = 256
ATTN_STREAMS = 4
ATTN_LOOKAHEAD = 1
ROW_GROUPS = 2


def _rms(x, g):
    y = x * lax.rsqrt(jnp.mean(x * x, axis=-1, keepdims=True) + RMS_EPS)
    return y * g


def _branch_count(delta):
    nonneg = delta >= 0
    c = jnp.zeros(delta.shape, F32)
    for w, d in zip(WINDOWS, DILATIONS):
        hit = nonneg & ((delta & (d - 1)) == 0) & (delta <= w)
        c = c + hit.astype(F32)
    return c


def _dot_nt(a, b):
    return lax.dot_general(a, b, (((1,), (1,)), ((), ())), preferred_element_type=F32)


def _project(x, g, wcc_ref):
    d_conv = wcc_ref.shape[1] // 3
    xn = _rms(x, g).astype(BF16)
    zc = jnp.dot(xn, wcc_ref[...], preferred_element_type=F32)
    return xn, zc[:, 0:d_conv], zc[:, d_conv:2 * d_conv], zc[:, 2 * d_conv:3 * d_conv]


def _conv3(wc, u2, u1, u0):
    acc = wc[0:1, :] * u2
    acc = acc + wc[1:2, :] * u1
    return acc + wc[2:3, :] * u0


def _inproj_prompt_kernel(x_ref, g_ref, wcc_ref, wqkvt_ref, wc_ref, qt_ref, kt_ref, vt_ref, co_ref,
                          ulast_ref, uext_ref):
    tm = x_ref.shape[0]
    j = pl.program_id(1)
    tiles_per_seq = pl.num_programs(1)

    @pl.when(j == 0)
    def _():
        uext_ref[0:V7X_SUBLANES, :] = jnp.zeros((V7X_SUBLANES, uext_ref.shape[1]), F32)

    rg = tm // ROW_GROUPS
    for r0 in range(0, tm, rg):
        xn, hc, gb, gc = _project(x_ref[r0:r0 + rg, :], g_ref[...], wcc_ref)
        qkvt = _dot_nt(wqkvt_ref[...], xn)
        qt_ref[0, :, r0:r0 + rg] = (qkvt[0:D_ATTN, :] * (ATTN_SCALE * LOG2_E)).astype(qt_ref.dtype)
        kt_ref[0, :, r0:r0 + rg] = qkvt[D_ATTN:2 * D_ATTN, :]
        vt_ref[0, :, r0:r0 + rg] = qkvt[2 * D_ATTN:3 * D_ATTN, :]
        u = gc * hc
        uext_ref[V7X_SUBLANES:V7X_SUBLANES + rg, :] = u
        u1 = uext_ref[V7X_SUBLANES - 1:V7X_SUBLANES - 1 + rg, :]
        u2 = uext_ref[V7X_SUBLANES - 2:V7X_SUBLANES - 2 + rg, :]
        co_ref[r0:r0 + rg, :] = (gb * _conv3(wc_ref[...], u2, u1, u)).astype(co_ref.dtype)
        tail = u[rg - V7X_SUBLANES:rg, :]
        uext_ref[0:V7X_SUBLANES, :] = tail

    @pl.when(j == tiles_per_seq - 1)
    def _():
        ulast_ref[0] = tail


def _inproj_sample_kernel(x_ref, g_ref, wcc_ref, wqkv_ref, wc_ref, p1_ref, p2_ref, q_ref, k_ref,
                          v_ref, co_ref, u_ref, uext_ref, *, dec_seq):
    rows = x_ref.shape[0]
    xn, hc, gb, gc = _project(x_ref[...], g_ref[...], wcc_ref)
    qkv = jnp.dot(xn, wqkv_ref[...], preferred_element_type=F32)
    q_ref[...] = qkv[:, 0:D_ATTN] * ATTN_SCALE
    k_ref[...] = qkv[:, D_ATTN:2 * D_ATTN]
    v_ref[...] = qkv[:, 2 * D_ATTN:3 * D_ATTN]
    u = gc * hc
    u_ref[...] = u
    uext_ref[0:V7X_SUBLANES, :] = jnp.zeros((V7X_SUBLANES, u.shape[1]), F32)
    uext_ref[V7X_SUBLANES:V7X_SUBLANES + rows, :] = u
    t = lax.broadcasted_iota(jnp.int32, u.shape, 0) % dec_seq
    u1 = jnp.where(t >= 1, uext_ref[V7X_SUBLANES - 1:V7X_SUBLANES - 1 + rows, :], p1_ref[...])
    u2 = jnp.where(t >= 2, uext_ref[V7X_SUBLANES - 2:V7X_SUBLANES - 2 + rows, :], p2_ref[...])
    co_ref[...] = (gb * _conv3(wc_ref[...], u2, u1, u)).astype(co_ref.dtype)


def _attn_tables(seq):
    x = np.arange(seq, dtype=np.int32)[:, None]
    i = np.arange(TQ, dtype=np.int32)[None, :]
    delta = i + (seq - TQ) - x
    c = np.zeros(delta.shape, np.float32)
    for w, d in zip(WINDOWS, DILATIONS):
        c += (delta >= 0) & (delta % d == 0) & (delta <= w)
    logc = np.where(c > 0.0, np.log2(np.maximum(c, 1.0)), NEG_INF).astype(np.float32)
    return delta.astype(np.float32), logc


def _attn_prompt_kernel(slopes_ref, dist_ref, logc_ref, qt_ref, kt_ref, vt_ref, o_ref,
                        kb_ref, vtb_ref, bm_ref):
    seq = qt_ref.shape[2]
    pair = pl.program_id(1)
    slope = [slopes_ref[HEADS_PER_STEP * pair + h] for h in range(HEADS_PER_STEP)]
    feature = lax.broadcasted_iota(jnp.int32, (V7X_LANES, 1), 0)
    first_head = feature < HEAD_DIM

    kb_ref[...] = kt_ref[0].T.astype(BF16)
    vtb_ref[...] = vt_ref[0].astype(BF16)

    for h in range(HEADS_PER_STEP):
        bm_ref[h] = logc_ref[...] - (slope[h] * LOG2_E) * dist_ref[...]

    def score_half(st, chunk):
        k0, n = chunk
        h, row0 = st["h"], seq - (st["q0"] + TQ)
        s = (jnp.dot(kb_ref[k0:k0 + n, :], st["q"], preferred_element_type=F32)
             + bm_ref[h, row0 + k0:row0 + k0 + n, :])
        c8 = jnp.max(s.reshape(n // V7X_SUBLANES, V7X_SUBLANES, TQ), axis=0)
        m_new = jnp.maximum(st["m"], jnp.max(c8, axis=0, keepdims=True))
        pending = dict(k0=k0, n=n, s=s, m=m_new, alpha=jnp.exp2(st["m"] - m_new))
        st["m"] = m_new
        return pending

    def value_half(st, pending):
        h, k0, n = st["h"], pending["k0"], pending["n"]
        p = jnp.exp2(pending["s"] - pending["m"])
        st["l8"] = (pending["alpha"] * st["l8"]
                    + jnp.sum(p.reshape(n // V7X_SUBLANES, V7X_SUBLANES, TQ), axis=0))
        vt_h = vtb_ref[h * HEAD_DIM:(h + 1) * HEAD_DIM, k0:k0 + n]
        st["acc"] = pending["alpha"] * st["acc"] + jnp.dot(vt_h, p.astype(BF16),
                                                           preferred_element_type=F32)

    blocks_per_group = ATTN_STREAMS // HEADS_PER_STEP
    for g0 in range(0, seq, TQ * blocks_per_group):
        streams = []
        for q0 in range(g0, g0 + TQ * blocks_per_group, TQ):
            qt = qt_ref[0, :, q0:q0 + TQ]
            zero = jnp.zeros_like(qt)
            for h, qh in enumerate([jnp.where(first_head, qt, zero), jnp.where(first_head, zero, qt)]):
                chunks = [(q0, TQ)] + [(max(k1 - TK, 0), min(TK, k1)) for k1 in range(q0, 0, -TK)]
                streams.append(dict(q0=q0, h=h, q=qh, chunks=chunks,
                                    m=jnp.full((1, TQ), NEG_INF, F32),
                                    l8=jnp.zeros((V7X_SUBLANES, TQ), F32),
                                    acc=jnp.zeros((HEAD_DIM, TQ), F32)))
        waiting = []
        for t in range(max(len(st["chunks"]) for st in streams)):
            live = [(st, st["chunks"][t]) for st in streams if t < len(st["chunks"])]
            due = waiting.pop(0) if len(waiting) >= ATTN_LOOKAHEAD else []
            issued = []
            for i, (st, k0) in enumerate(live):
                issued.append((st, score_half(st, k0)))
                if i < len(due):
                    value_half(*due[i])
            for st, pending in due[len(live):]:
                value_half(st, pending)
            waiting.append(issued)
        for step in waiting:
            for st, pending in step:
                value_half(st, pending)
        for i in range(0, len(streams), HEADS_PER_STEP):
            outs = [st["acc"] / jnp.sum(st["l8"], axis=0, keepdims=True)
                    for st in streams[i:i + HEADS_PER_STEP]]
            q0 = streams[i]["q0"]
            o_ref[q0:q0 + TQ, :] = jnp.concatenate(outs, axis=0).T.astype(o_ref.dtype)


def _attn_sample_kernel(slopes_ref, q_ref, kn_ref, vn_ref, kst_ref, vst_ref, o_ref,
                        knew_ref, vnew_ref):
    dec_seq = q_ref.shape[1]
    n_past = kst_ref.shape[2]
    width = q_ref.shape[2]
    n_rows = dec_seq * N_HEADS

    q = q_ref[0]
    head_of_lane = lax.broadcasted_iota(jnp.int32, (N_HEADS, width), 1) // HEAD_DIM
    head_of_row = lax.broadcasted_iota(jnp.int32, (N_HEADS, width), 0)
    head_mask = head_of_lane == head_of_row
    wt = jnp.concatenate(
        [jnp.where(head_mask, jnp.broadcast_to(q[t:t + 1, :], (N_HEADS, width)), 0.0)
         for t in range(dec_seq)], axis=0).astype(BF16)

    knew_ref[...] = jnp.zeros(knew_ref.shape, F32)
    vnew_ref[...] = jnp.zeros(vnew_ref.shape, F32)
    knew_ref[0:dec_seq, :] = kn_ref[0]
    vnew_ref[0:dec_seq, :] = vn_ref[0]

    s_past = jnp.dot(wt, kst_ref[0].astype(BF16), preferred_element_type=F32)
    s_new = _dot_nt(wt, knew_ref[...].astype(BF16))

    row = lax.broadcasted_iota(jnp.int32, (n_rows, 1), 0)
    step = row // N_HEADS
    slope_col = jnp.zeros((n_rows, 1), F32)
    for h in range(N_HEADS):
        slope_col = jnp.where(row % N_HEADS == h, slopes_ref[h], slope_col)

    def weigh(s, key_pos):
        delta = n_past + step - key_pos
        c = _branch_count(delta)
        s = jnp.where(c > 0.0, s - slope_col * delta.astype(F32), NEG_INF)
        return s, c

    s_past, c_past = weigh(s_past, lax.broadcasted_iota(jnp.int32, (1, n_past), 1))
    s_new, c_new = weigh(s_new, n_past + lax.broadcasted_iota(jnp.int32, (1, V7X_LANES), 1))
    m = jnp.maximum(jnp.max(s_past, axis=-1, keepdims=True),
                    jnp.max(s_new, axis=-1, keepdims=True))
    p_past = c_past * jnp.exp(s_past - m)
    p_new = c_new * jnp.exp(s_new - m)
    l = jnp.sum(p_past, axis=-1, keepdims=True) + jnp.sum(p_new, axis=-1, keepdims=True)
    o_all = (_dot_nt(p_past.astype(BF16), vst_ref[0].astype(BF16))
             + jnp.dot(p_new.astype(BF16), vnew_ref[...].astype(BF16),
                       preferred_element_type=F32)) / l
    outs = []
    for t in range(dec_seq):
        blk = o_all[t * N_HEADS:(t + 1) * N_HEADS, :]
        outs.append(jnp.sum(jnp.where(head_mask, blk, 0.0), axis=0, keepdims=True))
    o_ref[0] = jnp.concatenate(outs, axis=0).astype(o_ref.dtype)


def _ffn_kernel(x_ref, a_ref, c_ref, wo_ref, g2_ref, wg_ref, wu_ref, wd_ref, gf_ref, y_ref, *,
                final_norm):
    mix = jnp.concatenate([a_ref[...].astype(BF16), c_ref[...].astype(BF16)], axis=-1)
    h = x_ref[...] + jnp.dot(mix, wo_ref[...], preferred_element_type=F32)
    hn = _rms(h, g2_ref[...]).astype(BF16)
    gate = jnp.dot(hn, wg_ref[...], preferred_element_type=F32)
    up = jnp.dot(hn, wu_ref[...], preferred_element_type=F32)
    act = (gate * (1.0 / (1.0 + jnp.exp(-gate)))) * up
    y = h + jnp.dot(act.astype(BF16), wd_ref[...], preferred_element_type=F32)
    y_ref[...] = _rms(y, gf_ref[...]) if final_norm else y


def _nbytes(shape, dtype):
    n = 1
    for s in shape:
        n *= s
    return n * jnp.dtype(dtype).itemsize


def _vmem_limit(pipelined, resident, temporaries):
    need = 2 * sum(pipelined) + sum(resident) + sum(temporaries)
    return min(V7X_VMEM_BYTES, need + need // 4)


def _resident(shape):
    zeros = (0,) * len(shape)
    return pl.BlockSpec(shape, lambda *_: zeros, pipeline_mode=pl.Buffered(1))


def _row_tile(rows, largest=512):
    for tm in (largest, 512, 256, 128):
        if rows % tm == 0:
            return tm
    raise ValueError(f"row count {rows} is not a multiple of 128")


def _inproj_prompt(x, g, w_cc, w_qkvt, wc, seq):
    rows, d_model = x.shape
    d_conv = wc.shape[1]
    tm = _row_tile(seq, largest=ROW_GROUPS * 512)
    tiles_per_seq = seq // tm
    n_seq = rows // seq
    row_blk = lambda width: pl.BlockSpec((tm, width), lambda b, j: (b * tiles_per_seq + j, 0))
    col_blk = pl.BlockSpec((1, D_ATTN, tm), lambda b, j: (b, 0, j))
    limit = _vmem_limit(
        [_nbytes((tm, d_model), F32), _nbytes((D_ATTN, tm), BF16), 2 * _nbytes((D_ATTN, tm), F32),
         _nbytes((tm, d_conv), BF16)],
        [_nbytes(w_cc.shape, BF16), _nbytes(w_qkvt.shape, BF16)],
        [2 * _nbytes((tm, d_model), F32), 2 * _nbytes((tm, w_cc.shape[1]), F32),
         2 * _nbytes((w_qkvt.shape[0], tm), F32), 4 * _nbytes((tm, d_conv), F32)])
    return pl.pallas_call(
        _inproj_prompt_kernel,
        grid=(n_seq, tiles_per_seq),
        in_specs=[row_blk(d_model), _resident((1, d_model)), _resident(w_cc.shape),
                  _resident(w_qkvt.shape), _resident(wc.shape)],
        out_specs=[col_blk, col_blk, col_blk, row_blk(d_conv),
                   pl.BlockSpec((1, V7X_SUBLANES, d_conv), lambda b, j: (b, 0, 0))],
        out_shape=[jax.ShapeDtypeStruct((n_seq, D_ATTN, seq), BF16),
                   jax.ShapeDtypeStruct((n_seq, D_ATTN, seq), F32),
                   jax.ShapeDtypeStruct((n_seq, D_ATTN, seq), F32),
                   jax.ShapeDtypeStruct((rows, d_conv), BF16),
                   jax.ShapeDtypeStruct((n_seq, V7X_SUBLANES, d_conv), F32)],
        scratch_shapes=[pltpu.VMEM((tm + V7X_SUBLANES, d_conv), F32)],
        compiler_params=pltpu.CompilerParams(dimension_semantics=("arbitrary", "arbitrary"),
                                             vmem_limit_bytes=limit),
        name="inproj_prompt",
    )(x, g, w_cc, w_qkvt, wc)


def _inproj_sample(x, g, w_cc, w_qkv, wc, prev1, prev2, dec_seq):
    rows, d_model = x.shape
    d_conv = wc.shape[1]
    full = lambda shape: pl.BlockSpec(shape, lambda i: (0,) * len(shape))
    return pl.pallas_call(
        functools.partial(_inproj_sample_kernel, dec_seq=dec_seq),
        grid=(1,),
        in_specs=[full(x.shape), full((1, d_model)), full(w_cc.shape), full(w_qkv.shape),
                  full(wc.shape), full(prev1.shape), full(prev2.shape)],
        out_specs=[full((rows, D_ATTN))] * 3 + [full((rows, d_conv))] * 2,
        out_shape=[jax.ShapeDtypeStruct((rows, D_ATTN), F32)] * 3
        + [jax.ShapeDtypeStruct((rows, d_conv), BF16), jax.ShapeDtypeStruct((rows, d_conv), F32)],
        scratch_shapes=[pltpu.VMEM((rows + V7X_SUBLANES, d_conv), F32)],
        compiler_params=pltpu.CompilerParams(dimension_semantics=("arbitrary",)),
        name="inproj_sample",
    )(x, g, w_cc, w_qkv, wc, prev1, prev2)


def _attn_prompt(slopes, qt, kt, vt):
    n_seq, _, seq = qt.shape
    rows = n_seq * seq
    o_blk = pl.BlockSpec((seq, V7X_LANES), lambda b, p: (b, p))
    kv_blk = pl.BlockSpec((1, V7X_LANES, seq), lambda b, p: (b, p, 0))
    dist, logc = _attn_tables(seq)
    table = _nbytes((TQ, seq), F32)
    limit = _vmem_limit(
        [2 * _nbytes((seq, V7X_LANES), F32), 2 * _nbytes((seq, V7X_LANES), BF16)],
        [2 * _nbytes((seq, V7X_LANES), BF16), (2 + HEADS_PER_STEP) * table],
        [3 * HEADS_PER_STEP * table])
    return pl.pallas_call(
        _attn_prompt_kernel,
        grid=(n_seq, N_HEADS // HEADS_PER_STEP),
        in_specs=[pl.BlockSpec(memory_space=pltpu.SMEM), _resident((seq, TQ)),
                  _resident((seq, TQ)), kv_blk, kv_blk, kv_blk],
        out_specs=o_blk,
        out_shape=jax.ShapeDtypeStruct((rows, D_ATTN), BF16),
        scratch_shapes=[pltpu.VMEM((seq, V7X_LANES), BF16), pltpu.VMEM((V7X_LANES, seq), BF16),
                        pltpu.VMEM((HEADS_PER_STEP, seq, TQ), F32)],
        compiler_params=pltpu.CompilerParams(dimension_semantics=("arbitrary", "arbitrary"),
                                             vmem_limit_bytes=limit),
        name="attn_prompt",
    )(slopes, dist, logc, qt, kt, vt)


def _attn_sample(slopes, q, kn, vn, kst, vst):
    n_seq, dec_seq, width = q.shape
    n_past = kst.shape[2]
    new_blk = pl.BlockSpec((1, dec_seq, width), lambda b: (b, 0, 0))
    past_blk = pl.BlockSpec((1, width, n_past), lambda b: (b, 0, 0))
    limit = _vmem_limit(
        [2 * _nbytes((n_past, width), F32)], [],
        [2 * _nbytes((n_past, width), BF16), 8 * _nbytes((dec_seq * N_HEADS, n_past), F32)])
    return pl.pallas_call(
        _attn_sample_kernel,
        grid=(n_seq,),
        in_specs=[pl.BlockSpec(memory_space=pltpu.SMEM), new_blk, new_blk, new_blk,
                  past_blk, past_blk],
        out_specs=new_blk,
        out_shape=jax.ShapeDtypeStruct((n_seq, dec_seq, width), F32),
        scratch_shapes=[pltpu.VMEM((V7X_LANES, width), F32)] * 2,
        compiler_params=pltpu.CompilerParams(dimension_semantics=("arbitrary",),
                                             vmem_limit_bytes=limit),
        name="attn_sample",
    )(slopes, q, kn, vn, kst, vst)


def _ffn(x, attn_o, conv_o, wo, g2, wg, wu, wd, gf, final_norm, tm):
    rows, d_model = x.shape
    d_ff = wg.shape[1]
    row_blk = lambda width: pl.BlockSpec((tm, width), lambda i: (i, 0))
    limit = _vmem_limit(
        [2 * _nbytes((tm, d_model), F32), _nbytes((tm, attn_o.shape[1]), attn_o.dtype),
         _nbytes((tm, conv_o.shape[1]), conv_o.dtype)],
        [_nbytes(w.shape, BF16) for w in (wo, wg, wu, wd)],
        [4 * _nbytes((tm, d_model), F32), 3 * _nbytes((tm, d_ff), F32)])
    return pl.pallas_call(
        functools.partial(_ffn_kernel, final_norm=final_norm),
        grid=(rows // tm,),
        in_specs=[row_blk(d_model), row_blk(attn_o.shape[1]), row_blk(conv_o.shape[1]),
                  _resident(wo.shape), _resident((1, d_model)), _resident(wg.shape),
                  _resident(wu.shape), _resident(wd.shape), _resident((1, d_model))],
        out_specs=row_blk(d_model),
        out_shape=jax.ShapeDtypeStruct((rows, d_model), F32),
        compiler_params=pltpu.CompilerParams(dimension_semantics=("arbitrary",),
                                             vmem_limit_bytes=limit),
        name="ffn",
    )(x, attn_o, conv_o, wo, g2, wg, wu, wd, gf)


def _alibi_slopes():
    return jnp.exp2(-8.0 * jnp.arange(1, N_HEADS + 1, dtype=F32) / N_HEADS)


def kernel(x_prompt, x_sample, state_attn_k, state_attn_v, state_conv, norm_mix_g, w_in, w_conv,
           w_out, norm_ffn_g, w_gate, w_up, w_down, norm_final_g):
    depth = w_in.shape[0]
    batch, seq, d_model = x_prompt.shape
    dec_batch, dec_seq, _ = x_sample.shape
    d_conv = w_conv.shape[2]
    n_past = state_attn_k.shape[2]
    assert n_past >= max(WINDOWS) and seq <= max(WINDOWS) and seq % TQ == 0
    assert CONV_WIDTH - 1 <= dec_seq <= V7X_LANES
    slopes = _alibi_slopes()
    gf = norm_final_g.reshape(1, d_model)

    yp = x_prompt.reshape(batch * seq, d_model)
    ys = x_sample.reshape(dec_batch * dec_seq, d_model)
    outs = [[] for _ in range(6)]
    for layer in range(depth):
        g1 = norm_mix_g[layer].reshape(1, d_model)
        g2 = norm_ffn_g[layer].reshape(1, d_model)
        w_in_bf = w_in[layer].astype(BF16)
        w_qkv = w_in_bf[:, :3 * D_ATTN]
        w_cc = w_in_bf[:, 3 * D_ATTN:]
        weights = [w.astype(BF16) for w in (w_out[layer], w_gate[layer], w_up[layer], w_down[layer])]
        wc = w_conv[layer]

        qt, kt, vt, conv_o, u_last = _inproj_prompt(yp, g1, w_cc, w_qkv.T, wc, seq)
        attn_o = _attn_prompt(slopes, qt, kt, vt)
        y_layer = _ffn(yp, attn_o, conv_o, weights[0], g2, *weights[1:], gf,
                       layer == depth - 1, _row_tile(seq))
        to_heads = lambda a: a.reshape(batch, N_HEADS, HEAD_DIM, seq).transpose(0, 3, 1, 2)
        outs[0].append(to_heads(kt))
        outs[1].append(to_heads(vt))
        outs[2].append(u_last[:, V7X_SUBLANES - (CONV_WIDTH - 1):, :])
        yp = y_layer

        st = state_conv[layer]
        prev1 = jnp.pad(st[:, 1:2], ((0, 0), (0, dec_seq - 1), (0, 0)))
        prev2 = jnp.pad(st, ((0, 0), (0, dec_seq - 2), (0, 0)))
        qs, ks, vs, conv_s, u_s = _inproj_sample(
            ys, g1, w_cc, w_qkv, wc, prev1.reshape(-1, d_conv), prev2.reshape(-1, d_conv), dec_seq)
        as3 = lambda a: a.reshape(dec_batch, dec_seq, D_ATTN)
        feature_major = lambda a: a.transpose(0, 2, 3, 1).reshape(dec_batch, D_ATTN, n_past)
        attn_s = _attn_sample(slopes, as3(qs), as3(ks), as3(vs),
                              feature_major(state_attn_k[layer]),
                              feature_major(state_attn_v[layer]))
        ys = _ffn(ys, attn_s.reshape(-1, D_ATTN), conv_s, weights[0], g2, *weights[1:], gf,
                  layer == depth - 1, ys.shape[0])
        outs[3].append(ks.reshape(dec_batch, dec_seq, N_HEADS, HEAD_DIM))
        outs[4].append(vs.reshape(dec_batch, dec_seq, N_HEADS, HEAD_DIM))
        outs[5].append(u_s.reshape(dec_batch, dec_seq, d_conv)[:, dec_seq - (CONV_WIDTH - 1):])

    y_prompt = yp.reshape(batch, seq, d_model)
    y_sample = ys.reshape(dec_batch, dec_seq, d_model)
    new_k_p, new_v_p, new_c_p, new_k_s, new_v_s, new_c_s = [jnp.stack(o) for o in outs]
    return (y_prompt, y_sample, new_k_p, new_v_p, new_c_p, new_k_s, new_v_s, new_c_s)
```

```python
import functools

import jax
import jax.numpy as jnp
import numpy as np
from jax import lax
from jax.experimental import pallas as pl
from jax.experimental.pallas import tpu as pltpu

F32 = jnp.float32
BF16 = jnp.bfloat16

HEAD_DIM = 64
N_HEADS = 8
D_ATTN = N_HEADS * HEAD_DIM
CONV_WIDTH = 3
WINDOWS = (128, 512, 2048)
DILATIONS = (1, 4, 16)
RMS_EPS = 1e-6
ATTN_SCALE = HEAD_DIM ** -0.5
LOG2_E = 1.4426950408889634
NEG_INF = -1e30

V7X_LANES = 128
V7X_SUBLANES = 8
V7X_VMEM_BYTES = 64 * 1024 * 1024

HEADS_PER_STEP = V7X_LANES // HEAD_DIM
TQ = 256
TK = 256
ATTN_STREAMS = 4
ATTN_LOOKAHEAD = 1
ROW_GROUPS = 2


def _rms(x, g):
    y = x * lax.rsqrt(jnp.mean(x * x, axis=-1, keepdims=True) + RMS_EPS)
    return y * g


def _branch_count(delta):
    nonneg = delta >= 0
    c = jnp.zeros(delta.shape, F32)
    for w, d in zip(WINDOWS, DILATIONS):
        hit = nonneg & ((delta & (d - 1)) == 0) & (delta <= w)
        c = c + hit.astype(F32)
    return c


def _dot_nt(a, b):
    return lax.dot_general(a, b, (((1,), (1,)), ((), ())), preferred_element_type=F32)


def _project(x, g, wcc_ref):
    d_conv = wcc_ref.shape[1] // 3
    xn = _rms(x, g).astype(BF16)
    zc = jnp.dot(xn, wcc_ref[...], preferred_element_type=F32)
    return xn, zc[:, 0:d_conv], zc[:, d_conv:2 * d_conv], zc[:, 2 * d_conv:3 * d_conv]


def _conv3(wc, u2, u1, u0):
    acc = wc[0:1, :] * u2
    acc = acc + wc[1:2, :] * u1
    return acc + wc[2:3, :] * u0


def _inproj_prompt_kernel(x_ref, g_ref, wcc_ref, wqkvt_ref, wc_ref, qt_ref, kt_ref, vt_ref, co_ref,
                          ulast_ref, uext_ref):
    tm = x_ref.shape[0]
    j = pl.program_id(1)
    tiles_per_seq = pl.num_programs(1)

    @pl.when(j == 0)
    def _():
        uext_ref[0:V7X_SUBLANES, :] = jnp.zeros((V7X_SUBLANES, uext_ref.shape[1]), F32)

    rg = tm // ROW_GROUPS
    for r0 in range(0, tm, rg):
        xn, hc, gb, gc = _project(x_ref[r0:r0 + rg, :], g_ref[...], wcc_ref)
        qkvt = _dot_nt(wqkvt_ref[...], xn)
        qt_ref[0, :, r0:r0 + rg] = (qkvt[0:D_ATTN, :] * (ATTN_SCALE * LOG2_E)).astype(qt_ref.dtype)
        kt_ref[0, :, r0:r0 + rg] = qkvt[D_ATTN:2 * D_ATTN, :]
        vt_ref[0, :, r0:r0 + rg] = qkvt[2 * D_ATTN:3 * D_ATTN, :]
        u = gc * hc
        uext_ref[V7X_SUBLANES:V7X_SUBLANES + rg, :] = u
        u1 = uext_ref[V7X_SUBLANES - 1:V7X_SUBLANES - 1 + rg, :]
        u2 = uext_ref[V7X_SUBLANES - 2:V7X_SUBLANES - 2 + rg, :]
        co_ref[r0:r0 + rg, :] = (gb * _conv3(wc_ref[...], u2, u1, u)).astype(co_ref.dtype)
        tail = u[rg - V7X_SUBLANES:rg, :]
        uext_ref[0:V7X_SUBLANES, :] = tail

    @pl.when(j == tiles_per_seq - 1)
    def _():
        ulast_ref[0] = tail


def _inproj_sample_kernel(x_ref, g_ref, wcc_ref, wqkv_ref, wc_ref, p1_ref, p2_ref, q_ref, k_ref,
                          v_ref, co_ref, u_ref, uext_ref, *, dec_seq):
    rows = x_ref.shape[0]
    xn, hc, gb, gc = _project(x_ref[...], g_ref[...], wcc_ref)
    qkv = jnp.dot(xn, wqkv_ref[...], preferred_element_type=F32)
    q_ref[...] = qkv[:, 0:D_ATTN] * ATTN_SCALE
    k_ref[...] = qkv[:, D_ATTN:2 * D_ATTN]
    v_ref[...] = qkv[:, 2 * D_ATTN:3 * D_ATTN]
    u = gc * hc
    u_ref[...] = u
    uext_ref[0:V7X_SUBLANES, :] = jnp.zeros((V7X_SUBLANES, u.shape[1]), F32)
    uext_ref[V7X_SUBLANES:V7X_SUBLANES + rows, :] = u
    t = lax.broadcasted_iota(jnp.int32, u.shape, 0) % dec_seq
    u1 = jnp.where(t >= 1, uext_ref[V7X_SUBLANES - 1:V7X_SUBLANES - 1 + rows, :], p1_ref[...])
    u2 = jnp.where(t >= 2, uext_ref[V7X_SUBLANES - 2:V7X_SUBLANES - 2 + rows, :], p2_ref[...])
    co_ref[...] = (gb * _conv3(wc_ref[...], u2, u1, u)).astype(co_ref.dtype)


def _attn_tables(seq):
    x = np.arange(seq, dtype=np.int32)[:, None]
    i = np.arange(TQ, dtype=np.int32)[None, :]
    delta = i + (seq - TQ) - x
    c = np.zeros(delta.shape, np.float32)
    for w, d in zip(WINDOWS, DILATIONS):
        c += (delta >= 0) & (delta % d == 0) & (delta <= w)
    logc = np.where(c > 0.0, np.log2(np.maximum(c, 1.0)), NEG_INF).astype(np.float32)
    return delta.astype(np.float32), logc


def _attn_prompt_kernel(slopes_ref, dist_ref, logc_ref, qt_ref, kt_ref, vt_ref, o_ref,
                        kb_ref, vtb_ref, bm_ref):
    seq = qt_ref.shape[2]
    pair = pl.program_id(0)
    feature = lax.broadcasted_iota(jnp.int32, (V7X_LANES, 1), 0)
    first_head = feature < HEAD_DIM

    @pl.when(pl.program_id(1) == 0)
    def _():
        for h in range(HEADS_PER_STEP):
            slope = slopes_ref[HEADS_PER_STEP * pair + h]
            bm_ref[h] = logc_ref[...] - (slope * LOG2_E) * dist_ref[...]

    kb_ref[...] = kt_ref[0].T.astype(BF16)
    vtb_ref[...] = vt_ref[0].astype(BF16)

    def score_half(st, chunk):
        k0, n = chunk
        h, row0 = st["h"], seq - (st["q0"] + TQ)
        s = (jnp.dot(kb_ref[k0:k0 + n, :], st["q"], preferred_element_type=F32)
             + bm_ref[h, row0 + k0:row0 + k0 + n, :])
        c8 = jnp.max(s.reshape(n // V7X_SUBLANES, V7X_SUBLANES, TQ), axis=0)
        m_new = jnp.maximum(st["m"], jnp.max(c8, axis=0, keepdims=True))
        pending = dict(k0=k0, n=n, s=s, m=m_new, alpha=jnp.exp2(st["m"] - m_new))
        st["m"] = m_new
        return pending

    def value_half(st, pending):
        h, k0, n = st["h"], pending["k0"], pending["n"]
        p = jnp.exp2(pending["s"] - pending["m"])
        st["l8"] = (pending["alpha"] * st["l8"]
                    + jnp.sum(p.reshape(n // V7X_SUBLANES, V7X_SUBLANES, TQ), axis=0))
        vt_h = vtb_ref[h * HEAD_DIM:(h + 1) * HEAD_DIM, k0:k0 + n]
        st["acc"] = pending["alpha"] * st["acc"] + jnp.dot(vt_h, p.astype(BF16),
                                                           preferred_element_type=F32)

    blocks_per_group = ATTN_STREAMS // HEADS_PER_STEP
    for g0 in range(0, seq, TQ * blocks_per_group):
        streams = []
        for q0 in range(g0, g0 + TQ * blocks_per_group, TQ):
            qt = qt_ref[0, :, q0:q0 + TQ]
            zero = jnp.zeros_like(qt)
            for h, qh in enumerate([jnp.where(first_head, qt, zero), jnp.where(first_head, zero, qt)]):
                chunks = [(q0, TQ)] + [(max(k1 - TK, 0), min(TK, k1)) for k1 in range(q0, 0, -TK)]
                streams.append(dict(q0=q0, h=h, q=qh, chunks=chunks,
                                    m=jnp.full((1, TQ), NEG_INF, F32),
                                    l8=jnp.zeros((V7X_SUBLANES, TQ), F32),
                                    acc=jnp.zeros((HEAD_DIM, TQ), F32)))
        waiting = []
        for t in range(max(len(st["chunks"]) for st in streams)):
            live = [(st, st["chunks"][t]) for st in streams if t < len(st["chunks"])]
            due = waiting.pop(0) if len(waiting) >= ATTN_LOOKAHEAD else []
            issued = []
            for i, (st, k0) in enumerate(live):
                issued.append((st, score_half(st, k0)))
                if i < len(due):
                    value_half(*due[i])
            for st, pending in due[len(live):]:
                value_half(st, pending)
            waiting.append(issued)
        for step in waiting:
            for st, pending in step:
                value_half(st, pending)
        for i in range(0, len(streams), HEADS_PER_STEP):
            outs = [st["acc"] / jnp.sum(st["l8"], axis=0, keepdims=True)
                    for st in streams[i:i + HEADS_PER_STEP]]
            q0 = streams[i]["q0"]
            o_ref[q0:q0 + TQ, :] = jnp.concatenate(outs, axis=0).T.astype(o_ref.dtype)


def _attn_sample_kernel(slopes_ref, q_ref, kn_ref, vn_ref, kst_ref, vst_ref, o_ref,
                        knew_ref, vnew_ref):
    dec_seq = q_ref.shape[1]
    n_past = kst_ref.shape[2]
    width = q_ref.shape[2]
    n_rows = dec_seq * N_HEADS

    q = q_ref[0]
    head_of_lane = lax.broadcasted_iota(jnp.int32, (N_HEADS, width), 1) // HEAD_DIM
    head_of_row = lax.broadcasted_iota(jnp.int32, (N_HEADS, width), 0)
    head_mask = head_of_lane == head_of_row
    wt = jnp.concatenate(
        [jnp.where(head_mask, jnp.broadcast_to(q[t:t + 1, :], (N_HEADS, width)), 0.0)
         for t in range(dec_seq)], axis=0).astype(BF16)

    knew_ref[...] = jnp.zeros(knew_ref.shape, F32)
    vnew_ref[...] = jnp.zeros(vnew_ref.shape, F32)
    knew_ref[0:dec_seq, :] = kn_ref[0]
    vnew_ref[0:dec_seq, :] = vn_ref[0]

    s_past = jnp.dot(wt, kst_ref[0].astype(BF16), preferred_element_type=F32)
    s_new = _dot_nt(wt, knew_ref[...].astype(BF16))

    row = lax.broadcasted_iota(jnp.int32, (n_rows, 1), 0)
    step = row // N_HEADS
    slope_col = jnp.zeros((n_rows, 1), F32)
    for h in range(N_HEADS):
        slope_col = jnp.where(row % N_HEADS == h, slopes_ref[h], slope_col)

    def weigh(s, key_pos):
        delta = n_past + step - key_pos
        c = _branch_count(delta)
        s = jnp.where(c > 0.0, s - slope_col * delta.astype(F32), NEG_INF)
        return s, c

    s_past, c_past = weigh(s_past, lax.broadcasted_iota(jnp.int32, (1, n_past), 1))
    s_new, c_new = weigh(s_new, n_past + lax.broadcasted_iota(jnp.int32, (1, V7X_LANES), 1))
    m = jnp.maximum(jnp.max(s_past, axis=-1, keepdims=True),
                    jnp.max(s_new, axis=-1, keepdims=True))
    p_past = c_past * jnp.exp(s_past - m)
    p_new = c_new * jnp.exp(s_new - m)
    l = jnp.sum(p_past, axis=-1, keepdims=True) + jnp.sum(p_new, axis=-1, keepdims=True)
    o_all = (_dot_nt(p_past.astype(BF16), vst_ref[0].astype(BF16))
             + jnp.dot(p_new.astype(BF16), vnew_ref[...].astype(BF16),
                       preferred_element_type=F32)) / l
    outs = []
    for t in range(dec_seq):
        blk = o_all[t * N_HEADS:(t + 1) * N_HEADS, :]
        outs.append(jnp.sum(jnp.where(head_mask, blk, 0.0), axis=0, keepdims=True))
    o_ref[0] = jnp.concatenate(outs, axis=0).astype(o_ref.dtype)


def _ffn_kernel(x_ref, a_ref, c_ref, wo_ref, g2_ref, wg_ref, wu_ref, wd_ref, gf_ref, y_ref, *,
                final_norm):
    mix = jnp.concatenate([a_ref[...].astype(BF16), c_ref[...].astype(BF16)], axis=-1)
    h = x_ref[...] + jnp.dot(mix, wo_ref[...], preferred_element_type=F32)
    hn = _rms(h, g2_ref[...]).astype(BF16)
    gate = jnp.dot(hn, wg_ref[...], preferred_element_type=F32)
    up = jnp.dot(hn, wu_ref[...], preferred_element_type=F32)
    act = (gate * (1.0 / (1.0 + jnp.exp(-gate)))) * up
    y = h + jnp.dot(act.astype(BF16), wd_ref[...], preferred_element_type=F32)
    y_ref[...] = _rms(y, gf_ref[...]) if final_norm else y


def _nbytes(shape, dtype):
    n = 1
    for s in shape:
        n *= s
    return n * jnp.dtype(dtype).itemsize


def _vmem_limit(pipelined, resident, temporaries):
    need = 2 * sum(pipelined) + sum(resident) + sum(temporaries)
    return min(V7X_VMEM_BYTES, need + need // 4)


def _resident(shape):
    zeros = (0,) * len(shape)
    return pl.BlockSpec(shape, lambda *_: zeros, pipeline_mode=pl.Buffered(1))


def _row_tile(rows, largest=512):
    for tm in (largest, 512, 256, 128):
        if rows % tm == 0:
            return tm
    raise ValueError(f"row count {rows} is not a multiple of 128")


def _inproj_prompt(x, g, w_cc, w_qkvt, wc, seq):
    rows, d_model = x.shape
    d_conv = wc.shape[1]
    tm = _row_tile(seq, largest=ROW_GROUPS * 512)
    tiles_per_seq = seq // tm
    n_seq = rows // seq
    row_blk = lambda width: pl.BlockSpec((tm, width), lambda b, j: (b * tiles_per_seq + j, 0))
    col_blk = pl.BlockSpec((1, D_ATTN, tm), lambda b, j: (b, 0, j))
    limit = _vmem_limit(
        [_nbytes((tm, d_model), F32), _nbytes((D_ATTN, tm), BF16), 2 * _nbytes((D_ATTN, tm), F32),
         _nbytes((tm, d_conv), BF16)],
        [_nbytes(w_cc.shape, BF16), _nbytes(w_qkvt.shape, BF16)],
        [2 * _nbytes((tm, d_model), F32), 2 * _nbytes((tm, w_cc.shape[1]), F32),
         2 * _nbytes((w_qkvt.shape[0], tm), F32), 4 * _nbytes((tm, d_conv), F32)])
    return pl.pallas_call(
        _inproj_prompt_kernel,
        grid=(n_seq, tiles_per_seq),
        in_specs=[row_blk(d_model), _resident((1, d_model)), _resident(w_cc.shape),
                  _resident(w_qkvt.shape), _resident(wc.shape)],
        out_specs=[col_blk, col_blk, col_blk, row_blk(d_conv),
                   pl.BlockSpec((1, V7X_SUBLANES, d_conv), lambda b, j: (b, 0, 0))],
        out_shape=[jax.ShapeDtypeStruct((n_seq, D_ATTN, seq), BF16),
                   jax.ShapeDtypeStruct((n_seq, D_ATTN, seq), F32),
                   jax.ShapeDtypeStruct((n_seq, D_ATTN, seq), F32),
                   jax.ShapeDtypeStruct((rows, d_conv), BF16),
                   jax.ShapeDtypeStruct((n_seq, V7X_SUBLANES, d_conv), F32)],
        scratch_shapes=[pltpu.VMEM((tm + V7X_SUBLANES, d_conv), F32)],
        compiler_params=pltpu.CompilerParams(dimension_semantics=("arbitrary", "arbitrary"),
                                             vmem_limit_bytes=limit),
        name="inproj_prompt",
    )(x, g, w_cc, w_qkvt, wc)


def _inproj_sample(x, g, w_cc, w_qkv, wc, prev1, prev2, dec_seq):
    rows, d_model = x.shape
    d_conv = wc.shape[1]
    full = lambda shape: pl.BlockSpec(shape, lambda i: (0,) * len(shape))
    return pl.pallas_call(
        functools.partial(_inproj_sample_kernel, dec_seq=dec_seq),
        grid=(1,),
        in_specs=[full(x.shape), full((1, d_model)), full(w_cc.shape), full(w_qkv.shape),
                  full(wc.shape), full(prev1.shape), full(prev2.shape)],
        out_specs=[full((rows, D_ATTN))] * 3 + [full((rows, d_conv))] * 2,
        out_shape=[jax.ShapeDtypeStruct((rows, D_ATTN), F32)] * 3
        + [jax.ShapeDtypeStruct((rows, d_conv), BF16), jax.ShapeDtypeStruct((rows, d_conv), F32)],
        scratch_shapes=[pltpu.VMEM((rows + V7X_SUBLANES, d_conv), F32)],
        compiler_params=pltpu.CompilerParams(dimension_semantics=("arbitrary",)),
        name="inproj_sample",
    )(x, g, w_cc, w_qkv, wc, prev1, prev2)


def _attn_prompt(slopes, qt, kt, vt):
    n_seq, _, seq = qt.shape
    rows = n_seq * seq
    o_blk = pl.BlockSpec((seq, V7X_LANES), lambda p, b: (b, p))
    kv_blk = pl.BlockSpec((1, V7X_LANES, seq), lambda p, b: (b, p, 0))
    dist, logc = _attn_tables(seq)
    table = _nbytes((TQ, seq), F32)
    limit = _vmem_limit(
        [2 * _nbytes((seq, V7X_LANES), F32), 2 * _nbytes((seq, V7X_LANES), BF16)],
        [2 * _nbytes((seq, V7X_LANES), BF16), (2 + HEADS_PER_STEP) * table],
        [3 * HEADS_PER_STEP * table])
    return pl.pallas_call(
        _attn_prompt_kernel,
        grid=(N_HEADS // HEADS_PER_STEP, n_seq),
        in_specs=[pl.BlockSpec(memory_space=pltpu.SMEM), _resident((seq, TQ)),
                  _resident((seq, TQ)), kv_blk, kv_blk, kv_blk],
        out_specs=o_blk,
        out_shape=jax.ShapeDtypeStruct((rows, D_ATTN), BF16),
        scratch_shapes=[pltpu.VMEM((seq, V7X_LANES), BF16), pltpu.VMEM((V7X_LANES, seq), BF16),
                        pltpu.VMEM((HEADS_PER_STEP, seq, TQ), F32)],
        compiler_params=pltpu.CompilerParams(dimension_semantics=("arbitrary", "arbitrary"),
                                             vmem_limit_bytes=limit),
        name="attn_prompt",
    )(slopes, dist, logc, qt, kt, vt)


def _attn_sample(slopes, q, kn, vn, kst, vst):
    n_seq, dec_seq, width = q.shape
    n_past = kst.shape[2]
    new_blk = pl.BlockSpec((1, dec_seq, width), lambda b: (b, 0, 0))
    past_blk = pl.BlockSpec((1, width, n_past), lambda b: (b, 0, 0))
    limit = _vmem_limit(
        [2 * _nbytes((n_past, width), F32)], [],
        [2 * _nbytes((n_past, width), BF16), 8 * _nbytes((dec_seq * N_HEADS, n_past), F32)])
    return pl.pallas_call(
        _attn_sample_kernel,
        grid=(n_seq,),
        in_specs=[pl.BlockSpec(memory_space=pltpu.SMEM), new_blk, new_blk, new_blk,
                  past_blk, past_blk],
        out_specs=new_blk,
        out_shape=jax.ShapeDtypeStruct((n_seq, dec_seq, width), F32),
        scratch_shapes=[pltpu.VMEM((V7X_LANES, width), F32)] * 2,
        compiler_params=pltpu.CompilerParams(dimension_semantics=("arbitrary",),
                                             vmem_limit_bytes=limit),
        name="attn_sample",
    )(slopes, q, kn, vn, kst, vst)


def _ffn(x, attn_o, conv_o, wo, g2, wg, wu, wd, gf, final_norm, tm):
    rows, d_model = x.shape
    d_ff = wg.shape[1]
    row_blk = lambda width: pl.BlockSpec((tm, width), lambda i: (i, 0))
    limit = _vmem_limit(
        [2 * _nbytes((tm, d_model), F32), _nbytes((tm, attn_o.shape[1]), attn_o.dtype),
         _nbytes((tm, conv_o.shape[1]), conv_o.dtype)],
        [_nbytes(w.shape, BF16) for w in (wo, wg, wu, wd)],
        [4 * _nbytes((tm, d_model), F32), 3 * _nbytes((tm, d_ff), F32)])
    return pl.pallas_call(
        functools.partial(_ffn_kernel, final_norm=final_norm),
        grid=(rows // tm,),
        in_specs=[row_blk(d_model), row_blk(attn_o.shape[1]), row_blk(conv_o.shape[1]),
                  _resident(wo.shape), _resident((1, d_model)), _resident(wg.shape),
                  _resident(wu.shape), _resident(wd.shape), _resident((1, d_model))],
        out_specs=row_blk(d_model),
        out_shape=jax.ShapeDtypeStruct((rows, d_model), F32),
        compiler_params=pltpu.CompilerParams(dimension_semantics=("arbitrary",),
                                             vmem_limit_bytes=limit),
        name="ffn",
    )(x, attn_o, conv_o, wo, g2, wg, wu, wd, gf)


def _alibi_slopes():
    return jnp.exp2(-8.0 * jnp.arange(1, N_HEADS + 1, dtype=F32) / N_HEADS)


def kernel(x_prompt, x_sample, state_attn_k, state_attn_v, state_conv, norm_mix_g, w_in, w_conv,
           w_out, norm_ffn_g, w_gate, w_up, w_down, norm_final_g):
    depth = w_in.shape[0]
    batch, seq, d_model = x_prompt.shape
    dec_batch, dec_seq, _ = x_sample.shape
    d_conv = w_conv.shape[2]
    n_past = state_attn_k.shape[2]
    assert n_past >= max(WINDOWS) and seq <= max(WINDOWS) and seq % TQ == 0
    assert CONV_WIDTH - 1 <= dec_seq <= V7X_LANES
    slopes = _alibi_slopes()
    gf = norm_final_g.reshape(1, d_model)

    yp = x_prompt.reshape(batch * seq, d_model)
    ys = x_sample.reshape(dec_batch * dec_seq, d_model)
    outs = [[] for _ in range(6)]
    for layer in range(depth):
        g1 = norm_mix_g[layer].reshape(1, d_model)
        g2 = norm_ffn_g[layer].reshape(1, d_model)
        w_in_bf = w_in[layer].astype(BF16)
        w_qkv = w_in_bf[:, :3 * D_ATTN]
        w_cc = w_in_bf[:, 3 * D_ATTN:]
        weights = [w.astype(BF16) for w in (w_out[layer], w_gate[layer], w_up[layer], w_down[layer])]
        wc = w_conv[layer]

        qt, kt, vt, conv_o, u_last = _inproj_prompt(yp, g1, w_cc, w_qkv.T, wc, seq)
        attn_o = _attn_prompt(slopes, qt, kt, vt)
        y_layer = _ffn(yp, attn_o, conv_o, weights[0], g2, *weights[1:], gf,
                       layer == depth - 1, _row_tile(seq))
        to_heads = lambda a: a.reshape(batch, N_HEADS, HEAD_DIM, seq).transpose(0, 3, 1, 2)
        outs[0].append(to_heads(kt))
        outs[1].append(to_heads(vt))
        outs[2].append(u_last[:, V7X_SUBLANES - (CONV_WIDTH - 1):, :])
        yp = y_layer

        st = state_conv[layer]
        prev1 = jnp.pad(st[:, 1:2], ((0, 0), (0, dec_seq - 1), (0, 0)))
        prev2 = jnp.pad(st, ((0, 0), (0, dec_seq - 2), (0, 0)))
        qs, ks, vs, conv_s, u_s = _inproj_sample(
            ys, g1, w_cc, w_qkv, wc, prev1.reshape(-1, d_conv), prev2.reshape(-1, d_conv), dec_seq)
        as3 = lambda a: a.reshape(dec_batch, dec_seq, D_ATTN)
        feature_major = lambda a: a.transpose(0, 2, 3, 1).reshape(dec_batch, D_ATTN, n_past)
        attn_s = _attn_sample(slopes, as3(qs), as3(ks), as3(vs),
                              feature_major(state_attn_k[layer]),
                              feature_major(state_attn_v[layer]))
        ys = _ffn(ys, attn_s.reshape(-1, D_ATTN), conv_s, weights[0], g2, *weights[1:], gf,
                  layer == depth - 1, ys.shape[0])
        outs[3].append(ks.reshape(dec_batch, dec_seq, N_HEADS, HEAD_DIM))
        outs[4].append(vs.reshape(dec_batch, dec_seq, N_HEADS, HEAD_DIM))
        outs[5].append(u_s.reshape(dec_batch, dec_seq, d_conv)[:, dec_seq - (CONV_WIDTH - 1):])

    y_prompt = yp.reshape(batch, seq, d_model)
    y_sample = ys.reshape(dec_batch, dec_seq, d_model)
    new_k_p, new_v_p, new_c_p, new_k_s, new_v_s, new_c_s = [jnp.stack(o) for o in outs]
    return (y_prompt, y_sample, new_k_p, new_v_p, new_c_p, new_k_s, new_v_s, new_c_s)
```

```python
import functools

import jax
import jax.numpy as jnp
import numpy as np
from jax import lax
from jax.experimental import pallas as pl
from jax.experimental.pallas import tpu as pltpu

F32 = jnp.float32
BF16 = jnp.bfloat16

HEAD_DIM = 64
N_HEADS = 8
D_ATTN = N_HEADS * HEAD_DIM
CONV_WIDTH = 3
WINDOWS = (128, 512, 2048)
DILATIONS = (1, 4, 16)
RMS_EPS = 1e-6
ATTN_SCALE = HEAD_DIM ** -0.5
LOG2_E = 1.4426950408889634
NEG_INF = -1e30

V7X_LANES = 128
V7X_SUBLANES = 8
V7X_VMEM_BYTES = 64 * 1024 * 1024

HEADS_PER_STEP = V7X_LANES // HEAD_DIM
TQ = 256
TK = 256
ATTN_STREAMS = 4
ATTN_LOOKAHEAD = 1
ROW_GROUPS = 2


def _rms(x, g):
    y = x * lax.rsqrt(jnp.mean(x * x, axis=-1, keepdims=True) + RMS_EPS)
    return y * g


def _branch_count(delta):
    nonneg = delta >= 0
    c = jnp.zeros(delta.shape, F32)
    for w, d in zip(WINDOWS, DILATIONS):
        hit = nonneg & ((delta & (d - 1)) == 0) & (delta <= w)
        c = c + hit.astype(F32)
    return c


def _dot_nt(a, b):
    return lax.dot_general(a, b, (((1,), (1,)), ((), ())), preferred_element_type=F32)


def _project(x, g, wcc_ref):
    d_conv = wcc_ref.shape[1] // 3
    xn = _rms(x, g).astype(BF16)
    zc = jnp.dot(xn, wcc_ref[...], preferred_element_type=F32)
    return xn, zc[:, 0:d_conv], zc[:, d_conv:2 * d_conv], zc[:, 2 * d_conv:3 * d_conv]


def _conv3(wc, u2, u1, u0):
    acc = wc[0:1, :] * u2
    acc = acc + wc[1:2, :] * u1
    return acc + wc[2:3, :] * u0


def _inproj_prompt_kernel(x_ref, g_ref, wcc_ref, wqkvt_ref, wc_ref, qt_ref, kt_ref, vt_ref, co_ref,
                          ulast_ref, uext_ref):
    tm = x_ref.shape[0]
    j = pl.program_id(1)
    tiles_per_seq = pl.num_programs(1)

    @pl.when(j == 0)
    def _():
        uext_ref[0:V7X_SUBLANES, :] = jnp.zeros((V7X_SUBLANES, uext_ref.shape[1]), F32)

    rg = tm // ROW_GROUPS
    for r0 in range(0, tm, rg):
        xn, hc, gb, gc = _project(x_ref[r0:r0 + rg, :], g_ref[...], wcc_ref)
        qkvt = _dot_nt(wqkvt_ref[...], xn)
        qt_ref[0, :, r0:r0 + rg] = (qkvt[0:D_ATTN, :] * (ATTN_SCALE * LOG2_E)).astype(qt_ref.dtype)
        kt_ref[0, :, r0:r0 + rg] = qkvt[D_ATTN:2 * D_ATTN, :]
        vt_ref[0, :, r0:r0 + rg] = qkvt[2 * D_ATTN:3 * D_ATTN, :]
        u = gc * hc
        uext_ref[V7X_SUBLANES:V7X_SUBLANES + rg, :] = u
        u1 = uext_ref[V7X_SUBLANES - 1:V7X_SUBLANES - 1 + rg, :]
        u2 = uext_ref[V7X_SUBLANES - 2:V7X_SUBLANES - 2 + rg, :]
        co_ref[r0:r0 + rg, :] = (gb * _conv3(wc_ref[...], u2, u1, u)).astype(co_ref.dtype)
        tail = u[rg - V7X_SUBLANES:rg, :]
        uext_ref[0:V7X_SUBLANES, :] = tail

    @pl.when(j == tiles_per_seq - 1)
    def _():
        ulast_ref[0] = tail


def _inproj_sample_kernel(x_ref, g_ref, wcc_ref, wqkv_ref, wc_ref, p1_ref, p2_ref, q_ref, k_ref,
                          v_ref, co_ref, u_ref, uext_ref, *, dec_seq):
    rows = x_ref.shape[0]
    xn, hc, gb, gc = _project(x_ref[...], g_ref[...], wcc_ref)
    qkv = jnp.dot(xn, wqkv_ref[...], preferred_element_type=F32)
    q_ref[...] = qkv[:, 0:D_ATTN] * ATTN_SCALE
    k_ref[...] = qkv[:, D_ATTN:2 * D_ATTN]
    v_ref[...] = qkv[:, 2 * D_ATTN:3 * D_ATTN]
    u = gc * hc
    u_ref[...] = u
    uext_ref[0:V7X_SUBLANES, :] = jnp.zeros((V7X_SUBLANES, u.shape[1]), F32)
    uext_ref[V7X_SUBLANES:V7X_SUBLANES + rows, :] = u
    t = lax.broadcasted_iota(jnp.int32, u.shape, 0) % dec_seq
    u1 = jnp.where(t >= 1, uext_ref[V7X_SUBLANES - 1:V7X_SUBLANES - 1 + rows, :], p1_ref[...])
    u2 = jnp.where(t >= 2, uext_ref[V7X_SUBLANES - 2:V7X_SUBLANES - 2 + rows, :], p2_ref[...])
    co_ref[...] = (gb * _conv3(wc_ref[...], u2, u1, u)).astype(co_ref.dtype)


def _attn_tables(seq):
    x = np.arange(seq, dtype=np.int32)[:, None]
    i = np.arange(TQ, dtype=np.int32)[None, :]
    delta = i + (seq - TQ) - x
    c = np.zeros(delta.shape, np.float32)
    for w, d in zip(WINDOWS, DILATIONS):
        c += (delta >= 0) & (delta % d == 0) & (delta <= w)
    logc = np.where(c > 0.0, np.log2(np.maximum(c, 1.0)), NEG_INF).astype(np.float32)
    return delta.astype(np.float32), logc


def _attn_prompt_kernel(slopes_ref, dist_ref, logc_ref, qt_ref, kt_ref, vt_ref, o_ref,
                        kb_ref, vtb_ref, bm_ref):
    seq = qt_ref.shape[2]
    pair = pl.program_id(0)
    feature = lax.broadcasted_iota(jnp.int32, (V7X_LANES, 1), 0)
    first_head = feature < HEAD_DIM

    @pl.when(pl.program_id(1) == 0)
    def _():
        for h in range(HEADS_PER_STEP):
            slope = slopes_ref[HEADS_PER_STEP * pair + h]
            bm_ref[h] = logc_ref[...] - (slope * LOG2_E) * dist_ref[...]

    kb_ref[...] = kt_ref[0].T.astype(BF16)
    vtb_ref[...] = vt_ref[0].astype(BF16)

    def score_half(st, chunk):
        k0, n = chunk
        h, row0 = st["h"], seq - (st["q0"] + TQ)
        s = (jnp.dot(kb_ref[k0:k0 + n, :], st["q"], preferred_element_type=F32)
             + bm_ref[h, row0 + k0:row0 + k0 + n, :])
        c8 = jnp.max(s.reshape(n // V7X_SUBLANES, V7X_SUBLANES, TQ), axis=0)
        m_new = jnp.maximum(st["m"], jnp.max(c8, axis=0, keepdims=True))
        pending = dict(k0=k0, n=n, s=s, m=m_new, alpha=jnp.exp2(st["m"] - m_new))
        st["m"] = m_new
        return pending

    def value_half(st, pending):
        h, k0, n = st["h"], pending["k0"], pending["n"]
        p = jnp.exp2(pending["s"] - pending["m"])
        st["l8"] = (pending["alpha"] * st["l8"]
                    + jnp.sum(p.reshape(n // V7X_SUBLANES, V7X_SUBLANES, TQ), axis=0))
        vt_h = vtb_ref[h * HEAD_DIM:(h + 1) * HEAD_DIM, k0:k0 + n]
        st["acc"] = pending["alpha"] * st["acc"] + jnp.dot(vt_h, p.astype(BF16),
                                                           preferred_element_type=F32)

    blocks_per_group = ATTN_STREAMS // HEADS_PER_STEP
    for g0 in range(0, seq, TQ * blocks_per_group):
        streams = []
        for q0 in range(g0, g0 + TQ * blocks_per_group, TQ):
            qt = qt_ref[0, :, q0:q0 + TQ]
            zero = jnp.zeros_like(qt)
            for h, qh in enumerate([jnp.where(first_head, qt, zero), jnp.where(first_head, zero, qt)]):
                chunks = [(q0, TQ)] + [(max(k1 - TK, 0), min(TK, k1)) for k1 in range(q0, 0, -TK)]
                streams.append(dict(q0=q0, h=h, q=qh, chunks=chunks,
                                    m=jnp.full((1, TQ), NEG_INF, F32),
                                    l8=jnp.zeros((V7X_SUBLANES, TQ), F32),
                                    acc=jnp.zeros((HEAD_DIM, TQ), F32)))
        waiting = []
        for t in range(max(len(st["chunks"]) for st in streams)):
            live = [(st, st["chunks"][t]) for st in streams if t < len(st["chunks"])]
            due = waiting.pop(0) if len(waiting) >= ATTN_LOOKAHEAD else []
            issued = []
            for i, (st, k0) in enumerate(live):
                issued.append((st, score_half(st, k0)))
                if i < len(due):
                    value_half(*due[i])
            for st, pending in due[len(live):]:
                value_half(st, pending)
            waiting.append(issued)
        for step in waiting:
            for st, pending in step:
                value_half(st, pending)
        for i in range(0, len(streams), HEADS_PER_STEP):
            outs = [st["acc"] / jnp.sum(st["l8"], axis=0, keepdims=True)
                    for st in streams[i:i + HEADS_PER_STEP]]
            q0 = streams[i]["q0"]
            o_ref[q0:q0 + TQ, :] = jnp.concatenate(outs, axis=0).T.astype(o_ref.dtype)


def _attn_sample_kernel(slopes_ref, q_ref, kn_ref, vn_ref, kst_ref, vst_ref, o_ref,
                        knew_ref, vnew_ref):
    dec_seq = q_ref.shape[1]
    n_past = kst_ref.shape[2]
    width = q_ref.shape[2]
    n_rows = dec_seq * N_HEADS

    q = q_ref[0]
    head_of_lane = lax.broadcasted_iota(jnp.int32, (N_HEADS, width), 1) // HEAD_DIM
    head_of_row = lax.broadcasted_iota(jnp.int32, (N_HEADS, width), 0)
    head_mask = head_of_lane == head_of_row
    wt = jnp.concatenate(
        [jnp.where(head_mask, jnp.broadcast_to(q[t:t + 1, :], (N_HEADS, width)), 0.0)
         for t in range(dec_seq)], axis=0).astype(BF16)

    knew_ref[...] = jnp.zeros(knew_ref.shape, F32)
    vnew_ref[...] = jnp.zeros(vnew_ref.shape, F32)
    knew_ref[0:dec_seq, :] = kn_ref[0]
    vnew_ref[0:dec_seq, :] = vn_ref[0]

    s_past = jnp.dot(wt, kst_ref[0].astype(BF16), preferred_element_type=F32)
    s_new = _dot_nt(wt, knew_ref[...].astype(BF16))

    row = lax.broadcasted_iota(jnp.int32, (n_rows, 1), 0)
    step = row // N_HEADS
    slope_col = jnp.zeros((n_rows, 1), F32)
    for h in range(N_HEADS):
        slope_col = jnp.where(row % N_HEADS == h, slopes_ref[h], slope_col)

    def weigh(s, key_pos):
        delta = n_past + step - key_pos
        c = _branch_count(delta)
        s = jnp.where(c > 0.0, s - slope_col * delta.astype(F32), NEG_INF)
        return s, c

    s_past, c_past = weigh(s_past, lax.broadcasted_iota(jnp.int32, (1, n_past), 1))
    s_new, c_new = weigh(s_new, n_past + lax.broadcasted_iota(jnp.int32, (1, V7X_LANES), 1))
    m = jnp.maximum(jnp.max(s_past, axis=-1, keepdims=True),
                    jnp.max(s_new, axis=-1, keepdims=True))
    p_past = c_past * jnp.exp(s_past - m)
    p_new = c_new * jnp.exp(s_new - m)
    l = jnp.sum(p_past, axis=-1, keepdims=True) + jnp.sum(p_new, axis=-1, keepdims=True)
    o_all = (_dot_nt(p_past.astype(BF16), vst_ref[0].astype(BF16))
             + jnp.dot(p_new.astype(BF16), vnew_ref[...].astype(BF16),
                       preferred_element_type=F32)) / l
    outs = []
    for t in range(dec_seq):
        blk = o_all[t * N_HEADS:(t + 1) * N_HEADS, :]
        outs.append(jnp.sum(jnp.where(head_mask, blk, 0.0), axis=0, keepdims=True))
    o_ref[0] = jnp.concatenate(outs, axis=0).astype(o_ref.dtype)


def _ffn_kernel(x_ref, a_ref, c_ref, wo_ref, g2_ref, wg_ref, wu_ref, wd_ref, gf_ref, y_ref, *,
                final_norm):
    tm = x_ref.shape[0]
    rg = tm // ROW_GROUPS if tm % (ROW_GROUPS * V7X_LANES) == 0 else tm
    groups = [slice(r0, r0 + rg) for r0 in range(0, tm, rg)]
    hs = []
    for rows in groups:
        mix = jnp.concatenate([a_ref[rows, :].astype(BF16), c_ref[rows, :].astype(BF16)], axis=-1)
        hs.append(x_ref[rows, :] + jnp.dot(mix, wo_ref[...], preferred_element_type=F32))
    acts = []
    for h in hs:
        hn = _rms(h, g2_ref[...]).astype(BF16)
        gate = jnp.dot(hn, wg_ref[...], preferred_element_type=F32)
        up = jnp.dot(hn, wu_ref[...], preferred_element_type=F32)
        acts.append(((gate * (1.0 / (1.0 + jnp.exp(-gate)))) * up).astype(BF16))
    for rows, h, act in zip(groups, hs, acts):
        y = h + jnp.dot(act, wd_ref[...], preferred_element_type=F32)
        y_ref[rows, :] = _rms(y, gf_ref[...]) if final_norm else y


def _nbytes(shape, dtype):
    n = 1
    for s in shape:
        n *= s
    return n * jnp.dtype(dtype).itemsize


def _vmem_limit(pipelined, resident, temporaries):
    need = 2 * sum(pipelined) + sum(resident) + sum(temporaries)
    return min(V7X_VMEM_BYTES, need + need // 4)


def _resident(shape):
    zeros = (0,) * len(shape)
    return pl.BlockSpec(shape, lambda *_: zeros, pipeline_mode=pl.Buffered(1))


def _row_tile(rows, largest=512):
    for tm in (largest, 512, 256, 128):
        if rows % tm == 0:
            return tm
    raise ValueError(f"row count {rows} is not a multiple of 128")


def _inproj_prompt(x, g, w_cc, w_qkvt, wc, seq):
    rows, d_model = x.shape
    d_conv = wc.shape[1]
    tm = _row_tile(seq, largest=ROW_GROUPS * 512)
    tiles_per_seq = seq // tm
    n_seq = rows // seq
    row_blk = lambda width: pl.BlockSpec((tm, width), lambda b, j: (b * tiles_per_seq + j, 0))
    col_blk = pl.BlockSpec((1, D_ATTN, tm), lambda b, j: (b, 0, j))
    limit = _vmem_limit(
        [_nbytes((tm, d_model), F32), _nbytes((D_ATTN, tm), BF16), 2 * _nbytes((D_ATTN, tm), F32),
         _nbytes((tm, d_conv), BF16)],
        [_nbytes(w_cc.shape, BF16), _nbytes(w_qkvt.shape, BF16)],
        [2 * _nbytes((tm, d_model), F32), 2 * _nbytes((tm, w_cc.shape[1]), F32),
         2 * _nbytes((w_qkvt.shape[0], tm), F32), 4 * _nbytes((tm, d_conv), F32)])
    return pl.pallas_call(
        _inproj_prompt_kernel,
        grid=(n_seq, tiles_per_seq),
        in_specs=[row_blk(d_model), _resident((1, d_model)), _resident(w_cc.shape),
                  _resident(w_qkvt.shape), _resident(wc.shape)],
        out_specs=[col_blk, col_blk, col_blk, row_blk(d_conv),
                   pl.BlockSpec((1, V7X_SUBLANES, d_conv), lambda b, j: (b, 0, 0))],
        out_shape=[jax.ShapeDtypeStruct((n_seq, D_ATTN, seq), BF16),
                   jax.ShapeDtypeStruct((n_seq, D_ATTN, seq), F32),
                   jax.ShapeDtypeStruct((n_seq, D_ATTN, seq), F32),
                   jax.ShapeDtypeStruct((rows, d_conv), BF16),
                   jax.ShapeDtypeStruct((n_seq, V7X_SUBLANES, d_conv), F32)],
        scratch_shapes=[pltpu.VMEM((tm + V7X_SUBLANES, d_conv), F32)],
        compiler_params=pltpu.CompilerParams(dimension_semantics=("arbitrary", "arbitrary"),
                                             vmem_limit_bytes=limit),
        name="inproj_prompt",
    )(x, g, w_cc, w_qkvt, wc)


def _inproj_sample(x, g, w_cc, w_qkv, wc, prev1, prev2, dec_seq):
    rows, d_model = x.shape
    d_conv = wc.shape[1]
    full = lambda shape: pl.BlockSpec(shape, lambda i: (0,) * len(shape))
    return pl.pallas_call(
        functools.partial(_inproj_sample_kernel, dec_seq=dec_seq),
        grid=(1,),
        in_specs=[full(x.shape), full((1, d_model)), full(w_cc.shape), full(w_qkv.shape),
                  full(wc.shape), full(prev1.shape), full(prev2.shape)],
        out_specs=[full((rows, D_ATTN))] * 3 + [full((rows, d_conv))] * 2,
        out_shape=[jax.ShapeDtypeStruct((rows, D_ATTN), F32)] * 3
        + [jax.ShapeDtypeStruct((rows, d_conv), BF16), jax.ShapeDtypeStruct((rows, d_conv), F32)],
        scratch_shapes=[pltpu.VMEM((rows + V7X_SUBLANES, d_conv), F32)],
        compiler_params=pltpu.CompilerParams(dimension_semantics=("arbitrary",)),
        name="inproj_sample",
    )(x, g, w_cc, w_qkv, wc, prev1, prev2)


def _attn_prompt(slopes, qt, kt, vt):
    n_seq, _, seq = qt.shape
    rows = n_seq * seq
    o_blk = pl.BlockSpec((seq, V7X_LANES), lambda p, b: (b, p))
    kv_blk = pl.BlockSpec((1, V7X_LANES, seq), lambda p, b: (b, p, 0))
    dist, logc = _attn_tables(seq)
    table = _nbytes((TQ, seq), F32)
    limit = _vmem_limit(
        [2 * _nbytes((seq, V7X_LANES), F32), 2 * _nbytes((seq, V7X_LANES), BF16)],
        [2 * _nbytes((seq, V7X_LANES), BF16), (2 + HEADS_PER_STEP) * table],
        [3 * HEADS_PER_STEP * table])
    return pl.pallas_call(
        _attn_prompt_kernel,
        grid=(N_HEADS // HEADS_PER_STEP, n_seq),
        in_specs=[pl.BlockSpec(memory_space=pltpu.SMEM), _resident((seq, TQ)),
                  _resident((seq, TQ)), kv_blk, kv_blk, kv_blk],
        out_specs=o_blk,
        out_shape=jax.ShapeDtypeStruct((rows, D_ATTN), BF16),
        scratch_shapes=[pltpu.VMEM((seq, V7X_LANES), BF16), pltpu.VMEM((V7X_LANES, seq), BF16),
                        pltpu.VMEM((HEADS_PER_STEP, seq, TQ), F32)],
        compiler_params=pltpu.CompilerParams(dimension_semantics=("arbitrary", "arbitrary"),
                                             vmem_limit_bytes=limit),
        name="attn_prompt",
    )(slopes, dist, logc, qt, kt, vt)


def _attn_sample(slopes, q, kn, vn, kst, vst):
    n_seq, dec_seq, width = q.shape
    n_past = kst.shape[2]
    new_blk = pl.BlockSpec((1, dec_seq, width), lambda b: (b, 0, 0))
    past_blk = pl.BlockSpec((1, width, n_past), lambda b: (b, 0, 0))
    limit = _vmem_limit(
        [2 * _nbytes((n_past, width), F32)], [],
        [2 * _nbytes((n_past, width), BF16), 8 * _nbytes((dec_seq * N_HEADS, n_past), F32)])
    return pl.pallas_call(
        _attn_sample_kernel,
        grid=(n_seq,),
        in_specs=[pl.BlockSpec(memory_space=pltpu.SMEM), new_blk, new_blk, new_blk,
                  past_blk, past_blk],
        out_specs=new_blk,
        out_shape=jax.ShapeDtypeStruct((n_seq, dec_seq, width), F32),
        scratch_shapes=[pltpu.VMEM((V7X_LANES, width), F32)] * 2,
        compiler_params=pltpu.CompilerParams(dimension_semantics=("arbitrary",),
                                             vmem_limit_bytes=limit),
        name="attn_sample",
    )(slopes, q, kn, vn, kst, vst)


def _ffn(x, attn_o, conv_o, wo, g2, wg, wu, wd, gf, final_norm, tm):
    rows, d_model = x.shape
    d_ff = wg.shape[1]
    row_blk = lambda width: pl.BlockSpec((tm, width), lambda i: (i, 0))
    limit = _vmem_limit(
        [2 * _nbytes((tm, d_model), F32), _nbytes((tm, attn_o.shape[1]), attn_o.dtype),
         _nbytes((tm, conv_o.shape[1]), conv_o.dtype)],
        [_nbytes(w.shape, BF16) for w in (wo, wg, wu, wd)],
        [4 * _nbytes((tm, d_model), F32), 3 * _nbytes((tm, d_ff), F32)])
    return pl.pallas_call(
        functools.partial(_ffn_kernel, final_norm=final_norm),
        grid=(rows // tm,),
        in_specs=[row_blk(d_model), row_blk(attn_o.shape[1]), row_blk(conv_o.shape[1]),
                  _resident(wo.shape), _resident((1, d_model)), _resident(wg.shape),
                  _resident(wu.shape), _resident(wd.shape), _resident((1, d_model))],
        out_specs=row_blk(d_model),
        out_shape=jax.ShapeDtypeStruct((rows, d_model), F32),
        compiler_params=pltpu.CompilerParams(dimension_semantics=("arbitrary",),
                                             vmem_limit_bytes=limit),
        name="ffn",
    )(x, attn_o, conv_o, wo, g2, wg, wu, wd, gf)


def _alibi_slopes():
    return jnp.exp2(-8.0 * jnp.arange(1, N_HEADS + 1, dtype=F32) / N_HEADS)


def kernel(x_prompt, x_sample, state_attn_k, state_attn_v, state_conv, norm_mix_g, w_in, w_conv,
           w_out, norm_ffn_g, w_gate, w_up, w_down, norm_final_g):
    depth = w_in.shape[0]
    batch, seq, d_model = x_prompt.shape
    dec_batch, dec_seq, _ = x_sample.shape
    d_conv = w_conv.shape[2]
    n_past = state_attn_k.shape[2]
    assert n_past >= max(WINDOWS) and seq <= max(WINDOWS) and seq % TQ == 0
    assert CONV_WIDTH - 1 <= dec_seq <= V7X_LANES
    slopes = _alibi_slopes()
    gf = norm_final_g.reshape(1, d_model)

    yp = x_prompt.reshape(batch * seq, d_model)
    ys = x_sample.reshape(dec_batch * dec_seq, d_model)
    outs = [[] for _ in range(6)]
    for layer in range(depth):
        g1 = norm_mix_g[layer].reshape(1, d_model)
        g2 = norm_ffn_g[layer].reshape(1, d_model)
        w_in_bf = w_in[layer].astype(BF16)
        w_qkv = w_in_bf[:, :3 * D_ATTN]
        w_cc = w_in_bf[:, 3 * D_ATTN:]
        weights = [w.astype(BF16) for w in (w_out[layer], w_gate[layer], w_up[layer], w_down[layer])]
        wc = w_conv[layer]

        qt, kt, vt, conv_o, u_last = _inproj_prompt(yp, g1, w_cc, w_qkv.T, wc, seq)
        attn_o = _attn_prompt(slopes, qt, kt, vt)
        y_layer = _ffn(yp, attn_o, conv_o, weights[0], g2, *weights[1:], gf,
                       layer == depth - 1, _row_tile(seq))
        to_heads = lambda a: a.reshape(batch, N_HEADS, HEAD_DIM, seq).transpose(0, 3, 1, 2)
        outs[0].append(to_heads(kt))
        outs[1].append(to_heads(vt))
        outs[2].append(u_last[:, V7X_SUBLANES - (CONV_WIDTH - 1):, :])
        yp = y_layer

        st = state_conv[layer]
        prev1 = jnp.pad(st[:, 1:2], ((0, 0), (0, dec_seq - 1), (0, 0)))
        prev2 = jnp.pad(st, ((0, 0), (0, dec_seq - 2), (0, 0)))
        qs, ks, vs, conv_s, u_s = _inproj_sample(
            ys, g1, w_cc, w_qkv, wc, prev1.reshape(-1, d_conv), prev2.reshape(-1, d_conv), dec_seq)
        as3 = lambda a: a.reshape(dec_batch, dec_seq, D_ATTN)
        feature_major = lambda a: a.transpose(0, 2, 3, 1).reshape(dec_batch, D_ATTN, n_past)
        attn_s = _attn_sample(slopes, as3(qs), as3(ks), as3(vs),
                              feature_major(state_attn_k[layer]),
                              feature_major(state_attn_v[layer]))
        ys = _ffn(ys, attn_s.reshape(-1, D_ATTN), conv_s, weights[0], g2, *weights[1:], gf,
                  layer == depth - 1, ys.shape[0])
        outs[3].append(ks.reshape(dec_batch, dec_seq, N_HEADS, HEAD_DIM))
        outs[4].append(vs.reshape(dec_batch, dec_seq, N_HEADS, HEAD_DIM))
        outs[5].append(u_s.reshape(dec_batch, dec_seq, d_conv)[:, dec_seq - (CONV_WIDTH - 1):])

    y_prompt = yp.reshape(batch, seq, d_model)
    y_sample = ys.reshape(dec_batch, dec_seq, d_model)
    new_k_p, new_v_p, new_c_p, new_k_s, new_v_s, new_c_s = [jnp.stack(o) for o in outs]
    return (y_prompt, y_sample, new_k_p, new_v_p, new_c_p, new_k_s, new_v_s, new_c_s)
```

```python
import functools

import jax
import jax.numpy as jnp
import numpy as np
from jax import lax
from jax.experimental import pallas as pl
from jax.experimental.pallas import tpu as pltpu

F32 = jnp.float32
BF16 = jnp.bfloat16

HEAD_DIM = 64
N_HEADS = 8
D_ATTN = N_HEADS * HEAD_DIM
CONV_WIDTH = 3
WINDOWS = (128, 512, 2048)
DILATIONS = (1, 4, 16)
RMS_EPS = 1e-6
ATTN_SCALE = HEAD_DIM ** -0.5
LOG2_E = 1.4426950408889634
NEG_INF = -1e30

V7X_LANES = 128
V7X_SUBLANES = 8
V7X_VMEM_BYTES = 64 * 1024 * 1024

HEADS_PER_STEP = V7X_LANES // HEAD_DIM
TQ = 256
TK = 256
ATTN_STREAMS = 4
ATTN_LOOKAHEAD = 1
ROW_GROUPS = 4
FFN_ROWS = 1024
INPROJ_ROW_GROUPS = 8
INPROJ_ROWS = 2048


def _rms(x, g):
    y = x * lax.rsqrt(jnp.mean(x * x, axis=-1, keepdims=True) + RMS_EPS)
    return y * g


def _branch_count(delta):
    nonneg = delta >= 0
    c = jnp.zeros(delta.shape, F32)
    for w, d in zip(WINDOWS, DILATIONS):
        hit = nonneg & ((delta & (d - 1)) == 0) & (delta <= w)
        c = c + hit.astype(F32)
    return c


def _dot_nt(a, b):
    return lax.dot_general(a, b, (((1,), (1,)), ((), ())), preferred_element_type=F32)


def _project(x, g, wcc_ref):
    d_conv = wcc_ref.shape[1] // 3
    xn = _rms(x, g).astype(BF16)
    zc = jnp.dot(xn, wcc_ref[...], preferred_element_type=F32)
    return xn, zc[:, 0:d_conv], zc[:, d_conv:2 * d_conv], zc[:, 2 * d_conv:3 * d_conv]


def _conv3(wc, u2, u1, u0):
    acc = wc[0:1, :] * u2
    acc = acc + wc[1:2, :] * u1
    return acc + wc[2:3, :] * u0


def _inproj_prompt_kernel(x_ref, g_ref, wcc_ref, wqkvt_ref, wc_ref, qt_ref, kt_ref, vt_ref, co_ref,
                          ulast_ref, uext_ref):
    tm = x_ref.shape[0]
    j = pl.program_id(1)
    tiles_per_seq = pl.num_programs(1)

    @pl.when(j == 0)
    def _():
        uext_ref[0:V7X_SUBLANES, :] = jnp.zeros((V7X_SUBLANES, uext_ref.shape[1]), F32)

    rg = tm // INPROJ_ROW_GROUPS
    for r0 in range(0, tm, rg):
        xn, hc, gb, gc = _project(x_ref[r0:r0 + rg, :], g_ref[...], wcc_ref)
        qkvt = _dot_nt(wqkvt_ref[...], xn)
        qt_ref[0, :, r0:r0 + rg] = (qkvt[0:D_ATTN, :] * (ATTN_SCALE * LOG2_E)).astype(qt_ref.dtype)
        kt_ref[0, :, r0:r0 + rg] = qkvt[D_ATTN:2 * D_ATTN, :]
        vt_ref[0, :, r0:r0 + rg] = qkvt[2 * D_ATTN:3 * D_ATTN, :]
        u = gc * hc
        uext_ref[V7X_SUBLANES:V7X_SUBLANES + rg, :] = u
        u1 = uext_ref[V7X_SUBLANES - 1:V7X_SUBLANES - 1 + rg, :]
        u2 = uext_ref[V7X_SUBLANES - 2:V7X_SUBLANES - 2 + rg, :]
        co_ref[r0:r0 + rg, :] = (gb * _conv3(wc_ref[...], u2, u1, u)).astype(co_ref.dtype)
        tail = u[rg - V7X_SUBLANES:rg, :]
        uext_ref[0:V7X_SUBLANES, :] = tail

    @pl.when(j == tiles_per_seq - 1)
    def _():
        ulast_ref[0] = tail


def _inproj_sample_kernel(x_ref, g_ref, wcc_ref, wqkv_ref, wc_ref, p1_ref, p2_ref, q_ref, k_ref,
                          v_ref, co_ref, u_ref, uext_ref, *, dec_seq):
    rows = x_ref.shape[0]
    xn, hc, gb, gc = _project(x_ref[...], g_ref[...], wcc_ref)
    qkv = jnp.dot(xn, wqkv_ref[...], preferred_element_type=F32)
    q_ref[...] = qkv[:, 0:D_ATTN] * ATTN_SCALE
    k_ref[...] = qkv[:, D_ATTN:2 * D_ATTN]
    v_ref[...] = qkv[:, 2 * D_ATTN:3 * D_ATTN]
    u = gc * hc
    u_ref[...] = u
    uext_ref[0:V7X_SUBLANES, :] = jnp.zeros((V7X_SUBLANES, u.shape[1]), F32)
    uext_ref[V7X_SUBLANES:V7X_SUBLANES + rows, :] = u
    t = lax.broadcasted_iota(jnp.int32, u.shape, 0) % dec_seq
    u1 = jnp.where(t >= 1, uext_ref[V7X_SUBLANES - 1:V7X_SUBLANES - 1 + rows, :], p1_ref[...])
    u2 = jnp.where(t >= 2, uext_ref[V7X_SUBLANES - 2:V7X_SUBLANES - 2 + rows, :], p2_ref[...])
    co_ref[...] = (gb * _conv3(wc_ref[...], u2, u1, u)).astype(co_ref.dtype)


def _attn_tables(seq):
    x = np.arange(seq, dtype=np.int32)[:, None]
    i = np.arange(TQ, dtype=np.int32)[None, :]
    delta = i + (seq - TQ) - x
    c = np.zeros(delta.shape, np.float32)
    for w, d in zip(WINDOWS, DILATIONS):
        c += (delta >= 0) & (delta % d == 0) & (delta <= w)
    logc = np.where(c > 0.0, np.log2(np.maximum(c, 1.0)), NEG_INF).astype(np.float32)
    return delta.astype(np.float32), logc


def _attn_prompt_kernel(slopes_ref, dist_ref, logc_ref, qt_ref, kt_ref, vt_ref, o_ref,
                        kb_ref, vtb_ref, bm_ref):
    seq = qt_ref.shape[2]
    pair = pl.program_id(0)
    feature = lax.broadcasted_iota(jnp.int32, (V7X_LANES, 1), 0)
    first_head = feature < HEAD_DIM

    @pl.when(pl.program_id(1) == 0)
    def _():
        for h in range(HEADS_PER_STEP):
            slope = slopes_ref[HEADS_PER_STEP * pair + h]
            bm_ref[h] = logc_ref[...] - (slope * LOG2_E) * dist_ref[...]

    kb_ref[...] = kt_ref[0].T.astype(BF16)
    vtb_ref[...] = vt_ref[0].astype(BF16)

    def score_half(st, chunk):
        k0, n = chunk
        h, row0 = st["h"], seq - (st["q0"] + TQ)
        s = (jnp.dot(kb_ref[k0:k0 + n, :], st["q"], preferred_element_type=F32)
             + bm_ref[h, row0 + k0:row0 + k0 + n, :])
        c8 = jnp.max(s.reshape(n // V7X_SUBLANES, V7X_SUBLANES, TQ), axis=0)
        m_new = jnp.maximum(st["m"], jnp.max(c8, axis=0, keepdims=True))
        pending = dict(k0=k0, n=n, s=s, m=m_new, alpha=jnp.exp2(st["m"] - m_new))
        st["m"] = m_new
        return pending

    def value_half(st, pending):
        h, k0, n = st["h"], pending["k0"], pending["n"]
        p = jnp.exp2(pending["s"] - pending["m"])
        st["l8"] = (pending["alpha"] * st["l8"]
                    + jnp.sum(p.reshape(n // V7X_SUBLANES, V7X_SUBLANES, TQ), axis=0))
        vt_h = vtb_ref[h * HEAD_DIM:(h + 1) * HEAD_DIM, k0:k0 + n]
        st["acc"] = pending["alpha"] * st["acc"] + jnp.dot(vt_h, p.astype(BF16),
                                                           preferred_element_type=F32)

    blocks_per_group = ATTN_STREAMS // HEADS_PER_STEP
    for g0 in range(0, seq, TQ * blocks_per_group):
        streams = []
        for q0 in range(g0, g0 + TQ * blocks_per_group, TQ):
            qt = qt_ref[0, :, q0:q0 + TQ]
            zero = jnp.zeros_like(qt)
            for h, qh in enumerate([jnp.where(first_head, qt, zero), jnp.where(first_head, zero, qt)]):
                chunks = [(q0, TQ)] + [(max(k1 - TK, 0), min(TK, k1)) for k1 in range(q0, 0, -TK)]
                streams.append(dict(q0=q0, h=h, q=qh, chunks=chunks,
                                    m=jnp.full((1, TQ), NEG_INF, F32),
                                    l8=jnp.zeros((V7X_SUBLANES, TQ), F32),
                                    acc=jnp.zeros((HEAD_DIM, TQ), F32)))
        waiting = []
        for t in range(max(len(st["chunks"]) for st in streams)):
            live = [(st, st["chunks"][t]) for st in streams if t < len(st["chunks"])]
            due = waiting.pop(0) if len(waiting) >= ATTN_LOOKAHEAD else []
            issued = []
            for i, (st, k0) in enumerate(live):
                issued.append((st, score_half(st, k0)))
                if i < len(due):
                    value_half(*due[i])
            for st, pending in due[len(live):]:
                value_half(st, pending)
            waiting.append(issued)
        for step in waiting:
            for st, pending in step:
                value_half(st, pending)
        for i in range(0, len(streams), HEADS_PER_STEP):
            outs = [st["acc"] / jnp.sum(st["l8"], axis=0, keepdims=True)
                    for st in streams[i:i + HEADS_PER_STEP]]
            q0 = streams[i]["q0"]
            o_ref[q0:q0 + TQ, :] = jnp.concatenate(outs, axis=0).T.astype(o_ref.dtype)


def _attn_sample_kernel(slopes_ref, q_ref, kn_ref, vn_ref, kst_ref, vst_ref, o_ref,
                        knew_ref, vnew_ref):
    dec_seq = q_ref.shape[1]
    n_past = kst_ref.shape[2]
    width = q_ref.shape[2]
    n_rows = dec_seq * N_HEADS

    q = q_ref[0]
    head_of_lane = lax.broadcasted_iota(jnp.int32, (N_HEADS, width), 1) // HEAD_DIM
    head_of_row = lax.broadcasted_iota(jnp.int32, (N_HEADS, width), 0)
    head_mask = head_of_lane == head_of_row
    wt = jnp.concatenate(
        [jnp.where(head_mask, jnp.broadcast_to(q[t:t + 1, :], (N_HEADS, width)), 0.0)
         for t in range(dec_seq)], axis=0).astype(BF16)

    knew_ref[...] = jnp.zeros(knew_ref.shape, F32)
    vnew_ref[...] = jnp.zeros(vnew_ref.shape, F32)
    knew_ref[0:dec_seq, :] = kn_ref[0]
    vnew_ref[0:dec_seq, :] = vn_ref[0]

    s_past = jnp.dot(wt, kst_ref[0].astype(BF16), preferred_element_type=F32)
    s_new = _dot_nt(wt, knew_ref[...].astype(BF16))

    row = lax.broadcasted_iota(jnp.int32, (n_rows, 1), 0)
    step = row // N_HEADS
    slope_col = jnp.zeros((n_rows, 1), F32)
    for h in range(N_HEADS):
        slope_col = jnp.where(row % N_HEADS == h, slopes_ref[h], slope_col)

    def weigh(s, key_pos):
        delta = n_past + step - key_pos
        c = _branch_count(delta)
        s = jnp.where(c > 0.0, s - slope_col * delta.astype(F32), NEG_INF)
        return s, c

    s_past, c_past = weigh(s_past, lax.broadcasted_iota(jnp.int32, (1, n_past), 1))
    s_new, c_new = weigh(s_new, n_past + lax.broadcasted_iota(jnp.int32, (1, V7X_LANES), 1))
    m = jnp.maximum(jnp.max(s_past, axis=-1, keepdims=True),
                    jnp.max(s_new, axis=-1, keepdims=True))
    p_past = c_past * jnp.exp(s_past - m)
    p_new = c_new * jnp.exp(s_new - m)
    l = jnp.sum(p_past, axis=-1, keepdims=True) + jnp.sum(p_new, axis=-1, keepdims=True)
    o_all = (_dot_nt(p_past.astype(BF16), vst_ref[0].astype(BF16))
             + jnp.dot(p_new.astype(BF16), vnew_ref[...].astype(BF16),
                       preferred_element_type=F32)) / l
    outs = []
    for t in range(dec_seq):
        blk = o_all[t * N_HEADS:(t + 1) * N_HEADS, :]
        outs.append(jnp.sum(jnp.where(head_mask, blk, 0.0), axis=0, keepdims=True))
    o_ref[0] = jnp.concatenate(outs, axis=0).astype(o_ref.dtype)


def _ffn_kernel(x_ref, a_ref, c_ref, wo_ref, g2_ref, wg_ref, wu_ref, wd_ref, gf_ref, y_ref, *,
                final_norm):
    tm = x_ref.shape[0]
    rg = tm // ROW_GROUPS if tm % (ROW_GROUPS * V7X_LANES) == 0 else tm
    groups = [slice(r0, r0 + rg) for r0 in range(0, tm, rg)]
    hs = []
    for rows in groups:
        mix = jnp.concatenate([a_ref[rows, :].astype(BF16), c_ref[rows, :].astype(BF16)], axis=-1)
        hs.append(x_ref[rows, :] + jnp.dot(mix, wo_ref[...], preferred_element_type=F32))
    acts = []
    for h in hs:
        hn = _rms(h, g2_ref[...]).astype(BF16)
        gate = jnp.dot(hn, wg_ref[...], preferred_element_type=F32)
        up = jnp.dot(hn, wu_ref[...], preferred_element_type=F32)
        acts.append(((gate * (1.0 / (1.0 + jnp.exp(-gate)))) * up).astype(BF16))
    for rows, h, act in zip(groups, hs, acts):
        y = h + jnp.dot(act, wd_ref[...], preferred_element_type=F32)
        y_ref[rows, :] = _rms(y, gf_ref[...]) if final_norm else y


def _nbytes(shape, dtype):
    n = 1
    for s in shape:
        n *= s
    return n * jnp.dtype(dtype).itemsize


def _vmem_limit(pipelined, resident, temporaries):
    need = 2 * sum(pipelined) + sum(resident) + sum(temporaries)
    return min(V7X_VMEM_BYTES, need + need // 4)


def _resident(shape):
    zeros = (0,) * len(shape)
    return pl.BlockSpec(shape, lambda *_: zeros, pipeline_mode=pl.Buffered(1))


def _row_tile(rows, largest=512):
    for tm in (largest, 512, 256, 128):
        if rows % tm == 0:
            return tm
    raise ValueError(f"row count {rows} is not a multiple of 128")


def _inproj_prompt(x, g, w_cc, w_qkvt, wc, seq):
    rows, d_model = x.shape
    d_conv = wc.shape[1]
    tm = _row_tile(seq, largest=INPROJ_ROWS)
    tiles_per_seq = seq // tm
    n_seq = rows // seq
    row_blk = lambda width: pl.BlockSpec((tm, width), lambda b, j: (b * tiles_per_seq + j, 0))
    col_blk = pl.BlockSpec((1, D_ATTN, tm), lambda b, j: (b, 0, j))
    limit = _vmem_limit(
        [_nbytes((tm, d_model), F32), _nbytes((D_ATTN, tm), BF16), 2 * _nbytes((D_ATTN, tm), F32),
         _nbytes((tm, d_conv), BF16)],
        [_nbytes(w_cc.shape, BF16), _nbytes(w_qkvt.shape, BF16)],
        [2 * _nbytes((tm, d_model), F32), 2 * _nbytes((tm, w_cc.shape[1]), F32),
         2 * _nbytes((w_qkvt.shape[0], tm), F32), 4 * _nbytes((tm, d_conv), F32)])
    return pl.pallas_call(
        _inproj_prompt_kernel,
        grid=(n_seq, tiles_per_seq),
        in_specs=[row_blk(d_model), _resident((1, d_model)), _resident(w_cc.shape),
                  _resident(w_qkvt.shape), _resident(wc.shape)],
        out_specs=[col_blk, col_blk, col_blk, row_blk(d_conv),
                   pl.BlockSpec((1, V7X_SUBLANES, d_conv), lambda b, j: (b, 0, 0))],
        out_shape=[jax.ShapeDtypeStruct((n_seq, D_ATTN, seq), BF16),
                   jax.ShapeDtypeStruct((n_seq, D_ATTN, seq), F32),
                   jax.ShapeDtypeStruct((n_seq, D_ATTN, seq), F32),
                   jax.ShapeDtypeStruct((rows, d_conv), BF16),
                   jax.ShapeDtypeStruct((n_seq, V7X_SUBLANES, d_conv), F32)],
        scratch_shapes=[pltpu.VMEM((tm + V7X_SUBLANES, d_conv), F32)],
        compiler_params=pltpu.CompilerParams(dimension_semantics=("arbitrary", "arbitrary"),
                                             vmem_limit_bytes=limit),
        name="inproj_prompt",
    )(x, g, w_cc, w_qkvt, wc)


def _inproj_sample(x, g, w_cc, w_qkv, wc, prev1, prev2, dec_seq):
    rows, d_model = x.shape
    d_conv = wc.shape[1]
    full = lambda shape: pl.BlockSpec(shape, lambda i: (0,) * len(shape))
    return pl.pallas_call(
        functools.partial(_inproj_sample_kernel, dec_seq=dec_seq),
        grid=(1,),
        in_specs=[full(x.shape), full((1, d_model)), full(w_cc.shape), full(w_qkv.shape),
                  full(wc.shape), full(prev1.shape), full(prev2.shape)],
        out_specs=[full((rows, D_ATTN))] * 3 + [full((rows, d_conv))] * 2,
        out_shape=[jax.ShapeDtypeStruct((rows, D_ATTN), F32)] * 3
        + [jax.ShapeDtypeStruct((rows, d_conv), BF16), jax.ShapeDtypeStruct((rows, d_conv), F32)],
        scratch_shapes=[pltpu.VMEM((rows + V7X_SUBLANES, d_conv), F32)],
        compiler_params=pltpu.CompilerParams(dimension_semantics=("arbitrary",)),
        name="inproj_sample",
    )(x, g, w_cc, w_qkv, wc, prev1, prev2)


def _attn_prompt(slopes, qt, kt, vt):
    n_seq, _, seq = qt.shape
    rows = n_seq * seq
    o_blk = pl.BlockSpec((seq, V7X_LANES), lambda p, b: (b, p))
    kv_blk = pl.BlockSpec((1, V7X_LANES, seq), lambda p, b: (b, p, 0))
    dist, logc = _attn_tables(seq)
    table = _nbytes((TQ, seq), F32)
    limit = _vmem_limit(
        [2 * _nbytes((seq, V7X_LANES), F32), 2 * _nbytes((seq, V7X_LANES), BF16)],
        [2 * _nbytes((seq, V7X_LANES), BF16), (2 + HEADS_PER_STEP) * table],
        [3 * HEADS_PER_STEP * table])
    return pl.pallas_call(
        _attn_prompt_kernel,
        grid=(N_HEADS // HEADS_PER_STEP, n_seq),
        in_specs=[pl.BlockSpec(memory_space=pltpu.SMEM), _resident((seq, TQ)),
                  _resident((seq, TQ)), kv_blk, kv_blk, kv_blk],
        out_specs=o_blk,
        out_shape=jax.ShapeDtypeStruct((rows, D_ATTN), BF16),
        scratch_shapes=[pltpu.VMEM((seq, V7X_LANES), BF16), pltpu.VMEM((V7X_LANES, seq), BF16),
                        pltpu.VMEM((HEADS_PER_STEP, seq, TQ), F32)],
        compiler_params=pltpu.CompilerParams(dimension_semantics=("arbitrary", "arbitrary"),
                                             vmem_limit_bytes=limit),
        name="attn_prompt",
    )(slopes, dist, logc, qt, kt, vt)


def _attn_sample(slopes, q, kn, vn, kst, vst):
    n_seq, dec_seq, width = q.shape
    n_past = kst.shape[2]
    new_blk = pl.BlockSpec((1, dec_seq, width), lambda b: (b, 0, 0))
    past_blk = pl.BlockSpec((1, width, n_past), lambda b: (b, 0, 0))
    limit = _vmem_limit(
        [2 * _nbytes((n_past, width), F32)], [],
        [2 * _nbytes((n_past, width), BF16), 8 * _nbytes((dec_seq * N_HEADS, n_past), F32)])
    return pl.pallas_call(
        _attn_sample_kernel,
        grid=(n_seq,),
        in_specs=[pl.BlockSpec(memory_space=pltpu.SMEM), new_blk, new_blk, new_blk,
                  past_blk, past_blk],
        out_specs=new_blk,
        out_shape=jax.ShapeDtypeStruct((n_seq, dec_seq, width), F32),
        scratch_shapes=[pltpu.VMEM((V7X_LANES, width), F32)] * 2,
        compiler_params=pltpu.CompilerParams(dimension_semantics=("arbitrary",),
                                             vmem_limit_bytes=limit),
        name="attn_sample",
    )(slopes, q, kn, vn, kst, vst)


def _ffn(x, attn_o, conv_o, wo, g2, wg, wu, wd, gf, final_norm, tm):
    rows, d_model = x.shape
    d_ff = wg.shape[1]
    row_blk = lambda width: pl.BlockSpec((tm, width), lambda i: (i, 0))
    limit = _vmem_limit(
        [2 * _nbytes((tm, d_model), F32), _nbytes((tm, attn_o.shape[1]), attn_o.dtype),
         _nbytes((tm, conv_o.shape[1]), conv_o.dtype)],
        [_nbytes(w.shape, BF16) for w in (wo, wg, wu, wd)],
        [4 * _nbytes((tm, d_model), F32), 3 * _nbytes((tm, d_ff), F32)])
    return pl.pallas_call(
        functools.partial(_ffn_kernel, final_norm=final_norm),
        grid=(rows // tm,),
        in_specs=[row_blk(d_model), row_blk(attn_o.shape[1]), row_blk(conv_o.shape[1]),
                  _resident(wo.shape), _resident((1, d_model)), _resident(wg.shape),
                  _resident(wu.shape), _resident(wd.shape), _resident((1, d_model))],
        out_specs=row_blk(d_model),
        out_shape=jax.ShapeDtypeStruct((rows, d_model), F32),
        compiler_params=pltpu.CompilerParams(dimension_semantics=("arbitrary",),
                                             vmem_limit_bytes=limit),
        name="ffn",
    )(x, attn_o, conv_o, wo, g2, wg, wu, wd, gf)


def _alibi_slopes():
    return jnp.exp2(-8.0 * jnp.arange(1, N_HEADS + 1, dtype=F32) / N_HEADS)


def kernel(x_prompt, x_sample, state_attn_k, state_attn_v, state_conv, norm_mix_g, w_in, w_conv,
           w_out, norm_ffn_g, w_gate, w_up, w_down, norm_final_g):
    depth = w_in.shape[0]
    batch, seq, d_model = x_prompt.shape
    dec_batch, dec_seq, _ = x_sample.shape
    d_conv = w_conv.shape[2]
    n_past = state_attn_k.shape[2]
    assert n_past >= max(WINDOWS) and seq <= max(WINDOWS) and seq % TQ == 0
    assert CONV_WIDTH - 1 <= dec_seq <= V7X_LANES
    slopes = _alibi_slopes()
    gf = norm_final_g.reshape(1, d_model)

    yp = x_prompt.reshape(batch * seq, d_model)
    ys = x_sample.reshape(dec_batch * dec_seq, d_model)
    outs = [[] for _ in range(6)]
    for layer in range(depth):
        g1 = norm_mix_g[layer].reshape(1, d_model)
        g2 = norm_ffn_g[layer].reshape(1, d_model)
        w_in_bf = w_in[layer].astype(BF16)
        w_qkv = w_in_bf[:, :3 * D_ATTN]
        w_cc = w_in_bf[:, 3 * D_ATTN:]
        weights = [w.astype(BF16) for w in (w_out[layer], w_gate[layer], w_up[layer], w_down[layer])]
        wc = w_conv[layer]

        qt, kt, vt, conv_o, u_last = _inproj_prompt(yp, g1, w_cc, w_qkv.T, wc, seq)
        attn_o = _attn_prompt(slopes, qt, kt, vt)
        y_layer = _ffn(yp, attn_o, conv_o, weights[0], g2, *weights[1:], gf,
                       layer == depth - 1, _row_tile(seq, largest=FFN_ROWS))
        to_heads = lambda a: a.reshape(batch, N_HEADS, HEAD_DIM, seq).transpose(0, 3, 1, 2)
        outs[0].append(to_heads(kt))
        outs[1].append(to_heads(vt))
        outs[2].append(u_last[:, V7X_SUBLANES - (CONV_WIDTH - 1):, :])
        yp = y_layer

        st = state_conv[layer]
        prev1 = jnp.pad(st[:, 1:2], ((0, 0), (0, dec_seq - 1), (0, 0)))
        prev2 = jnp.pad(st, ((0, 0), (0, dec_seq - 2), (0, 0)))
        qs, ks, vs, conv_s, u_s = _inproj_sample(
            ys, g1, w_cc, w_qkv, wc, prev1.reshape(-1, d_conv), prev2.reshape(-1, d_conv), dec_seq)
        as3 = lambda a: a.reshape(dec_batch, dec_seq, D_ATTN)
        feature_major = lambda a: a.transpose(0, 2, 3, 1).reshape(dec_batch, D_ATTN, n_past)
        attn_s = _attn_sample(slopes, as3(qs), as3(ks), as3(vs),
                              feature_major(state_attn_k[layer]),
                              feature_major(state_attn_v[layer]))
        ys = _ffn(ys, attn_s.reshape(-1, D_ATTN), conv_s, weights[0], g2, *weights[1:], gf,
                  layer == depth - 1, ys.shape[0])
        outs[3].append(ks.reshape(dec_batch, dec_seq, N_HEADS, HEAD_DIM))
        outs[4].append(vs.reshape(dec_batch, dec_seq, N_HEADS, HEAD_DIM))
        outs[5].append(u_s.reshape(dec_batch, dec_seq, d_conv)[:, dec_seq - (CONV_WIDTH - 1):])

    y_prompt = yp.reshape(batch, seq, d_model)
    y_sample = ys.reshape(dec_batch, dec_seq, d_model)
    new_k_p, new_v_p, new_c_p, new_k_s, new_v_s, new_c_s = [jnp.stack(o) for o in outs]
    return (y_prompt, y_sample, new_k_p, new_v_p, new_c_p, new_k_s, new_v_s, new_c_s)
```

```python
import functools

import jax
import jax.numpy as jnp
import numpy as np
from jax import lax
from jax.experimental import pallas as pl
from jax.experimental.pallas import tpu as pltpu

F32 = jnp.float32
BF16 = jnp.bfloat16

HEAD_DIM = 64
N_HEADS = 8
D_ATTN = N_HEADS * HEAD_DIM
CONV_WIDTH = 3
WINDOWS = (128, 512, 2048)
DILATIONS = (1, 4, 16)
RMS_EPS = 1e-6
ATTN_SCALE = HEAD_DIM ** -0.5
LOG2_E = 1.4426950408889634
NEG_INF = -1e30

V7X_LANES = 128
V7X_SUBLANES = 8
V7X_VMEM_BYTES = 64 * 1024 * 1024

HEADS_PER_STEP = V7X_LANES // HEAD_DIM
TQ = 256
TK = 256
ATTN_STREAMS = 4
ATTN_LOOKAHEAD = 1
ROW_GROUPS = 4
FFN_ROWS = 1024
INPROJ_ROW_GROUPS = 8
INPROJ_ROWS = 2048


def _rms(x, g):
    y = x * lax.rsqrt(jnp.mean(x * x, axis=-1, keepdims=True) + RMS_EPS)
    return y * g


def _branch_count(delta):
    nonneg = delta >= 0
    c = jnp.zeros(delta.shape, F32)
    for w, d in zip(WINDOWS, DILATIONS):
        hit = nonneg & ((delta & (d - 1)) == 0) & (delta <= w)
        c = c + hit.astype(F32)
    return c


def _dot_nt(a, b):
    return lax.dot_general(a, b, (((1,), (1,)), ((), ())), preferred_element_type=F32)


def _project(x, g, wcc_ref):
    d_conv = wcc_ref.shape[1] // 3
    xn = _rms(x, g).astype(BF16)
    zc = jnp.dot(xn, wcc_ref[...], preferred_element_type=F32)
    return xn, zc[:, 0:d_conv], zc[:, d_conv:2 * d_conv], zc[:, 2 * d_conv:3 * d_conv]


def _conv3(wc, u2, u1, u0):
    acc = wc[0:1, :] * u2
    acc = acc + wc[1:2, :] * u1
    return acc + wc[2:3, :] * u0


def _inproj_prompt_kernel(x_ref, g_ref, wcc_ref, wqkvt_ref, wc_ref, qt_ref, kt_ref, vt_ref, co_ref,
                          ulast_ref, uext_ref):
    tm = x_ref.shape[0]
    j = pl.program_id(1)
    tiles_per_seq = pl.num_programs(1)

    @pl.when(j == 0)
    def _():
        uext_ref[0:V7X_SUBLANES, :] = jnp.zeros((V7X_SUBLANES, uext_ref.shape[1]), F32)

    rg = tm // INPROJ_ROW_GROUPS
    for r0 in range(0, tm, rg):
        xn, hc, gb, gc = _project(x_ref[r0:r0 + rg, :], g_ref[...], wcc_ref)
        qkvt = _dot_nt(wqkvt_ref[...], xn)
        qt_ref[0, :, r0:r0 + rg] = (qkvt[0:D_ATTN, :] * (ATTN_SCALE * LOG2_E)).astype(qt_ref.dtype)
        kt_ref[0, :, r0:r0 + rg] = qkvt[D_ATTN:2 * D_ATTN, :]
        vt_ref[0, :, r0:r0 + rg] = qkvt[2 * D_ATTN:3 * D_ATTN, :]
        u = gc * hc
        uext_ref[V7X_SUBLANES:V7X_SUBLANES + rg, :] = u
        u1 = uext_ref[V7X_SUBLANES - 1:V7X_SUBLANES - 1 + rg, :]
        u2 = uext_ref[V7X_SUBLANES - 2:V7X_SUBLANES - 2 + rg, :]
        co_ref[r0:r0 + rg, :] = (gb * _conv3(wc_ref[...], u2, u1, u)).astype(co_ref.dtype)
        tail = u[rg - V7X_SUBLANES:rg, :]
        uext_ref[0:V7X_SUBLANES, :] = tail

    @pl.when(j == tiles_per_seq - 1)
    def _():
        ulast_ref[0] = tail


def _inproj_sample_kernel(x_ref, g_ref, wcc_ref, wqkv_ref, wc_ref, p1_ref, p2_ref, q_ref, k_ref,
                          v_ref, co_ref, u_ref, uext_ref, *, dec_seq):
    rows = x_ref.shape[0]
    xn, hc, gb, gc = _project(x_ref[...], g_ref[...], wcc_ref)
    qkv = jnp.dot(xn, wqkv_ref[...], preferred_element_type=F32)
    q_ref[...] = qkv[:, 0:D_ATTN] * ATTN_SCALE
    k_ref[...] = qkv[:, D_ATTN:2 * D_ATTN]
    v_ref[...] = qkv[:, 2 * D_ATTN:3 * D_ATTN]
    u = gc * hc
    u_ref[...] = u
    uext_ref[0:V7X_SUBLANES, :] = jnp.zeros((V7X_SUBLANES, u.shape[1]), F32)
    uext_ref[V7X_SUBLANES:V7X_SUBLANES + rows, :] = u
    t = lax.broadcasted_iota(jnp.int32, u.shape, 0) % dec_seq
    u1 = jnp.where(t >= 1, uext_ref[V7X_SUBLANES - 1:V7X_SUBLANES - 1 + rows, :], p1_ref[...])
    u2 = jnp.where(t >= 2, uext_ref[V7X_SUBLANES - 2:V7X_SUBLANES - 2 + rows, :], p2_ref[...])
    co_ref[...] = (gb * _conv3(wc_ref[...], u2, u1, u)).astype(co_ref.dtype)


def _attn_tables(seq):
    x = np.arange(seq, dtype=np.int32)[:, None]
    i = np.arange(TQ, dtype=np.int32)[None, :]
    delta = i + (seq - TQ) - x
    c = np.zeros(delta.shape, np.float32)
    for w, d in zip(WINDOWS, DILATIONS):
        c += (delta >= 0) & (delta % d == 0) & (delta <= w)
    logc = np.where(c > 0.0, np.log2(np.maximum(c, 1.0)), NEG_INF).astype(np.float32)
    return delta.astype(np.float32), logc


def _attn_prompt_kernel(slopes_ref, dist_ref, logc_ref, qt_ref, kt_ref, vt_ref, o_ref,
                        kb_ref, vtb_ref, bm_ref):
    seq = qt_ref.shape[2]
    pair = pl.program_id(0)
    feature = lax.broadcasted_iota(jnp.int32, (V7X_LANES, 1), 0)
    first_head = feature < HEAD_DIM

    @pl.when(pl.program_id(1) == 0)
    def _():
        for h in range(HEADS_PER_STEP):
            slope = slopes_ref[HEADS_PER_STEP * pair + h]
            bm_ref[h] = logc_ref[...] - (slope * LOG2_E) * dist_ref[...]

    kb_ref[...] = kt_ref[0].T.astype(BF16)
    vtb_ref[...] = vt_ref[0].astype(BF16)

    def score_half(st, chunk):
        k0, n = chunk
        h, row0 = st["h"], seq - (st["q0"] + TQ)
        s = (jnp.dot(kb_ref[k0:k0 + n, :], st["q"], preferred_element_type=F32)
             + bm_ref[h, row0 + k0:row0 + k0 + n, :])
        c8 = jnp.max(s.reshape(n // V7X_SUBLANES, V7X_SUBLANES, TQ), axis=0)
        m_new = jnp.maximum(st["m"], jnp.max(c8, axis=0, keepdims=True))
        pending = dict(k0=k0, n=n, s=s, m=m_new, alpha=jnp.exp2(st["m"] - m_new))
        st["m"] = m_new
        return pending

    def value_half(st, pending):
        h, k0, n = st["h"], pending["k0"], pending["n"]
        p = jnp.exp2(pending["s"] - pending["m"])
        st["l8"] = (pending["alpha"] * st["l8"]
                    + jnp.sum(p.reshape(n // V7X_SUBLANES, V7X_SUBLANES, TQ), axis=0))
        vt_h = vtb_ref[h * HEAD_DIM:(h + 1) * HEAD_DIM, k0:k0 + n]
        st["acc"] = pending["alpha"] * st["acc"] + jnp.dot(vt_h, p.astype(BF16),
                                                           preferred_element_type=F32)

    def open_block(q0):
        qt = qt_ref[0, :, q0:q0 + TQ]
        zero = jnp.zeros_like(qt)
        chunks = [(q0, TQ)] + [(max(k1 - TK, 0), min(TK, k1)) for k1 in range(q0, 0, -TK)]
        block = dict(q0=q0, left=HEADS_PER_STEP * len(chunks))
        block["streams"] = [
            dict(block=block, h=h, q0=q0, q=qh, chunks=chunks, m=jnp.full((1, TQ), NEG_INF, F32),
                 l8=jnp.zeros((V7X_SUBLANES, TQ), F32), acc=jnp.zeros((HEAD_DIM, TQ), F32))
            for h, qh in enumerate([jnp.where(first_head, qt, zero), jnp.where(first_head, zero, qt)])]
        return block

    def retire(st, pending):
        value_half(st, pending)
        block = st["block"]
        block["left"] -= 1
        if block["left"] == 0:
            outs = [s["acc"] / jnp.sum(s["l8"], axis=0, keepdims=True) for s in block["streams"]]
            q0 = block["q0"]
            o_ref[q0:q0 + TQ, :] = jnp.concatenate(outs, axis=0).T.astype(o_ref.dtype)

    blocks_per_group = ATTN_STREAMS // HEADS_PER_STEP
    waiting = []
    for g0 in range(0, seq, TQ * blocks_per_group):
        streams = [st for q0 in range(g0, g0 + TQ * blocks_per_group, TQ)
                   for st in open_block(q0)["streams"]]
        for t in range(max(len(st["chunks"]) for st in streams)):
            live = [(st, st["chunks"][t]) for st in streams if t < len(st["chunks"])]
            due = waiting.pop(0) if len(waiting) >= ATTN_LOOKAHEAD else []
            issued = []
            for i, (st, chunk) in enumerate(live):
                issued.append((st, score_half(st, chunk)))
                if i < len(due):
                    retire(*due[i])
            for st, pending in due[len(live):]:
                retire(st, pending)
            waiting.append(issued)
    for step in waiting:
        for st, pending in step:
            retire(st, pending)


def _attn_sample_kernel(slopes_ref, q_ref, kn_ref, vn_ref, kst_ref, vst_ref, o_ref,
                        knew_ref, vnew_ref):
    dec_seq = q_ref.shape[1]
    n_past = kst_ref.shape[2]
    width = q_ref.shape[2]
    n_rows = dec_seq * N_HEADS

    q = q_ref[0]
    head_of_lane = lax.broadcasted_iota(jnp.int32, (N_HEADS, width), 1) // HEAD_DIM
    head_of_row = lax.broadcasted_iota(jnp.int32, (N_HEADS, width), 0)
    head_mask = head_of_lane == head_of_row
    wt = jnp.concatenate(
        [jnp.where(head_mask, jnp.broadcast_to(q[t:t + 1, :], (N_HEADS, width)), 0.0)
         for t in range(dec_seq)], axis=0).astype(BF16)

    knew_ref[...] = jnp.zeros(knew_ref.shape, F32)
    vnew_ref[...] = jnp.zeros(vnew_ref.shape, F32)
    knew_ref[0:dec_seq, :] = kn_ref[0]
    vnew_ref[0:dec_seq, :] = vn_ref[0]

    s_past = jnp.dot(wt, kst_ref[0].astype(BF16), preferred_element_type=F32)
    s_new = _dot_nt(wt, knew_ref[...].astype(BF16))

    row = lax.broadcasted_iota(jnp.int32, (n_rows, 1), 0)
    step = row // N_HEADS
    slope_col = jnp.zeros((n_rows, 1), F32)
    for h in range(N_HEADS):
        slope_col = jnp.where(row % N_HEADS == h, slopes_ref[h], slope_col)

    def weigh(s, key_pos):
        delta = n_past + step - key_pos
        c = _branch_count(delta)
        s = jnp.where(c > 0.0, s - slope_col * delta.astype(F32), NEG_INF)
        return s, c

    s_past, c_past = weigh(s_past, lax.broadcasted_iota(jnp.int32, (1, n_past), 1))
    s_new, c_new = weigh(s_new, n_past + lax.broadcasted_iota(jnp.int32, (1, V7X_LANES), 1))
    m = jnp.maximum(jnp.max(s_past, axis=-1, keepdims=True),
                    jnp.max(s_new, axis=-1, keepdims=True))
    p_past = c_past * jnp.exp(s_past - m)
    p_new = c_new * jnp.exp(s_new - m)
    l = jnp.sum(p_past, axis=-1, keepdims=True) + jnp.sum(p_new, axis=-1, keepdims=True)
    o_all = (_dot_nt(p_past.astype(BF16), vst_ref[0].astype(BF16))
             + jnp.dot(p_new.astype(BF16), vnew_ref[...].astype(BF16),
                       preferred_element_type=F32)) / l
    outs = []
    for t in range(dec_seq):
        blk = o_all[t * N_HEADS:(t + 1) * N_HEADS, :]
        outs.append(jnp.sum(jnp.where(head_mask, blk, 0.0), axis=0, keepdims=True))
    o_ref[0] = jnp.concatenate(outs, axis=0).astype(o_ref.dtype)


def _ffn_kernel(x_ref, a_ref, c_ref, wo_ref, g2_ref, wg_ref, wu_ref, wd_ref, gf_ref, y_ref, *,
                final_norm):
    tm = x_ref.shape[0]
    rg = tm // ROW_GROUPS if tm % (ROW_GROUPS * V7X_LANES) == 0 else tm
    groups = [slice(r0, r0 + rg) for r0 in range(0, tm, rg)]
    hs = []
    for rows in groups:
        mix = jnp.concatenate([a_ref[rows, :].astype(BF16), c_ref[rows, :].astype(BF16)], axis=-1)
        hs.append(x_ref[rows, :] + jnp.dot(mix, wo_ref[...], preferred_element_type=F32))
    acts = []
    for h in hs:
        hn = _rms(h, g2_ref[...]).astype(BF16)
        gate = jnp.dot(hn, wg_ref[...], preferred_element_type=F32)
        up = jnp.dot(hn, wu_ref[...], preferred_element_type=F32)
        acts.append(((gate * (1.0 / (1.0 + jnp.exp(-gate)))) * up).astype(BF16))
    for rows, h, act in zip(groups, hs, acts):
        y = h + jnp.dot(act, wd_ref[...], preferred_element_type=F32)
        y_ref[rows, :] = _rms(y, gf_ref[...]) if final_norm else y


def _nbytes(shape, dtype):
    n = 1
    for s in shape:
        n *= s
    return n * jnp.dtype(dtype).itemsize


def _vmem_limit(pipelined, resident, temporaries):
    need = 2 * sum(pipelined) + sum(resident) + sum(temporaries)
    return min(V7X_VMEM_BYTES, need + need // 4)


def _resident(shape):
    zeros = (0,) * len(shape)
    return pl.BlockSpec(shape, lambda *_: zeros, pipeline_mode=pl.Buffered(1))


def _row_tile(rows, largest=512):
    for tm in (largest, 512, 256, 128):
        if rows % tm == 0:
            return tm
    raise ValueError(f"row count {rows} is not a multiple of 128")


def _inproj_prompt(x, g, w_cc, w_qkvt, wc, seq):
    rows, d_model = x.shape
    d_conv = wc.shape[1]
    tm = _row_tile(seq, largest=INPROJ_ROWS)
    tiles_per_seq = seq // tm
    n_seq = rows // seq
    row_blk = lambda width: pl.BlockSpec((tm, width), lambda b, j: (b * tiles_per_seq + j, 0))
    col_blk = pl.BlockSpec((1, D_ATTN, tm), lambda b, j: (b, 0, j))
    limit = _vmem_limit(
        [_nbytes((tm, d_model), F32), _nbytes((D_ATTN, tm), BF16), 2 * _nbytes((D_ATTN, tm), F32),
         _nbytes((tm, d_conv), BF16)],
        [_nbytes(w_cc.shape, BF16), _nbytes(w_qkvt.shape, BF16)],
        [2 * _nbytes((tm, d_model), F32), 2 * _nbytes((tm, w_cc.shape[1]), F32),
         2 * _nbytes((w_qkvt.shape[0], tm), F32), 4 * _nbytes((tm, d_conv), F32)])
    return pl.pallas_call(
        _inproj_prompt_kernel,
        grid=(n_seq, tiles_per_seq),
        in_specs=[row_blk(d_model), _resident((1, d_model)), _resident(w_cc.shape),
                  _resident(w_qkvt.shape), _resident(wc.shape)],
        out_specs=[col_blk, col_blk, col_blk, row_blk(d_conv),
                   pl.BlockSpec((1, V7X_SUBLANES, d_conv), lambda b, j: (b, 0, 0))],
        out_shape=[jax.ShapeDtypeStruct((n_seq, D_ATTN, seq), BF16),
                   jax.ShapeDtypeStruct((n_seq, D_ATTN, seq), F32),
                   jax.ShapeDtypeStruct((n_seq, D_ATTN, seq), F32),
                   jax.ShapeDtypeStruct((rows, d_conv), BF16),
                   jax.ShapeDtypeStruct((n_seq, V7X_SUBLANES, d_conv), F32)],
        scratch_shapes=[pltpu.VMEM((tm + V7X_SUBLANES, d_conv), F32)],
        compiler_params=pltpu.CompilerParams(dimension_semantics=("arbitrary", "arbitrary"),
                                             vmem_limit_bytes=limit),
        name="inproj_prompt",
    )(x, g, w_cc, w_qkvt, wc)


def _inproj_sample(x, g, w_cc, w_qkv, wc, prev1, prev2, dec_seq):
    rows, d_model = x.shape
    d_conv = wc.shape[1]
    full = lambda shape: pl.BlockSpec(shape, lambda i: (0,) * len(shape))
    return pl.pallas_call(
        functools.partial(_inproj_sample_kernel, dec_seq=dec_seq),
        grid=(1,),
        in_specs=[full(x.shape), full((1, d_model)), full(w_cc.shape), full(w_qkv.shape),
                  full(wc.shape), full(prev1.shape), full(prev2.shape)],
        out_specs=[full((rows, D_ATTN))] * 3 + [full((rows, d_conv))] * 2,
        out_shape=[jax.ShapeDtypeStruct((rows, D_ATTN), F32)] * 3
        + [jax.ShapeDtypeStruct((rows, d_conv), BF16), jax.ShapeDtypeStruct((rows, d_conv), F32)],
        scratch_shapes=[pltpu.VMEM((rows + V7X_SUBLANES, d_conv), F32)],
        compiler_params=pltpu.CompilerParams(dimension_semantics=("arbitrary",)),
        name="inproj_sample",
    )(x, g, w_cc, w_qkv, wc, prev1, prev2)


def _attn_prompt(slopes, qt, kt, vt):
    n_seq, _, seq = qt.shape
    rows = n_seq * seq
    o_blk = pl.BlockSpec((seq, V7X_LANES), lambda p, b: (b, p))
    kv_blk = pl.BlockSpec((1, V7X_LANES, seq), lambda p, b: (b, p, 0))
    dist, logc = _attn_tables(seq)
    table = _nbytes((TQ, seq), F32)
    limit = _vmem_limit(
        [2 * _nbytes((seq, V7X_LANES), F32), 2 * _nbytes((seq, V7X_LANES), BF16)],
        [2 * _nbytes((seq, V7X_LANES), BF16), (2 + HEADS_PER_STEP) * table],
        [3 * HEADS_PER_STEP * table])
    return pl.pallas_call(
        _attn_prompt_kernel,
        grid=(N_HEADS // HEADS_PER_STEP, n_seq),
        in_specs=[pl.BlockSpec(memory_space=pltpu.SMEM), _resident((seq, TQ)),
                  _resident((seq, TQ)), kv_blk, kv_blk, kv_blk],
        out_specs=o_blk,
        out_shape=jax.ShapeDtypeStruct((rows, D_ATTN), BF16),
        scratch_shapes=[pltpu.VMEM((seq, V7X_LANES), BF16), pltpu.VMEM((V7X_LANES, seq), BF16),
                        pltpu.VMEM((HEADS_PER_STEP, seq, TQ), F32)],
        compiler_params=pltpu.CompilerParams(dimension_semantics=("arbitrary", "arbitrary"),
                                             vmem_limit_bytes=limit),
        name="attn_prompt",
    )(slopes, dist, logc, qt, kt, vt)


def _attn_sample(slopes, q, kn, vn, kst, vst):
    n_seq, dec_seq, width = q.shape
    n_past = kst.shape[2]
    new_blk = pl.BlockSpec((1, dec_seq, width), lambda b: (b, 0, 0))
    past_blk = pl.BlockSpec((1, width, n_past), lambda b: (b, 0, 0))
    limit = _vmem_limit(
        [2 * _nbytes((n_past, width), F32)], [],
        [2 * _nbytes((n_past, width), BF16), 8 * _nbytes((dec_seq * N_HEADS, n_past), F32)])
    return pl.pallas_call(
        _attn_sample_kernel,
        grid=(n_seq,),
        in_specs=[pl.BlockSpec(memory_space=pltpu.SMEM), new_blk, new_blk, new_blk,
                  past_blk, past_blk],
        out_specs=new_blk,
        out_shape=jax.ShapeDtypeStruct((n_seq, dec_seq, width), F32),
        scratch_shapes=[pltpu.VMEM((V7X_LANES, width), F32)] * 2,
        compiler_params=pltpu.CompilerParams(dimension_semantics=("arbitrary",),
                                             vmem_limit_bytes=limit),
        name="attn_sample",
    )(slopes, q, kn, vn, kst, vst)


def _ffn(x, attn_o, conv_o, wo, g2, wg, wu, wd, gf, final_norm, tm):
    rows, d_model = x.shape
    d_ff = wg.shape[1]
    row_blk = lambda width: pl.BlockSpec((tm, width), lambda i: (i, 0))
    limit = _vmem_limit(
        [2 * _nbytes((tm, d_model), F32), _nbytes((tm, attn_o.shape[1]), attn_o.dtype),
         _nbytes((tm, conv_o.shape[1]), conv_o.dtype)],
        [_nbytes(w.shape, BF16) for w in (wo, wg, wu, wd)],
        [4 * _nbytes((tm, d_model), F32), 3 * _nbytes((tm, d_ff), F32)])
    return pl.pallas_call(
        functools.partial(_ffn_kernel, final_norm=final_norm),
        grid=(rows // tm,),
        in_specs=[row_blk(d_model), row_blk(attn_o.shape[1]), row_blk(conv_o.shape[1]),
                  _resident(wo.shape), _resident((1, d_model)), _resident(wg.shape),
                  _resident(wu.shape), _resident(wd.shape), _resident((1, d_model))],
        out_specs=row_blk(d_model),
        out_shape=jax.ShapeDtypeStruct((rows, d_model), F32),
        compiler_params=pltpu.CompilerParams(dimension_semantics=("arbitrary",),
                                             vmem_limit_bytes=limit),
        name="ffn",
    )(x, attn_o, conv_o, wo, g2, wg, wu, wd, gf)


def _alibi_slopes():
    return jnp.exp2(-8.0 * jnp.arange(1, N_HEADS + 1, dtype=F32) / N_HEADS)


def kernel(x_prompt, x_sample, state_attn_k, state_attn_v, state_conv, norm_mix_g, w_in, w_conv,
           w_out, norm_ffn_g, w_gate, w_up, w_down, norm_final_g):
    depth = w_in.shape[0]
    batch, seq, d_model = x_prompt.shape
    dec_batch, dec_seq, _ = x_sample.shape
    d_conv = w_conv.shape[2]
    n_past = state_attn_k.shape[2]
    assert n_past >= max(WINDOWS) and seq <= max(WINDOWS) and seq % TQ == 0
    assert CONV_WIDTH - 1 <= dec_seq <= V7X_LANES
    slopes = _alibi_slopes()
    gf = norm_final_g.reshape(1, d_model)

    yp = x_prompt.reshape(batch * seq, d_model)
    ys = x_sample.reshape(dec_batch * dec_seq, d_model)
    outs = [[] for _ in range(6)]
    for layer in range(depth):
        g1 = norm_mix_g[layer].reshape(1, d_model)
        g2 = norm_ffn_g[layer].reshape(1, d_model)
        w_in_bf = w_in[layer].astype(BF16)
        w_qkv = w_in_bf[:, :3 * D_ATTN]
        w_cc = w_in_bf[:, 3 * D_ATTN:]
        weights = [w.astype(BF16) for w in (w_out[layer], w_gate[layer], w_up[layer], w_down[layer])]
        wc = w_conv[layer]

        qt, kt, vt, conv_o, u_last = _inproj_prompt(yp, g1, w_cc, w_qkv.T, wc, seq)
        attn_o = _attn_prompt(slopes, qt, kt, vt)
        y_layer = _ffn(yp, attn_o, conv_o, weights[0], g2, *weights[1:], gf,
                       layer == depth - 1, _row_tile(seq, largest=FFN_ROWS))
        to_heads = lambda a: a.reshape(batch, N_HEADS, HEAD_DIM, seq).transpose(0, 3, 1, 2)
        outs[0].append(to_heads(kt))
        outs[1].append(to_heads(vt))
        outs[2].append(u_last[:, V7X_SUBLANES - (CONV_WIDTH - 1):, :])
        yp = y_layer

        st = state_conv[layer]
        prev1 = jnp.pad(st[:, 1:2], ((0, 0), (0, dec_seq - 1), (0, 0)))
        prev2 = jnp.pad(st, ((0, 0), (0, dec_seq - 2), (0, 0)))
        qs, ks, vs, conv_s, u_s = _inproj_sample(
            ys, g1, w_cc, w_qkv, wc, prev1.reshape(-1, d_conv), prev2.reshape(-1, d_conv), dec_seq)
        as3 = lambda a: a.reshape(dec_batch, dec_seq, D_ATTN)
        feature_major = lambda a: a.transpose(0, 2, 3, 1).reshape(dec_batch, D_ATTN, n_past)
        attn_s = _attn_sample(slopes, as3(qs), as3(ks), as3(vs),
                              feature_major(state_attn_k[layer]),
                              feature_major(state_attn_v[layer]))
        ys = _ffn(ys, attn_s.reshape(-1, D_ATTN), conv_s, weights[0], g2, *weights[1:], gf,
                  layer == depth - 1, ys.shape[0])
        outs[3].append(ks.reshape(dec_batch, dec_seq, N_HEADS, HEAD_DIM))
        outs[4].append(vs.reshape(dec_batch, dec_seq, N_HEADS, HEAD_DIM))
        outs[5].append(u_s.reshape(dec_batch, dec_seq, d_conv)[:, dec_seq - (CONV_WIDTH - 1):])

    y_prompt = yp.reshape(batch, seq, d_model)
    y_sample = ys.reshape(dec_batch, dec_seq, d_model)
    new_k_p, new_v_p, new_c_p, new_k_s, new_v_s, new_c_s = [jnp.stack(o) for o in outs]
    return (y_prompt, y_sample, new_k_p, new_v_p, new_c_p, new_k_s, new_v_s, new_c_s)
```

```python
import functools

import jax
import jax.numpy as jnp
import numpy as np
from jax import lax
from jax.experimental import pallas as pl
from jax.experimental.pallas import tpu as pltpu

F32 = jnp.float32
BF16 = jnp.bfloat16

HEAD_DIM = 64
N_HEADS = 8
D_ATTN = N_HEADS * HEAD_DIM
CONV_WIDTH = 3
WINDOWS = (128, 512, 2048)
DILATIONS = (1, 4, 16)
RMS_EPS = 1e-6
ATTN_SCALE = HEAD_DIM ** -0.5
LOG2_E = 1.4426950408889634
NEG_INF = -1e30

V7X_LANES = 128
V7X_SUBLANES = 8
V7X_VMEM_BYTES = 64 * 1024 * 1024

HEADS_PER_STEP = V7X_LANES // HEAD_DIM
TQ = 256
TK = 512
ATTN_STREAMS = 4
ATTN_LOOKAHEAD = 1
ROW_GROUPS = 4
FFN_ROWS = 1024
INPROJ_ROW_GROUPS = 8
INPROJ_ROWS = 2048


def _rms(x, g):
    y = x * lax.rsqrt(jnp.mean(x * x, axis=-1, keepdims=True) + RMS_EPS)
    return y * g


def _branch_count(delta):
    nonneg = delta >= 0
    c = jnp.zeros(delta.shape, F32)
    for w, d in zip(WINDOWS, DILATIONS):
        hit = nonneg & ((delta & (d - 1)) == 0) & (delta <= w)
        c = c + hit.astype(F32)
    return c


def _dot_nt(a, b):
    return lax.dot_general(a, b, (((1,), (1,)), ((), ())), preferred_element_type=F32)


def _project(x, g, wcc_ref):
    d_conv = wcc_ref.shape[1] // 3
    xn = _rms(x, g).astype(BF16)
    zc = jnp.dot(xn, wcc_ref[...], preferred_element_type=F32)
    return xn, zc[:, 0:d_conv], zc[:, d_conv:2 * d_conv], zc[:, 2 * d_conv:3 * d_conv]


def _conv3(wc, u2, u1, u0):
    acc = wc[0:1, :] * u2
    acc = acc + wc[1:2, :] * u1
    return acc + wc[2:3, :] * u0


def _inproj_prompt_kernel(x_ref, g_ref, wcc_ref, wqkvt_ref, wc_ref, qt_ref, kt_ref, vt_ref, co_ref,
                          ulast_ref, uext_ref):
    tm = x_ref.shape[0]
    j = pl.program_id(1)
    tiles_per_seq = pl.num_programs(1)

    @pl.when(j == 0)
    def _():
        uext_ref[0:V7X_SUBLANES, :] = jnp.zeros((V7X_SUBLANES, uext_ref.shape[1]), F32)

    rg = tm // INPROJ_ROW_GROUPS
    for r0 in range(0, tm, rg):
        xn, hc, gb, gc = _project(x_ref[r0:r0 + rg, :], g_ref[...], wcc_ref)
        qkvt = _dot_nt(wqkvt_ref[...], xn)
        qt_ref[0, :, r0:r0 + rg] = (qkvt[0:D_ATTN, :] * (ATTN_SCALE * LOG2_E)).astype(qt_ref.dtype)
        kt_ref[0, :, r0:r0 + rg] = qkvt[D_ATTN:2 * D_ATTN, :]
        vt_ref[0, :, r0:r0 + rg] = qkvt[2 * D_ATTN:3 * D_ATTN, :]
        u = gc * hc
        uext_ref[V7X_SUBLANES:V7X_SUBLANES + rg, :] = u
        u1 = uext_ref[V7X_SUBLANES - 1:V7X_SUBLANES - 1 + rg, :]
        u2 = uext_ref[V7X_SUBLANES - 2:V7X_SUBLANES - 2 + rg, :]
        co_ref[r0:r0 + rg, :] = (gb * _conv3(wc_ref[...], u2, u1, u)).astype(co_ref.dtype)
        tail = u[rg - V7X_SUBLANES:rg, :]
        uext_ref[0:V7X_SUBLANES, :] = tail

    @pl.when(j == tiles_per_seq - 1)
    def _():
        ulast_ref[0] = tail


def _inproj_sample_kernel(x_ref, g_ref, wcc_ref, wqkv_ref, wc_ref, p1_ref, p2_ref, q_ref, k_ref,
                          v_ref, co_ref, u_ref, uext_ref, *, dec_seq):
    rows = x_ref.shape[0]
    xn, hc, gb, gc = _project(x_ref[...], g_ref[...], wcc_ref)
    qkv = jnp.dot(xn, wqkv_ref[...], preferred_element_type=F32)
    q_ref[...] = qkv[:, 0:D_ATTN] * ATTN_SCALE
    k_ref[...] = qkv[:, D_ATTN:2 * D_ATTN]
    v_ref[...] = qkv[:, 2 * D_ATTN:3 * D_ATTN]
    u = gc * hc
    u_ref[...] = u
    uext_ref[0:V7X_SUBLANES, :] = jnp.zeros((V7X_SUBLANES, u.shape[1]), F32)
    uext_ref[V7X_SUBLANES:V7X_SUBLANES + rows, :] = u
    t = lax.broadcasted_iota(jnp.int32, u.shape, 0) % dec_seq
    u1 = jnp.where(t >= 1, uext_ref[V7X_SUBLANES - 1:V7X_SUBLANES - 1 + rows, :], p1_ref[...])
    u2 = jnp.where(t >= 2, uext_ref[V7X_SUBLANES - 2:V7X_SUBLANES - 2 + rows, :], p2_ref[...])
    co_ref[...] = (gb * _conv3(wc_ref[...], u2, u1, u)).astype(co_ref.dtype)


def _attn_tables(seq):
    x = np.arange(seq, dtype=np.int32)[:, None]
    i = np.arange(TQ, dtype=np.int32)[None, :]
    delta = i + (seq - TQ) - x
    c = np.zeros(delta.shape, np.float32)
    for w, d in zip(WINDOWS, DILATIONS):
        c += (delta >= 0) & (delta % d == 0) & (delta <= w)
    logc = np.where(c > 0.0, np.log2(np.maximum(c, 1.0)), NEG_INF).astype(np.float32)
    return delta.astype(np.float32), logc


def _attn_prompt_kernel(slopes_ref, dist_ref, logc_ref, qt_ref, kt_ref, vt_ref, o_ref,
                        kb_ref, vtb_ref, bm_ref):
    seq = qt_ref.shape[2]
    pair = pl.program_id(0)
    feature = lax.broadcasted_iota(jnp.int32, (V7X_LANES, 1), 0)
    first_head = feature < HEAD_DIM

    @pl.when(pl.program_id(1) == 0)
    def _():
        for h in range(HEADS_PER_STEP):
            slope = slopes_ref[HEADS_PER_STEP * pair + h]
            bm_ref[h] = logc_ref[...] - (slope * LOG2_E) * dist_ref[...]

    kb_ref[...] = kt_ref[0].T.astype(BF16)
    vtb_ref[...] = vt_ref[0].astype(BF16)

    def score_half(st, chunk):
        k0, n = chunk
        h, row0 = st["h"], seq - (st["q0"] + TQ)
        s = (jnp.dot(kb_ref[k0:k0 + n, :], st["q"], preferred_element_type=F32)
             + bm_ref[h, row0 + k0:row0 + k0 + n, :])
        c8 = jnp.max(s.reshape(n // V7X_SUBLANES, V7X_SUBLANES, TQ), axis=0)
        m_new = jnp.maximum(st["m"], jnp.max(c8, axis=0, keepdims=True))
        pending = dict(k0=k0, n=n, s=s, m=m_new, alpha=jnp.exp2(st["m"] - m_new))
        st["m"] = m_new
        return pending

    def value_half(st, pending):
        h, k0, n = st["h"], pending["k0"], pending["n"]
        p = jnp.exp2(pending["s"] - pending["m"])
        st["l8"] = (pending["alpha"] * st["l8"]
                    + jnp.sum(p.reshape(n // V7X_SUBLANES, V7X_SUBLANES, TQ), axis=0))
        vt_h = vtb_ref[h * HEAD_DIM:(h + 1) * HEAD_DIM, k0:k0 + n]
        st["acc"] = pending["alpha"] * st["acc"] + jnp.dot(vt_h, p.astype(BF16),
                                                           preferred_element_type=F32)

    def open_block(q0):
        qt = qt_ref[0, :, q0:q0 + TQ]
        zero = jnp.zeros_like(qt)
        chunks = [(q0, TQ)] + [(max(k1 - TK, 0), min(TK, k1)) for k1 in range(q0, 0, -TK)]
        block = dict(q0=q0, left=HEADS_PER_STEP * len(chunks))
        block["streams"] = [
            dict(block=block, h=h, q0=q0, q=qh, chunks=chunks, m=jnp.full((1, TQ), NEG_INF, F32),
                 l8=jnp.zeros((V7X_SUBLANES, TQ), F32), acc=jnp.zeros((HEAD_DIM, TQ), F32))
            for h, qh in enumerate([jnp.where(first_head, qt, zero), jnp.where(first_head, zero, qt)])]
        return block

    def retire(st, pending):
        value_half(st, pending)
        block = st["block"]
        block["left"] -= 1
        if block["left"] == 0:
            outs = [s["acc"] / jnp.sum(s["l8"], axis=0, keepdims=True) for s in block["streams"]]
            q0 = block["q0"]
            o_ref[q0:q0 + TQ, :] = jnp.concatenate(outs, axis=0).T.astype(o_ref.dtype)

    blocks_per_group = ATTN_STREAMS // HEADS_PER_STEP
    waiting = []
    for g0 in range(0, seq, TQ * blocks_per_group):
        streams = [st for q0 in range(g0, g0 + TQ * blocks_per_group, TQ)
                   for st in open_block(q0)["streams"]]
        for t in range(max(len(st["chunks"]) for st in streams)):
            live = [(st, st["chunks"][t]) for st in streams if t < len(st["chunks"])]
            due = waiting.pop(0) if len(waiting) >= ATTN_LOOKAHEAD else []
            issued = []
            for i, (st, chunk) in enumerate(live):
                issued.append((st, score_half(st, chunk)))
                if i < len(due):
                    retire(*due[i])
            for st, pending in due[len(live):]:
                retire(st, pending)
            waiting.append(issued)
    for step in waiting:
        for st, pending in step:
            retire(st, pending)


def _attn_sample_kernel(slopes_ref, q_ref, kn_ref, vn_ref, kst_ref, vst_ref, o_ref,
                        knew_ref, vnew_ref):
    dec_seq = q_ref.shape[1]
    n_past = kst_ref.shape[2]
    width = q_ref.shape[2]
    n_rows = dec_seq * N_HEADS

    q = q_ref[0]
    head_of_lane = lax.broadcasted_iota(jnp.int32, (N_HEADS, width), 1) // HEAD_DIM
    head_of_row = lax.broadcasted_iota(jnp.int32, (N_HEADS, width), 0)
    head_mask = head_of_lane == head_of_row
    wt = jnp.concatenate(
        [jnp.where(head_mask, jnp.broadcast_to(q[t:t + 1, :], (N_HEADS, width)), 0.0)
         for t in range(dec_seq)], axis=0).astype(BF16)

    knew_ref[...] = jnp.zeros(knew_ref.shape, F32)
    vnew_ref[...] = jnp.zeros(vnew_ref.shape, F32)
    knew_ref[0:dec_seq, :] = kn_ref[0]
    vnew_ref[0:dec_seq, :] = vn_ref[0]

    s_past = jnp.dot(wt, kst_ref[0].astype(BF16), preferred_element_type=F32)
    s_new = _dot_nt(wt, knew_ref[...].astype(BF16))

    row = lax.broadcasted_iota(jnp.int32, (n_rows, 1), 0)
    step = row // N_HEADS
    slope_col = jnp.zeros((n_rows, 1), F32)
    for h in range(N_HEADS):
        slope_col = jnp.where(row % N_HEADS == h, slopes_ref[h], slope_col)

    def weigh(s, key_pos):
        delta = n_past + step - key_pos
        c = _branch_count(delta)
        s = jnp.where(c > 0.0, s - slope_col * delta.astype(F32), NEG_INF)
        return s, c

    s_past, c_past = weigh(s_past, lax.broadcasted_iota(jnp.int32, (1, n_past), 1))
    s_new, c_new = weigh(s_new, n_past + lax.broadcasted_iota(jnp.int32, (1, V7X_LANES), 1))
    m = jnp.maximum(jnp.max(s_past, axis=-1, keepdims=True),
                    jnp.max(s_new, axis=-1, keepdims=True))
    p_past = c_past * jnp.exp(s_past - m)
    p_new = c_new * jnp.exp(s_new - m)
    l = jnp.sum(p_past, axis=-1, keepdims=True) + jnp.sum(p_new, axis=-1, keepdims=True)
    o_all = (_dot_nt(p_past.astype(BF16), vst_ref[0].astype(BF16))
             + jnp.dot(p_new.astype(BF16), vnew_ref[...].astype(BF16),
                       preferred_element_type=F32)) / l
    outs = []
    for t in range(dec_seq):
        blk = o_all[t * N_HEADS:(t + 1) * N_HEADS, :]
        outs.append(jnp.sum(jnp.where(head_mask, blk, 0.0), axis=0, keepdims=True))
    o_ref[0] = jnp.concatenate(outs, axis=0).astype(o_ref.dtype)


def _ffn_kernel(x_ref, a_ref, c_ref, wo_ref, g2_ref, wg_ref, wu_ref, wd_ref, gf_ref, y_ref, *,
                final_norm):
    tm = x_ref.shape[0]
    rg = tm // ROW_GROUPS if tm % (ROW_GROUPS * V7X_LANES) == 0 else tm
    groups = [slice(r0, r0 + rg) for r0 in range(0, tm, rg)]
    hs = []
    for rows in groups:
        mix = jnp.concatenate([a_ref[rows, :].astype(BF16), c_ref[rows, :].astype(BF16)], axis=-1)
        hs.append(x_ref[rows, :] + jnp.dot(mix, wo_ref[...], preferred_element_type=F32))
    acts = []
    for h in hs:
        hn = _rms(h, g2_ref[...]).astype(BF16)
        gate = jnp.dot(hn, wg_ref[...], preferred_element_type=F32)
        up = jnp.dot(hn, wu_ref[...], preferred_element_type=F32)
        acts.append(((gate * (1.0 / (1.0 + jnp.exp(-gate)))) * up).astype(BF16))
    for rows, h, act in zip(groups, hs, acts):
        y = h + jnp.dot(act, wd_ref[...], preferred_element_type=F32)
        y_ref[rows, :] = _rms(y, gf_ref[...]) if final_norm else y


def _nbytes(shape, dtype):
    n = 1
    for s in shape:
        n *= s
    return n * jnp.dtype(dtype).itemsize


def _vmem_limit(pipelined, resident, temporaries):
    need = 2 * sum(pipelined) + sum(resident) + sum(temporaries)
    return min(V7X_VMEM_BYTES, need + need // 4)


def _resident(shape):
    zeros = (0,) * len(shape)
    return pl.BlockSpec(shape, lambda *_: zeros, pipeline_mode=pl.Buffered(1))


def _row_tile(rows, largest=512):
    for tm in (largest, 512, 256, 128):
        if rows % tm == 0:
            return tm
    raise ValueError(f"row count {rows} is not a multiple of 128")


def _inproj_prompt(x, g, w_cc, w_qkvt, wc, seq):
    rows, d_model = x.shape
    d_conv = wc.shape[1]
    tm = _row_tile(seq, largest=INPROJ_ROWS)
    tiles_per_seq = seq // tm
    n_seq = rows // seq
    row_blk = lambda width: pl.BlockSpec((tm, width), lambda b, j: (b * tiles_per_seq + j, 0))
    col_blk = pl.BlockSpec((1, D_ATTN, tm), lambda b, j: (b, 0, j))
    limit = _vmem_limit(
        [_nbytes((tm, d_model), F32), _nbytes((D_ATTN, tm), BF16), 2 * _nbytes((D_ATTN, tm), F32),
         _nbytes((tm, d_conv), BF16)],
        [_nbytes(w_cc.shape, BF16), _nbytes(w_qkvt.shape, BF16)],
        [2 * _nbytes((tm, d_model), F32), 2 * _nbytes((tm, w_cc.shape[1]), F32),
         2 * _nbytes((w_qkvt.shape[0], tm), F32), 4 * _nbytes((tm, d_conv), F32)])
    return pl.pallas_call(
        _inproj_prompt_kernel,
        grid=(n_seq, tiles_per_seq),
        in_specs=[row_blk(d_model), _resident((1, d_model)), _resident(w_cc.shape),
                  _resident(w_qkvt.shape), _resident(wc.shape)],
        out_specs=[col_blk, col_blk, col_blk, row_blk(d_conv),
                   pl.BlockSpec((1, V7X_SUBLANES, d_conv), lambda b, j: (b, 0, 0))],
        out_shape=[jax.ShapeDtypeStruct((n_seq, D_ATTN, seq), BF16),
                   jax.ShapeDtypeStruct((n_seq, D_ATTN, seq), F32),
                   jax.ShapeDtypeStruct((n_seq, D_ATTN, seq), F32),
                   jax.ShapeDtypeStruct((rows, d_conv), BF16),
                   jax.ShapeDtypeStruct((n_seq, V7X_SUBLANES, d_conv), F32)],
        scratch_shapes=[pltpu.VMEM((tm + V7X_SUBLANES, d_conv), F32)],
        compiler_params=pltpu.CompilerParams(dimension_semantics=("arbitrary", "arbitrary"),
                                             vmem_limit_bytes=limit),
        name="inproj_prompt",
    )(x, g, w_cc, w_qkvt, wc)


def _inproj_sample(x, g, w_cc, w_qkv, wc, prev1, prev2, dec_seq):
    rows, d_model = x.shape
    d_conv = wc.shape[1]
    full = lambda shape: pl.BlockSpec(shape, lambda i: (0,) * len(shape))
    return pl.pallas_call(
        functools.partial(_inproj_sample_kernel, dec_seq=dec_seq),
        grid=(1,),
        in_specs=[full(x.shape), full((1, d_model)), full(w_cc.shape), full(w_qkv.shape),
                  full(wc.shape), full(prev1.shape), full(prev2.shape)],
        out_specs=[full((rows, D_ATTN))] * 3 + [full((rows, d_conv))] * 2,
        out_shape=[jax.ShapeDtypeStruct((rows, D_ATTN), F32)] * 3
        + [jax.ShapeDtypeStruct((rows, d_conv), BF16), jax.ShapeDtypeStruct((rows, d_conv), F32)],
        scratch_shapes=[pltpu.VMEM((rows + V7X_SUBLANES, d_conv), F32)],
        compiler_params=pltpu.CompilerParams(dimension_semantics=("arbitrary",)),
        name="inproj_sample",
    )(x, g, w_cc, w_qkv, wc, prev1, prev2)


def _attn_prompt(slopes, qt, kt, vt):
    n_seq, _, seq = qt.shape
    rows = n_seq * seq
    o_blk = pl.BlockSpec((seq, V7X_LANES), lambda p, b: (b, p))
    kv_blk = pl.BlockSpec((1, V7X_LANES, seq), lambda p, b: (b, p, 0))
    dist, logc = _attn_tables(seq)
    table = _nbytes((TQ, seq), F32)
    limit = _vmem_limit(
        [2 * _nbytes((seq, V7X_LANES), F32), 2 * _nbytes((seq, V7X_LANES), BF16)],
        [2 * _nbytes((seq, V7X_LANES), BF16), (2 + HEADS_PER_STEP) * table],
        [3 * HEADS_PER_STEP * table])
    return pl.pallas_call(
        _attn_prompt_kernel,
        grid=(N_HEADS // HEADS_PER_STEP, n_seq),
        in_specs=[pl.BlockSpec(memory_space=pltpu.SMEM), _resident((seq, TQ)),
                  _resident((seq, TQ)), kv_blk, kv_blk, kv_blk],
        out_specs=o_blk,
        out_shape=jax.ShapeDtypeStruct((rows, D_ATTN), BF16),
        scratch_shapes=[pltpu.VMEM((seq, V7X_LANES), BF16), pltpu.VMEM((V7X_LANES, seq), BF16),
                        pltpu.VMEM((HEADS_PER_STEP, seq, TQ), F32)],
        compiler_params=pltpu.CompilerParams(dimension_semantics=("arbitrary", "arbitrary"),
                                             vmem_limit_bytes=limit),
        name="attn_prompt",
    )(slopes, dist, logc, qt, kt, vt)


def _attn_sample(slopes, q, kn, vn, kst, vst):
    n_seq, dec_seq, width = q.shape
    n_past = kst.shape[2]
    new_blk = pl.BlockSpec((1, dec_seq, width), lambda b: (b, 0, 0))
    past_blk = pl.BlockSpec((1, width, n_past), lambda b: (b, 0, 0))
    limit = _vmem_limit(
        [2 * _nbytes((n_past, width), F32)], [],
        [2 * _nbytes((n_past, width), BF16), 8 * _nbytes((dec_seq * N_HEADS, n_past), F32)])
    return pl.pallas_call(
        _attn_sample_kernel,
        grid=(n_seq,),
        in_specs=[pl.BlockSpec(memory_space=pltpu.SMEM), new_blk, new_blk, new_blk,
                  past_blk, past_blk],
        out_specs=new_blk,
        out_shape=jax.ShapeDtypeStruct((n_seq, dec_seq, width), F32),
        scratch_shapes=[pltpu.VMEM((V7X_LANES, width), F32)] * 2,
        compiler_params=pltpu.CompilerParams(dimension_semantics=("arbitrary",),
                                             vmem_limit_bytes=limit),
        name="attn_sample",
    )(slopes, q, kn, vn, kst, vst)


def _ffn(x, attn_o, conv_o, wo, g2, wg, wu, wd, gf, final_norm, tm):
    rows, d_model = x.shape
    d_ff = wg.shape[1]
    row_blk = lambda width: pl.BlockSpec((tm, width), lambda i: (i, 0))
    limit = _vmem_limit(
        [2 * _nbytes((tm, d_model), F32), _nbytes((tm, attn_o.shape[1]), attn_o.dtype),
         _nbytes((tm, conv_o.shape[1]), conv_o.dtype)],
        [_nbytes(w.shape, BF16) for w in (wo, wg, wu, wd)],
        [4 * _nbytes((tm, d_model), F32), 3 * _nbytes((tm, d_ff), F32)])
    return pl.pallas_call(
        functools.partial(_ffn_kernel, final_norm=final_norm),
        grid=(rows // tm,),
        in_specs=[row_blk(d_model), row_blk(attn_o.shape[1]), row_blk(conv_o.shape[1]),
                  _resident(wo.shape), _resident((1, d_model)), _resident(wg.shape),
                  _resident(wu.shape), _resident(wd.shape), _resident((1, d_model))],
        out_specs=row_blk(d_model),
        out_shape=jax.ShapeDtypeStruct((rows, d_model), F32),
        compiler_params=pltpu.CompilerParams(dimension_semantics=("arbitrary",),
                                             vmem_limit_bytes=limit),
        name="ffn",
    )(x, attn_o, conv_o, wo, g2, wg, wu, wd, gf)


def _alibi_slopes():
    return jnp.exp2(-8.0 * jnp.arange(1, N_HEADS + 1, dtype=F32) / N_HEADS)


def kernel(x_prompt, x_sample, state_attn_k, state_attn_v, state_conv, norm_mix_g, w_in, w_conv,
           w_out, norm_ffn_g, w_gate, w_up, w_down, norm_final_g):
    depth = w_in.shape[0]
    batch, seq, d_model = x_prompt.shape
    dec_batch, dec_seq, _ = x_sample.shape
    d_conv = w_conv.shape[2]
    n_past = state_attn_k.shape[2]
    assert n_past >= max(WINDOWS) and seq <= max(WINDOWS) and seq % TQ == 0
    assert CONV_WIDTH - 1 <= dec_seq <= V7X_LANES
    slopes = _alibi_slopes()
    gf = norm_final_g.reshape(1, d_model)

    yp = x_prompt.reshape(batch * seq, d_model)
    ys = x_sample.reshape(dec_batch * dec_seq, d_model)
    outs = [[] for _ in range(6)]
    for layer in range(depth):
        g1 = norm_mix_g[layer].reshape(1, d_model)
        g2 = norm_ffn_g[layer].reshape(1, d_model)
        w_in_bf = w_in[layer].astype(BF16)
        w_qkv = w_in_bf[:, :3 * D_ATTN]
        w_cc = w_in_bf[:, 3 * D_ATTN:]
        weights = [w.astype(BF16) for w in (w_out[layer], w_gate[layer], w_up[layer], w_down[layer])]
        wc = w_conv[layer]

        qt, kt, vt, conv_o, u_last = _inproj_prompt(yp, g1, w_cc, w_qkv.T, wc, seq)
        attn_o = _attn_prompt(slopes, qt, kt, vt)
        y_layer = _ffn(yp, attn_o, conv_o, weights[0], g2, *weights[1:], gf,
                       layer == depth - 1, _row_tile(seq, largest=FFN_ROWS))
        to_heads = lambda a: a.reshape(batch, N_HEADS, HEAD_DIM, seq).transpose(0, 3, 1, 2)
        outs[0].append(to_heads(kt))
        outs[1].append(to_heads(vt))
        outs[2].append(u_last[:, V7X_SUBLANES - (CONV_WIDTH - 1):, :])
        yp = y_layer

        st = state_conv[layer]
        prev1 = jnp.pad(st[:, 1:2], ((0, 0), (0, dec_seq - 1), (0, 0)))
        prev2 = jnp.pad(st, ((0, 0), (0, dec_seq - 2), (0, 0)))
        qs, ks, vs, conv_s, u_s = _inproj_sample(
            ys, g1, w_cc, w_qkv, wc, prev1.reshape(-1, d_conv), prev2.reshape(-1, d_conv), dec_seq)
        as3 = lambda a: a.reshape(dec_batch, dec_seq, D_ATTN)
        feature_major = lambda a: a.transpose(0, 2, 3, 1).reshape(dec_batch, D_ATTN, n_past)
        attn_s = _attn_sample(slopes, as3(qs), as3(ks), as3(vs),
                              feature_major(state_attn_k[layer]),
                              feature_major(state_attn_v[layer]))
        ys = _ffn(ys, attn_s.reshape(-1, D_ATTN), conv_s, weights[0], g2, *weights[1:], gf,
                  layer == depth - 1, ys.shape[0])
        outs[3].append(ks.reshape(dec_batch, dec_seq, N_HEADS, HEAD_DIM))
        outs[4].append(vs.reshape(dec_batch, dec_seq, N_HEADS, HEAD_DIM))
        outs[5].append(u_s.reshape(dec_batch, dec_seq, d_conv)[:, dec_seq - (CONV_WIDTH - 1):])

    y_prompt = yp.reshape(batch, seq, d_model)
    y_sample = ys.reshape(dec_batch, dec_seq, d_model)
    new_k_p, new_v_p, new_c_p, new_k_s, new_v_s, new_c_s = [jnp.stack(o) for o in outs]
    return (y_prompt, y_sample, new_k_p, new_v_p, new_c_p, new_k_s, new_v_s, new_c_s)
```

```python
import functools

import jax
import jax.numpy as jnp
import numpy as np
from jax import lax
from jax.experimental import pallas as pl
from jax.experimental.pallas import tpu as pltpu

F32 = jnp.float32
BF16 = jnp.bfloat16

HEAD_DIM = 64
N_HEADS = 8
D_ATTN = N_HEADS * HEAD_DIM
CONV_WIDTH = 3
WINDOWS = (128, 512, 2048)
DILATIONS = (1, 4, 16)
RMS_EPS = 1e-6
ATTN_SCALE = HEAD_DIM ** -0.5
LOG2_E = 1.4426950408889634
NEG_INF = -1e30

V7X_LANES = 128
V7X_SUBLANES = 8
V7X_VMEM_BYTES = 64 * 1024 * 1024

HEADS_PER_STEP = V7X_LANES // HEAD_DIM
TQ = 256
TK = 1024
ATTN_STREAMS = 4
ATTN_LOOKAHEAD = 1
ROW_GROUPS = 4
FFN_ROWS = 1024
INPROJ_ROW_GROUPS = 8
INPROJ_ROWS = 2048


def _rms(x, g):
    y = x * lax.rsqrt(jnp.mean(x * x, axis=-1, keepdims=True) + RMS_EPS)
    return y * g


def _branch_count(delta):
    nonneg = delta >= 0
    c = jnp.zeros(delta.shape, F32)
    for w, d in zip(WINDOWS, DILATIONS):
        hit = nonneg & ((delta & (d - 1)) == 0) & (delta <= w)
        c = c + hit.astype(F32)
    return c


def _dot_nt(a, b):
    return lax.dot_general(a, b, (((1,), (1,)), ((), ())), preferred_element_type=F32)


def _project(x, g, wcc_ref):
    d_conv = wcc_ref.shape[1] // 3
    xn = _rms(x, g).astype(BF16)
    zc = jnp.dot(xn, wcc_ref[...], preferred_element_type=F32)
    return xn, zc[:, 0:d_conv], zc[:, d_conv:2 * d_conv], zc[:, 2 * d_conv:3 * d_conv]


def _conv3(wc, u2, u1, u0):
    acc = wc[0:1, :] * u2
    acc = acc + wc[1:2, :] * u1
    return acc + wc[2:3, :] * u0


def _inproj_prompt_kernel(x_ref, g_ref, wcc_ref, wqkvt_ref, wc_ref, qt_ref, kt_ref, vt_ref, co_ref,
                          ulast_ref, uext_ref):
    tm = x_ref.shape[0]
    j = pl.program_id(1)
    tiles_per_seq = pl.num_programs(1)

    @pl.when(j == 0)
    def _():
        uext_ref[0:V7X_SUBLANES, :] = jnp.zeros((V7X_SUBLANES, uext_ref.shape[1]), F32)

    rg = tm // INPROJ_ROW_GROUPS
    for r0 in range(0, tm, rg):
        xn, hc, gb, gc = _project(x_ref[r0:r0 + rg, :], g_ref[...], wcc_ref)
        qkvt = _dot_nt(wqkvt_ref[...], xn)
        qt_ref[0, :, r0:r0 + rg] = (qkvt[0:D_ATTN, :] * (ATTN_SCALE * LOG2_E)).astype(qt_ref.dtype)
        kt_ref[0, :, r0:r0 + rg] = qkvt[D_ATTN:2 * D_ATTN, :]
        vt_ref[0, :, r0:r0 + rg] = qkvt[2 * D_ATTN:3 * D_ATTN, :]
        u = gc * hc
        uext_ref[V7X_SUBLANES:V7X_SUBLANES + rg, :] = u
        u1 = uext_ref[V7X_SUBLANES - 1:V7X_SUBLANES - 1 + rg, :]
        u2 = uext_ref[V7X_SUBLANES - 2:V7X_SUBLANES - 2 + rg, :]
        co_ref[r0:r0 + rg, :] = (gb * _conv3(wc_ref[...], u2, u1, u)).astype(co_ref.dtype)
        tail = u[rg - V7X_SUBLANES:rg, :]
        uext_ref[0:V7X_SUBLANES, :] = tail

    @pl.when(j == tiles_per_seq - 1)
    def _():
        ulast_ref[0] = tail


def _inproj_sample_kernel(x_ref, g_ref, wcc_ref, wqkv_ref, wc_ref, p1_ref, p2_ref, q_ref, k_ref,
                          v_ref, co_ref, u_ref, uext_ref, *, dec_seq):
    rows = x_ref.shape[0]
    xn, hc, gb, gc = _project(x_ref[...], g_ref[...], wcc_ref)
    qkv = jnp.dot(xn, wqkv_ref[...], preferred_element_type=F32)
    q_ref[...] = qkv[:, 0:D_ATTN] * ATTN_SCALE
    k_ref[...] = qkv[:, D_ATTN:2 * D_ATTN]
    v_ref[...] = qkv[:, 2 * D_ATTN:3 * D_ATTN]
    u = gc * hc
    u_ref[...] = u
    uext_ref[0:V7X_SUBLANES, :] = jnp.zeros((V7X_SUBLANES, u.shape[1]), F32)
    uext_ref[V7X_SUBLANES:V7X_SUBLANES + rows, :] = u
    t = lax.broadcasted_iota(jnp.int32, u.shape, 0) % dec_seq
    u1 = jnp.where(t >= 1, uext_ref[V7X_SUBLANES - 1:V7X_SUBLANES - 1 + rows, :], p1_ref[...])
    u2 = jnp.where(t >= 2, uext_ref[V7X_SUBLANES - 2:V7X_SUBLANES - 2 + rows, :], p2_ref[...])
    co_ref[...] = (gb * _conv3(wc_ref[...], u2, u1, u)).astype(co_ref.dtype)


def _attn_tables(seq):
    x = np.arange(seq, dtype=np.int32)[:, None]
    i = np.arange(TQ, dtype=np.int32)[None, :]
    delta = i + (seq - TQ) - x
    c = np.zeros(delta.shape, np.float32)
    for w, d in zip(WINDOWS, DILATIONS):
        c += (delta >= 0) & (delta % d == 0) & (delta <= w)
    logc = np.where(c > 0.0, np.log2(np.maximum(c, 1.0)), NEG_INF).astype(np.float32)
    return delta.astype(np.float32), logc


def _attn_prompt_kernel(slopes_ref, dist_ref, logc_ref, qt_ref, kt_ref, vt_ref, o_ref,
                        kb_ref, vtb_ref, bm_ref):
    seq = qt_ref.shape[2]
    pair = pl.program_id(0)
    feature = lax.broadcasted_iota(jnp.int32, (V7X_LANES, 1), 0)
    first_head = feature < HEAD_DIM

    @pl.when(pl.program_id(1) == 0)
    def _():
        for h in range(HEADS_PER_STEP):
            slope = slopes_ref[HEADS_PER_STEP * pair + h]
            bm_ref[h] = logc_ref[...] - (slope * LOG2_E) * dist_ref[...]

    kb_ref[...] = kt_ref[0].T.astype(BF16)
    vtb_ref[...] = vt_ref[0].astype(BF16)

    def score_half(st, chunk):
        k0, n = chunk
        h, row0 = st["h"], seq - (st["q0"] + TQ)
        s = (jnp.dot(kb_ref[k0:k0 + n, :], st["q"], preferred_element_type=F32)
             + bm_ref[h, row0 + k0:row0 + k0 + n, :])
        c8 = jnp.max(s.reshape(n // V7X_SUBLANES, V7X_SUBLANES, TQ), axis=0)
        m_new = jnp.maximum(st["m"], jnp.max(c8, axis=0, keepdims=True))
        pending = dict(k0=k0, n=n, s=s, m=m_new, alpha=jnp.exp2(st["m"] - m_new))
        st["m"] = m_new
        return pending

    def value_half(st, pending):
        h, k0, n = st["h"], pending["k0"], pending["n"]
        p = jnp.exp2(pending["s"] - pending["m"])
        st["l8"] = (pending["alpha"] * st["l8"]
                    + jnp.sum(p.reshape(n // V7X_SUBLANES, V7X_SUBLANES, TQ), axis=0))
        vt_h = vtb_ref[h * HEAD_DIM:(h + 1) * HEAD_DIM, k0:k0 + n]
        st["acc"] = pending["alpha"] * st["acc"] + jnp.dot(vt_h, p.astype(BF16),
                                                           preferred_element_type=F32)

    def open_block(q0):
        qt = qt_ref[0, :, q0:q0 + TQ]
        zero = jnp.zeros_like(qt)
        chunks = [(q0, TQ)] + [(max(k1 - TK, 0), min(TK, k1)) for k1 in range(q0, 0, -TK)]
        block = dict(q0=q0, left=HEADS_PER_STEP * len(chunks))
        block["streams"] = [
            dict(block=block, h=h, q0=q0, q=qh, chunks=chunks, m=jnp.full((1, TQ), NEG_INF, F32),
                 l8=jnp.zeros((V7X_SUBLANES, TQ), F32), acc=jnp.zeros((HEAD_DIM, TQ), F32))
            for h, qh in enumerate([jnp.where(first_head, qt, zero), jnp.where(first_head, zero, qt)])]
        return block

    def retire(st, pending):
        value_half(st, pending)
        block = st["block"]
        block["left"] -= 1
        if block["left"] == 0:
            outs = [s["acc"] / jnp.sum(s["l8"], axis=0, keepdims=True) for s in block["streams"]]
            q0 = block["q0"]
            o_ref[q0:q0 + TQ, :] = jnp.concatenate(outs, axis=0).T.astype(o_ref.dtype)

    blocks_per_group = ATTN_STREAMS // HEADS_PER_STEP
    waiting = []
    for g0 in range(0, seq, TQ * blocks_per_group):
        streams = [st for q0 in range(g0, g0 + TQ * blocks_per_group, TQ)
                   for st in open_block(q0)["streams"]]
        for t in range(max(len(st["chunks"]) for st in streams)):
            live = [(st, st["chunks"][t]) for st in streams if t < len(st["chunks"])]
            due = waiting.pop(0) if len(waiting) >= ATTN_LOOKAHEAD else []
            issued = []
            for i, (st, chunk) in enumerate(live):
                issued.append((st, score_half(st, chunk)))
                if i < len(due):
                    retire(*due[i])
            for st, pending in due[len(live):]:
                retire(st, pending)
            waiting.append(issued)
    for step in waiting:
        for st, pending in step:
            retire(st, pending)


def _attn_sample_kernel(slopes_ref, q_ref, kn_ref, vn_ref, kst_ref, vst_ref, o_ref,
                        knew_ref, vnew_ref):
    dec_seq = q_ref.shape[1]
    n_past = kst_ref.shape[2]
    width = q_ref.shape[2]
    n_rows = dec_seq * N_HEADS

    q = q_ref[0]
    head_of_lane = lax.broadcasted_iota(jnp.int32, (N_HEADS, width), 1) // HEAD_DIM
    head_of_row = lax.broadcasted_iota(jnp.int32, (N_HEADS, width), 0)
    head_mask = head_of_lane == head_of_row
    wt = jnp.concatenate(
        [jnp.where(head_mask, jnp.broadcast_to(q[t:t + 1, :], (N_HEADS, width)), 0.0)
         for t in range(dec_seq)], axis=0).astype(BF16)

    knew_ref[...] = jnp.zeros(knew_ref.shape, F32)
    vnew_ref[...] = jnp.zeros(vnew_ref.shape, F32)
    knew_ref[0:dec_seq, :] = kn_ref[0]
    vnew_ref[0:dec_seq, :] = vn_ref[0]

    s_past = jnp.dot(wt, kst_ref[0].astype(BF16), preferred_element_type=F32)
    s_new = _dot_nt(wt, knew_ref[...].astype(BF16))

    row = lax.broadcasted_iota(jnp.int32, (n_rows, 1), 0)
    step = row // N_HEADS
    slope_col = jnp.zeros((n_rows, 1), F32)
    for h in range(N_HEADS):
        slope_col = jnp.where(row % N_HEADS == h, slopes_ref[h], slope_col)

    def weigh(s, key_pos):
        delta = n_past + step - key_pos
        c = _branch_count(delta)
        s = jnp.where(c > 0.0, s - slope_col * delta.astype(F32), NEG_INF)
        return s, c

    s_past, c_past = weigh(s_past, lax.broadcasted_iota(jnp.int32, (1, n_past), 1))
    s_new, c_new = weigh(s_new, n_past + lax.broadcasted_iota(jnp.int32, (1, V7X_LANES), 1))
    m = jnp.maximum(jnp.max(s_past, axis=-1, keepdims=True),
                    jnp.max(s_new, axis=-1, keepdims=True))
    p_past = c_past * jnp.exp(s_past - m)
    p_new = c_new * jnp.exp(s_new - m)
    l = jnp.sum(p_past, axis=-1, keepdims=True) + jnp.sum(p_new, axis=-1, keepdims=True)
    o_all = (_dot_nt(p_past.astype(BF16), vst_ref[0].astype(BF16))
             + jnp.dot(p_new.astype(BF16), vnew_ref[...].astype(BF16),
                       preferred_element_type=F32)) / l
    outs = []
    for t in range(dec_seq):
        blk = o_all[t * N_HEADS:(t + 1) * N_HEADS, :]
        outs.append(jnp.sum(jnp.where(head_mask, blk, 0.0), axis=0, keepdims=True))
    o_ref[0] = jnp.concatenate(outs, axis=0).astype(o_ref.dtype)


def _ffn_kernel(x_ref, a_ref, c_ref, wo_ref, g2_ref, wg_ref, wu_ref, wd_ref, gf_ref, y_ref, *,
                final_norm):
    tm = x_ref.shape[0]
    rg = tm // ROW_GROUPS if tm % (ROW_GROUPS * V7X_LANES) == 0 else tm
    groups = [slice(r0, r0 + rg) for r0 in range(0, tm, rg)]
    hs = []
    for rows in groups:
        mix = jnp.concatenate([a_ref[rows, :].astype(BF16), c_ref[rows, :].astype(BF16)], axis=-1)
        hs.append(x_ref[rows, :] + jnp.dot(mix, wo_ref[...], preferred_element_type=F32))
    acts = []
    for h in hs:
        hn = _rms(h, g2_ref[...]).astype(BF16)
        gate = jnp.dot(hn, wg_ref[...], preferred_element_type=F32)
        up = jnp.dot(hn, wu_ref[...], preferred_element_type=F32)
        acts.append(((gate * (1.0 / (1.0 + jnp.exp(-gate)))) * up).astype(BF16))
    for rows, h, act in zip(groups, hs, acts):
        y = h + jnp.dot(act, wd_ref[...], preferred_element_type=F32)
        y_ref[rows, :] = _rms(y, gf_ref[...]) if final_norm else y


def _nbytes(shape, dtype):
    n = 1
    for s in shape:
        n *= s
    return n * jnp.dtype(dtype).itemsize


def _vmem_limit(pipelined, resident, temporaries):
    need = 2 * sum(pipelined) + sum(resident) + sum(temporaries)
    return min(V7X_VMEM_BYTES, need + need // 4)


def _resident(shape):
    zeros = (0,) * len(shape)
    return pl.BlockSpec(shape, lambda *_: zeros, pipeline_mode=pl.Buffered(1))


def _row_tile(rows, largest=512):
    for tm in (largest, 512, 256, 128):
        if rows % tm == 0:
            return tm
    raise ValueError(f"row count {rows} is not a multiple of 128")


def _inproj_prompt(x, g, w_cc, w_qkvt, wc, seq):
    rows, d_model = x.shape
    d_conv = wc.shape[1]
    tm = _row_tile(seq, largest=INPROJ_ROWS)
    tiles_per_seq = seq // tm
    n_seq = rows // seq
    row_blk = lambda width: pl.BlockSpec((tm, width), lambda b, j: (b * tiles_per_seq + j, 0))
    col_blk = pl.BlockSpec((1, D_ATTN, tm), lambda b, j: (b, 0, j))
    limit = _vmem_limit(
        [_nbytes((tm, d_model), F32), _nbytes((D_ATTN, tm), BF16), 2 * _nbytes((D_ATTN, tm), F32),
         _nbytes((tm, d_conv), BF16)],
        [_nbytes(w_cc.shape, BF16), _nbytes(w_qkvt.shape, BF16)],
        [2 * _nbytes((tm, d_model), F32), 2 * _nbytes((tm, w_cc.shape[1]), F32),
         2 * _nbytes((w_qkvt.shape[0], tm), F32), 4 * _nbytes((tm, d_conv), F32)])
    return pl.pallas_call(
        _inproj_prompt_kernel,
        grid=(n_seq, tiles_per_seq),
        in_specs=[row_blk(d_model), _resident((1, d_model)), _resident(w_cc.shape),
                  _resident(w_qkvt.shape), _resident(wc.shape)],
        out_specs=[col_blk, col_blk, col_blk, row_blk(d_conv),
                   pl.BlockSpec((1, V7X_SUBLANES, d_conv), lambda b, j: (b, 0, 0))],
        out_shape=[jax.ShapeDtypeStruct((n_seq, D_ATTN, seq), BF16),
                   jax.ShapeDtypeStruct((n_seq, D_ATTN, seq), F32),
                   jax.ShapeDtypeStruct((n_seq, D_ATTN, seq), F32),
                   jax.ShapeDtypeStruct((rows, d_conv), BF16),
                   jax.ShapeDtypeStruct((n_seq, V7X_SUBLANES, d_conv), F32)],
        scratch_shapes=[pltpu.VMEM((tm + V7X_SUBLANES, d_conv), F32)],
        compiler_params=pltpu.CompilerParams(dimension_semantics=("arbitrary", "arbitrary"),
                                             vmem_limit_bytes=limit),
        name="inproj_prompt",
    )(x, g, w_cc, w_qkvt, wc)


def _inproj_sample(x, g, w_cc, w_qkv, wc, prev1, prev2, dec_seq):
    rows, d_model = x.shape
    d_conv = wc.shape[1]
    full = lambda shape: pl.BlockSpec(shape, lambda i: (0,) * len(shape))
    return pl.pallas_call(
        functools.partial(_inproj_sample_kernel, dec_seq=dec_seq),
        grid=(1,),
        in_specs=[full(x.shape), full((1, d_model)), full(w_cc.shape), full(w_qkv.shape),
                  full(wc.shape), full(prev1.shape), full(prev2.shape)],
        out_specs=[full((rows, D_ATTN))] * 3 + [full((rows, d_conv))] * 2,
        out_shape=[jax.ShapeDtypeStruct((rows, D_ATTN), F32)] * 3
        + [jax.ShapeDtypeStruct((rows, d_conv), BF16), jax.ShapeDtypeStruct((rows, d_conv), F32)],
        scratch_shapes=[pltpu.VMEM((rows + V7X_SUBLANES, d_conv), F32)],
        compiler_params=pltpu.CompilerParams(dimension_semantics=("arbitrary",)),
        name="inproj_sample",
    )(x, g, w_cc, w_qkv, wc, prev1, prev2)


def _attn_prompt(slopes, qt, kt, vt):
    n_seq, _, seq = qt.shape
    rows = n_seq * seq
    o_blk = pl.BlockSpec((seq, V7X_LANES), lambda p, b: (b, p))
    kv_blk = pl.BlockSpec((1, V7X_LANES, seq), lambda p, b: (b, p, 0))
    dist, logc = _attn_tables(seq)
    table = _nbytes((TQ, seq), F32)
    limit = _vmem_limit(
        [2 * _nbytes((seq, V7X_LANES), F32), 2 * _nbytes((seq, V7X_LANES), BF16)],
        [2 * _nbytes((seq, V7X_LANES), BF16), (2 + HEADS_PER_STEP) * table],
        [3 * HEADS_PER_STEP * table])
    return pl.pallas_call(
        _attn_prompt_kernel,
        grid=(N_HEADS // HEADS_PER_STEP, n_seq),
        in_specs=[pl.BlockSpec(memory_space=pltpu.SMEM), _resident((seq, TQ)),
                  _resident((seq, TQ)), kv_blk, kv_blk, kv_blk],
        out_specs=o_blk,
        out_shape=jax.ShapeDtypeStruct((rows, D_ATTN), BF16),
        scratch_shapes=[pltpu.VMEM((seq, V7X_LANES), BF16), pltpu.VMEM((V7X_LANES, seq), BF16),
                        pltpu.VMEM((HEADS_PER_STEP, seq, TQ), F32)],
        compiler_params=pltpu.CompilerParams(dimension_semantics=("arbitrary", "arbitrary"),
                                             vmem_limit_bytes=limit),
        name="attn_prompt",
    )(slopes, dist, logc, qt, kt, vt)


def _attn_sample(slopes, q, kn, vn, kst, vst):
    n_seq, dec_seq, width = q.shape
    n_past = kst.shape[2]
    new_blk = pl.BlockSpec((1, dec_seq, width), lambda b: (b, 0, 0))
    past_blk = pl.BlockSpec((1, width, n_past), lambda b: (b, 0, 0))
    limit = _vmem_limit(
        [2 * _nbytes((n_past, width), F32)], [],
        [2 * _nbytes((n_past, width), BF16), 8 * _nbytes((dec_seq * N_HEADS, n_past), F32)])
    return pl.pallas_call(
        _attn_sample_kernel,
        grid=(n_seq,),
        in_specs=[pl.BlockSpec(memory_space=pltpu.SMEM), new_blk, new_blk, new_blk,
                  past_blk, past_blk],
        out_specs=new_blk,
        out_shape=jax.ShapeDtypeStruct((n_seq, dec_seq, width), F32),
        scratch_shapes=[pltpu.VMEM((V7X_LANES, width), F32)] * 2,
        compiler_params=pltpu.CompilerParams(dimension_semantics=("arbitrary",),
                                             vmem_limit_bytes=limit),
        name="attn_sample",
    )(slopes, q, kn, vn, kst, vst)


def _ffn(x, attn_o, conv_o, wo, g2, wg, wu, wd, gf, final_norm, tm):
    rows, d_model = x.shape
    d_ff = wg.shape[1]
    row_blk = lambda width: pl.BlockSpec((tm, width), lambda i: (i, 0))
    limit = _vmem_limit(
        [2 * _nbytes((tm, d_model), F32), _nbytes((tm, attn_o.shape[1]), attn_o.dtype),
         _nbytes((tm, conv_o.shape[1]), conv_o.dtype)],
        [_nbytes(w.shape, BF16) for w in (wo, wg, wu, wd)],
        [4 * _nbytes((tm, d_model), F32), 3 * _nbytes((tm, d_ff), F32)])
    return pl.pallas_call(
        functools.partial(_ffn_kernel, final_norm=final_norm),
        grid=(rows // tm,),
        in_specs=[row_blk(d_model), row_blk(attn_o.shape[1]), row_blk(conv_o.shape[1]),
                  _resident(wo.shape), _resident((1, d_model)), _resident(wg.shape),
                  _resident(wu.shape), _resident(wd.shape), _resident((1, d_model))],
        out_specs=row_blk(d_model),
        out_shape=jax.ShapeDtypeStruct((rows, d_model), F32),
        compiler_params=pltpu.CompilerParams(dimension_semantics=("arbitrary",),
                                             vmem_limit_bytes=limit),
        name="ffn",
    )(x, attn_o, conv_o, wo, g2, wg, wu, wd, gf)


def _alibi_slopes():
    return jnp.exp2(-8.0 * jnp.arange(1, N_HEADS + 1, dtype=F32) / N_HEADS)


def kernel(x_prompt, x_sample, state_attn_k, state_attn_v, state_conv, norm_mix_g, w_in, w_conv,
           w_out, norm_ffn_g, w_gate, w_up, w_down, norm_final_g):
    depth = w_in.shape[0]
    batch, seq, d_model = x_prompt.shape
    dec_batch, dec_seq, _ = x_sample.shape
    d_conv = w_conv.shape[2]
    n_past = state_attn_k.shape[2]
    assert n_past >= max(WINDOWS) and seq <= max(WINDOWS) and seq % TQ == 0
    assert CONV_WIDTH - 1 <= dec_seq <= V7X_LANES
    slopes = _alibi_slopes()
    gf = norm_final_g.reshape(1, d_model)

    yp = x_prompt.reshape(batch * seq, d_model)
    ys = x_sample.reshape(dec_batch * dec_seq, d_model)
    outs = [[] for _ in range(6)]
    for layer in range(depth):
        g1 = norm_mix_g[layer].reshape(1, d_model)
        g2 = norm_ffn_g[layer].reshape(1, d_model)
        w_in_bf = w_in[layer].astype(BF16)
        w_qkv = w_in_bf[:, :3 * D_ATTN]
        w_cc = w_in_bf[:, 3 * D_ATTN:]
        weights = [w.astype(BF16) for w in (w_out[layer], w_gate[layer], w_up[layer], w_down[layer])]
        wc = w_conv[layer]

        qt, kt, vt, conv_o, u_last = _inproj_prompt(yp, g1, w_cc, w_qkv.T, wc, seq)
        attn_o = _attn_prompt(slopes, qt, kt, vt)
        y_layer = _ffn(yp, attn_o, conv_o, weights[0], g2, *weights[1:], gf,
                       layer == depth - 1, _row_tile(seq, largest=FFN_ROWS))
        to_heads = lambda a: a.reshape(batch, N_HEADS, HEAD_DIM, seq).transpose(0, 3, 1, 2)
        outs[0].append(to_heads(kt))
        outs[1].append(to_heads(vt))
        outs[2].append(u_last[:, V7X_SUBLANES - (CONV_WIDTH - 1):, :])
        yp = y_layer

        st = state_conv[layer]
        prev1 = jnp.pad(st[:, 1:2], ((0, 0), (0, dec_seq - 1), (0, 0)))
        prev2 = jnp.pad(st, ((0, 0), (0, dec_seq - 2), (0, 0)))
        qs, ks, vs, conv_s, u_s = _inproj_sample(
            ys, g1, w_cc, w_qkv, wc, prev1.reshape(-1, d_conv), prev2.reshape(-1, d_conv), dec_seq)
        as3 = lambda a: a.reshape(dec_batch, dec_seq, D_ATTN)
        feature_major = lambda a: a.transpose(0, 2, 3, 1).reshape(dec_batch, D_ATTN, n_past)
        attn_s = _attn_sample(slopes, as3(qs), as3(ks), as3(vs),
                              feature_major(state_attn_k[layer]),
                              feature_major(state_attn_v[layer]))
        ys = _ffn(ys, attn_s.reshape(-1, D_ATTN), conv_s, weights[0], g2, *weights[1:], gf,
                  layer == depth - 1, ys.shape[0])
        outs[3].append(ks.reshape(dec_batch, dec_seq, N_HEADS, HEAD_DIM))
        outs[4].append(vs.reshape(dec_batch, dec_seq, N_HEADS, HEAD_DIM))
        outs[5].append(u_s.reshape(dec_batch, dec_seq, d_conv)[:, dec_seq - (CONV_WIDTH - 1):])

    y_prompt = yp.reshape(batch, seq, d_model)
    y_sample = ys.reshape(dec_batch, dec_seq, d_model)
    new_k_p, new_v_p, new_c_p, new_k_s, new_v_s, new_c_s = [jnp.stack(o) for o in outs]
    return (y_prompt, y_sample, new_k_p, new_v_p, new_c_p, new_k_s, new_v_s, new_c_s)
```

```python
import functools

import jax
import jax.numpy as jnp
import numpy as np
from jax import lax
from jax.experimental import pallas as pl
from jax.experimental.pallas import tpu as pltpu

F32 = jnp.float32
BF16 = jnp.bfloat16

HEAD_DIM = 64
N_HEADS = 8
D_ATTN = N_HEADS * HEAD_DIM
CONV_WIDTH = 3
WINDOWS = (128, 512, 2048)
DILATIONS = (1, 4, 16)
RMS_EPS = 1e-6
ATTN_SCALE = HEAD_DIM ** -0.5
LOG2_E = 1.4426950408889634
NEG_INF = -1e30

V7X_LANES = 128
V7X_SUBLANES = 8
V7X_VMEM_BYTES = 64 * 1024 * 1024

HEADS_PER_STEP = V7X_LANES // HEAD_DIM
TQ = 256
TK = 512
ATTN_STREAMS = 4
ATTN_LOOKAHEAD = 1
ROW_GROUPS = 4
FFN_ROWS = 1024
INPROJ_ROW_GROUPS = 8
INPROJ_ROWS = 2048


def _rms(x, g):
    y = x * lax.rsqrt(jnp.mean(x * x, axis=-1, keepdims=True) + RMS_EPS)
    return y * g


def _branch_count(delta):
    nonneg = delta >= 0
    c = jnp.zeros(delta.shape, F32)
    for w, d in zip(WINDOWS, DILATIONS):
        hit = nonneg & ((delta & (d - 1)) == 0) & (delta <= w)
        c = c + hit.astype(F32)
    return c


def _dot_nt(a, b):
    return lax.dot_general(a, b, (((1,), (1,)), ((), ())), preferred_element_type=F32)


def _project(x, g, wcc_ref):
    d_conv = wcc_ref.shape[1] // 3
    xn = _rms(x, g).astype(BF16)
    zc = jnp.dot(xn, wcc_ref[...], preferred_element_type=F32)
    return xn, zc[:, 0:d_conv], zc[:, d_conv:2 * d_conv], zc[:, 2 * d_conv:3 * d_conv]


def _conv3(wc, u2, u1, u0):
    acc = wc[0:1, :] * u2
    acc = acc + wc[1:2, :] * u1
    return acc + wc[2:3, :] * u0


def _inproj_prompt_kernel(x_ref, g_ref, wcc_ref, wqkvt_ref, wc_ref, qt_ref, kt_ref, vt_ref, co_ref,
                          ulast_ref, uext_ref):
    tm = x_ref.shape[0]
    j = pl.program_id(1)
    tiles_per_seq = pl.num_programs(1)

    @pl.when(j == 0)
    def _():
        uext_ref[0:V7X_SUBLANES, :] = jnp.zeros((V7X_SUBLANES, uext_ref.shape[1]), F32)

    rg = tm // INPROJ_ROW_GROUPS
    for r0 in range(0, tm, rg):
        xn, hc, gb, gc = _project(x_ref[r0:r0 + rg, :], g_ref[...], wcc_ref)
        qkvt = _dot_nt(wqkvt_ref[...], xn)
        qt_ref[0, :, r0:r0 + rg] = (qkvt[0:D_ATTN, :] * (ATTN_SCALE * LOG2_E)).astype(qt_ref.dtype)
        kt_ref[0, :, r0:r0 + rg] = qkvt[D_ATTN:2 * D_ATTN, :]
        vt_ref[0, :, r0:r0 + rg] = qkvt[2 * D_ATTN:3 * D_ATTN, :]
        u = gc * hc
        uext_ref[V7X_SUBLANES:V7X_SUBLANES + rg, :] = u
        u1 = uext_ref[V7X_SUBLANES - 1:V7X_SUBLANES - 1 + rg, :]
        u2 = uext_ref[V7X_SUBLANES - 2:V7X_SUBLANES - 2 + rg, :]
        co_ref[r0:r0 + rg, :] = (gb * _conv3(wc_ref[...], u2, u1, u)).astype(co_ref.dtype)
        tail = u[rg - V7X_SUBLANES:rg, :]
        uext_ref[0:V7X_SUBLANES, :] = tail

    @pl.when(j == tiles_per_seq - 1)
    def _():
        ulast_ref[0] = tail


def _inproj_sample_kernel(x_ref, g_ref, wcc_ref, wqkvt_ref, wc_ref, p1_ref, p2_ref, q_ref, k_ref,
                          v_ref, co_ref, u_ref, uext_ref, *, dec_seq):
    rows = x_ref.shape[0]
    xn, hc, gb, gc = _project(x_ref[...], g_ref[...], wcc_ref)
    qkv = _dot_nt(xn, wqkvt_ref[...])
    q_ref[...] = qkv[:, 0:D_ATTN] * ATTN_SCALE
    k_ref[...] = qkv[:, D_ATTN:2 * D_ATTN]
    v_ref[...] = qkv[:, 2 * D_ATTN:3 * D_ATTN]
    u = gc * hc
    u_ref[...] = u
    uext_ref[0:V7X_SUBLANES, :] = jnp.zeros((V7X_SUBLANES, u.shape[1]), F32)
    uext_ref[V7X_SUBLANES:V7X_SUBLANES + rows, :] = u
    t = lax.broadcasted_iota(jnp.int32, u.shape, 0) % dec_seq
    u1 = jnp.where(t >= 1, uext_ref[V7X_SUBLANES - 1:V7X_SUBLANES - 1 + rows, :], p1_ref[...])
    u2 = jnp.where(t >= 2, uext_ref[V7X_SUBLANES - 2:V7X_SUBLANES - 2 + rows, :], p2_ref[...])
    co_ref[...] = (gb * _conv3(wc_ref[...], u2, u1, u)).astype(co_ref.dtype)


def _attn_tables(seq):
    x = np.arange(seq, dtype=np.int32)[:, None]
    i = np.arange(TQ, dtype=np.int32)[None, :]
    delta = i + (seq - TQ) - x
    c = np.zeros(delta.shape, np.float32)
    for w, d in zip(WINDOWS, DILATIONS):
        c += (delta >= 0) & (delta % d == 0) & (delta <= w)
    logc = np.where(c > 0.0, np.log2(np.maximum(c, 1.0)), NEG_INF).astype(np.float32)
    return delta.astype(np.float32), logc


def _attn_prompt_kernel(slopes_ref, dist_ref, logc_ref, qt_ref, kt_ref, vt_ref, o_ref,
                        kb_ref, vtb_ref, bm_ref):
    seq = qt_ref.shape[2]
    pair = pl.program_id(0)
    feature = lax.broadcasted_iota(jnp.int32, (V7X_LANES, 1), 0)
    first_head = feature < HEAD_DIM

    @pl.when(pl.program_id(1) == 0)
    def _():
        for h in range(HEADS_PER_STEP):
            slope = slopes_ref[HEADS_PER_STEP * pair + h]
            bm_ref[h] = logc_ref[...] - (slope * LOG2_E) * dist_ref[...]

    kb_ref[...] = kt_ref[0].T.astype(BF16)
    vtb_ref[...] = vt_ref[0].astype(BF16)

    def score_half(st, chunk):
        k0, n = chunk
        h, row0 = st["h"], seq - (st["q0"] + TQ)
        s = (jnp.dot(kb_ref[k0:k0 + n, :], st["q"], preferred_element_type=F32)
             + bm_ref[h, row0 + k0:row0 + k0 + n, :])
        c8 = jnp.max(s.reshape(n // V7X_SUBLANES, V7X_SUBLANES, TQ), axis=0)
        m_new = jnp.maximum(st["m"], jnp.max(c8, axis=0, keepdims=True))
        pending = dict(k0=k0, n=n, s=s, m=m_new, alpha=jnp.exp2(st["m"] - m_new))
        st["m"] = m_new
        return pending

    def value_half(st, pending):
        h, k0, n = st["h"], pending["k0"], pending["n"]
        p = jnp.exp2(pending["s"] - pending["m"])
        st["l8"] = (pending["alpha"] * st["l8"]
                    + jnp.sum(p.reshape(n // V7X_SUBLANES, V7X_SUBLANES, TQ), axis=0))
        vt_h = vtb_ref[h * HEAD_DIM:(h + 1) * HEAD_DIM, k0:k0 + n]
        st["acc"] = pending["alpha"] * st["acc"] + jnp.dot(vt_h, p.astype(BF16),
                                                           preferred_element_type=F32)

    def open_block(q0):
        qt = qt_ref[0, :, q0:q0 + TQ]
        zero = jnp.zeros_like(qt)
        chunks = [(max(k1 - TK, 0), min(TK, k1)) for k1 in range(q0 + TQ, 0, -TK)]
        block = dict(q0=q0, left=HEADS_PER_STEP * len(chunks))
        block["streams"] = [
            dict(block=block, h=h, q0=q0, q=qh, chunks=chunks, m=jnp.full((1, TQ), NEG_INF, F32),
                 l8=jnp.zeros((V7X_SUBLANES, TQ), F32), acc=jnp.zeros((HEAD_DIM, TQ), F32))
            for h, qh in enumerate([jnp.where(first_head, qt, zero), jnp.where(first_head, zero, qt)])]
        return block

    def retire(st, pending):
        value_half(st, pending)
        block = st["block"]
        block["left"] -= 1
        if block["left"] == 0:
            outs = [s["acc"] / jnp.sum(s["l8"], axis=0, keepdims=True) for s in block["streams"]]
            q0 = block["q0"]
            o_ref[q0:q0 + TQ, :] = jnp.concatenate(outs, axis=0).T.astype(o_ref.dtype)

    blocks_per_group = ATTN_STREAMS // HEADS_PER_STEP
    waiting = []
    for g0 in range(0, seq, TQ * blocks_per_group):
        streams = [st for q0 in range(g0, g0 + TQ * blocks_per_group, TQ)
                   for st in open_block(q0)["streams"]]
        for t in range(max(len(st["chunks"]) for st in streams)):
            live = [(st, st["chunks"][t]) for st in streams if t < len(st["chunks"])]
            due = waiting.pop(0) if len(waiting) >= ATTN_LOOKAHEAD else []
            issued = []
            for i, (st, chunk) in enumerate(live):
                issued.append((st, score_half(st, chunk)))
                if i < len(due):
                    retire(*due[i])
            for st, pending in due[len(live):]:
                retire(st, pending)
            waiting.append(issued)
    for step in waiting:
        for st, pending in step:
            retire(st, pending)


def _attn_sample_kernel(slopes_ref, q_ref, kn_ref, vn_ref, kst_ref, vst_ref, o_ref,
                        knew_ref, vnew_ref):
    dec_seq = q_ref.shape[1]
    n_past = kst_ref.shape[2]
    width = q_ref.shape[2]
    n_rows = dec_seq * N_HEADS

    q = q_ref[0]
    head_of_lane = lax.broadcasted_iota(jnp.int32, (N_HEADS, width), 1) // HEAD_DIM
    head_of_row = lax.broadcasted_iota(jnp.int32, (N_HEADS, width), 0)
    head_mask = head_of_lane == head_of_row
    wt = jnp.concatenate(
        [jnp.where(head_mask, jnp.broadcast_to(q[t:t + 1, :], (N_HEADS, width)), 0.0)
         for t in range(dec_seq)], axis=0).astype(BF16)

    knew_ref[...] = jnp.zeros(knew_ref.shape, F32)
    vnew_ref[...] = jnp.zeros(vnew_ref.shape, F32)
    knew_ref[0:dec_seq, :] = kn_ref[0]
    vnew_ref[0:dec_seq, :] = vn_ref[0]

    s_past = jnp.dot(wt, kst_ref[0].astype(BF16), preferred_element_type=F32)
    s_new = _dot_nt(wt, knew_ref[...].astype(BF16))

    row = lax.broadcasted_iota(jnp.int32, (n_rows, 1), 0)
    step = row // N_HEADS
    slope_col = jnp.zeros((n_rows, 1), F32)
    for h in range(N_HEADS):
        slope_col = jnp.where(row % N_HEADS == h, slopes_ref[h], slope_col)

    def weigh(s, key_pos):
        delta = n_past + step - key_pos
        c = _branch_count(delta)
        s = jnp.where(c > 0.0, s - slope_col * delta.astype(F32), NEG_INF)
        return s, c

    s_past, c_past = weigh(s_past, lax.broadcasted_iota(jnp.int32, (1, n_past), 1))
    s_new, c_new = weigh(s_new, n_past + lax.broadcasted_iota(jnp.int32, (1, V7X_LANES), 1))
    m = jnp.maximum(jnp.max(s_past, axis=-1, keepdims=True),
                    jnp.max(s_new, axis=-1, keepdims=True))
    p_past = c_past * jnp.exp(s_past - m)
    p_new = c_new * jnp.exp(s_new - m)
    l = jnp.sum(p_past, axis=-1, keepdims=True) + jnp.sum(p_new, axis=-1, keepdims=True)
    o_all = (_dot_nt(p_past.astype(BF16), vst_ref[0].astype(BF16))
             + jnp.dot(p_new.astype(BF16), vnew_ref[...].astype(BF16),
                       preferred_element_type=F32)) / l
    outs = []
    for t in range(dec_seq):
        blk = o_all[t * N_HEADS:(t + 1) * N_HEADS, :]
        outs.append(jnp.sum(jnp.where(head_mask, blk, 0.0), axis=0, keepdims=True))
    o_ref[0] = jnp.concatenate(outs, axis=0).astype(o_ref.dtype)


def _ffn_kernel(x_ref, a_ref, c_ref, wo_ref, g2_ref, wg_ref, wu_ref, wd_ref, gf_ref, y_ref, *,
                final_norm):
    tm = x_ref.shape[0]
    rg = tm // ROW_GROUPS if tm % (ROW_GROUPS * V7X_LANES) == 0 else tm
    groups = [slice(r0, r0 + rg) for r0 in range(0, tm, rg)]
    hs = []
    for rows in groups:
        mix = jnp.concatenate([a_ref[rows, :].astype(BF16), c_ref[rows, :].astype(BF16)], axis=-1)
        hs.append(x_ref[rows, :] + jnp.dot(mix, wo_ref[...], preferred_element_type=F32))
    acts = []
    for h in hs:
        hn = _rms(h, g2_ref[...]).astype(BF16)
        gate = jnp.dot(hn, wg_ref[...], preferred_element_type=F32)
        up = jnp.dot(hn, wu_ref[...], preferred_element_type=F32)
        acts.append(((gate * (1.0 / (1.0 + jnp.exp(-gate)))) * up).astype(BF16))
    for rows, h, act in zip(groups, hs, acts):
        y = h + jnp.dot(act, wd_ref[...], preferred_element_type=F32)
        y_ref[rows, :] = _rms(y, gf_ref[...]) if final_norm else y


def _nbytes(shape, dtype):
    n = 1
    for s in shape:
        n *= s
    return n * jnp.dtype(dtype).itemsize


def _vmem_limit(pipelined, resident, temporaries):
    need = 2 * sum(pipelined) + sum(resident) + sum(temporaries)
    return min(V7X_VMEM_BYTES, need + need // 4)


def _resident(shape):
    zeros = (0,) * len(shape)
    return pl.BlockSpec(shape, lambda *_: zeros, pipeline_mode=pl.Buffered(1))


def _row_tile(rows, largest=512):
    for tm in (largest, 512, 256, 128):
        if rows % tm == 0:
            return tm
    raise ValueError(f"row count {rows} is not a multiple of 128")


def _inproj_prompt(x, g, w_cc, w_qkvt, wc, seq):
    rows, d_model = x.shape
    d_conv = wc.shape[1]
    tm = _row_tile(seq, largest=INPROJ_ROWS)
    tiles_per_seq = seq // tm
    n_seq = rows // seq
    row_blk = lambda width: pl.BlockSpec((tm, width), lambda b, j: (b * tiles_per_seq + j, 0))
    col_blk = pl.BlockSpec((1, D_ATTN, tm), lambda b, j: (b, 0, j))
    limit = _vmem_limit(
        [_nbytes((tm, d_model), F32), _nbytes((D_ATTN, tm), BF16), 2 * _nbytes((D_ATTN, tm), F32),
         _nbytes((tm, d_conv), BF16)],
        [_nbytes(w_cc.shape, BF16), _nbytes(w_qkvt.shape, BF16)],
        [2 * _nbytes((tm, d_model), F32), 2 * _nbytes((tm, w_cc.shape[1]), F32),
         2 * _nbytes((w_qkvt.shape[0], tm), F32), 4 * _nbytes((tm, d_conv), F32)])
    return pl.pallas_call(
        _inproj_prompt_kernel,
        grid=(n_seq, tiles_per_seq),
        in_specs=[row_blk(d_model), _resident((1, d_model)), _resident(w_cc.shape),
                  _resident(w_qkvt.shape), _resident(wc.shape)],
        out_specs=[col_blk, col_blk, col_blk, row_blk(d_conv),
                   pl.BlockSpec((1, V7X_SUBLANES, d_conv), lambda b, j: (b, 0, 0))],
        out_shape=[jax.ShapeDtypeStruct((n_seq, D_ATTN, seq), BF16),
                   jax.ShapeDtypeStruct((n_seq, D_ATTN, seq), F32),
                   jax.ShapeDtypeStruct((n_seq, D_ATTN, seq), F32),
                   jax.ShapeDtypeStruct((rows, d_conv), BF16),
                   jax.ShapeDtypeStruct((n_seq, V7X_SUBLANES, d_conv), F32)],
        scratch_shapes=[pltpu.VMEM((tm + V7X_SUBLANES, d_conv), F32)],
        compiler_params=pltpu.CompilerParams(dimension_semantics=("arbitrary", "arbitrary"),
                                             vmem_limit_bytes=limit),
        name="inproj_prompt",
    )(x, g, w_cc, w_qkvt, wc)


def _inproj_sample(x, g, w_cc, w_qkvt, wc, prev1, prev2, dec_seq):
    rows, d_model = x.shape
    d_conv = wc.shape[1]
    full = lambda shape: pl.BlockSpec(shape, lambda i: (0,) * len(shape))
    return pl.pallas_call(
        functools.partial(_inproj_sample_kernel, dec_seq=dec_seq),
        grid=(1,),
        in_specs=[full(x.shape), full((1, d_model)), full(w_cc.shape), full(w_qkvt.shape),
                  full(wc.shape), full(prev1.shape), full(prev2.shape)],
        out_specs=[full((rows, D_ATTN))] * 3 + [full((rows, d_conv))] * 2,
        out_shape=[jax.ShapeDtypeStruct((rows, D_ATTN), F32)] * 3
        + [jax.ShapeDtypeStruct((rows, d_conv), BF16), jax.ShapeDtypeStruct((rows, d_conv), F32)],
        scratch_shapes=[pltpu.VMEM((rows + V7X_SUBLANES, d_conv), F32)],
        compiler_params=pltpu.CompilerParams(dimension_semantics=("arbitrary",)),
        name="inproj_sample",
    )(x, g, w_cc, w_qkvt, wc, prev1, prev2)


def _attn_prompt(slopes, qt, kt, vt):
    n_seq, _, seq = qt.shape
    rows = n_seq * seq
    o_blk = pl.BlockSpec((seq, V7X_LANES), lambda p, b: (b, p))
    kv_blk = pl.BlockSpec((1, V7X_LANES, seq), lambda p, b: (b, p, 0))
    dist, logc = _attn_tables(seq)
    table = _nbytes((TQ, seq), F32)
    limit = _vmem_limit(
        [2 * _nbytes((seq, V7X_LANES), F32), 2 * _nbytes((seq, V7X_LANES), BF16)],
        [2 * _nbytes((seq, V7X_LANES), BF16), (2 + HEADS_PER_STEP) * table],
        [3 * HEADS_PER_STEP * table])
    return pl.pallas_call(
        _attn_prompt_kernel,
        grid=(N_HEADS // HEADS_PER_STEP, n_seq),
        in_specs=[pl.BlockSpec(memory_space=pltpu.SMEM), _resident((seq, TQ)),
                  _resident((seq, TQ)), kv_blk, kv_blk, kv_blk],
        out_specs=o_blk,
        out_shape=jax.ShapeDtypeStruct((rows, D_ATTN), BF16),
        scratch_shapes=[pltpu.VMEM((seq, V7X_LANES), BF16), pltpu.VMEM((V7X_LANES, seq), BF16),
                        pltpu.VMEM((HEADS_PER_STEP, seq, TQ), F32)],
        compiler_params=pltpu.CompilerParams(dimension_semantics=("arbitrary", "arbitrary"),
                                             vmem_limit_bytes=limit),
        name="attn_prompt",
    )(slopes, dist, logc, qt, kt, vt)


def _attn_sample(slopes, q, kn, vn, kst, vst):
    n_seq, dec_seq, width = q.shape
    n_past = kst.shape[2]
    new_blk = pl.BlockSpec((1, dec_seq, width), lambda b: (b, 0, 0))
    past_blk = pl.BlockSpec((1, width, n_past), lambda b: (b, 0, 0))
    limit = _vmem_limit(
        [2 * _nbytes((n_past, width), F32)], [],
        [2 * _nbytes((n_past, width), BF16), 8 * _nbytes((dec_seq * N_HEADS, n_past), F32)])
    return pl.pallas_call(
        _attn_sample_kernel,
        grid=(n_seq,),
        in_specs=[pl.BlockSpec(memory_space=pltpu.SMEM), new_blk, new_blk, new_blk,
                  past_blk, past_blk],
        out_specs=new_blk,
        out_shape=jax.ShapeDtypeStruct((n_seq, dec_seq, width), F32),
        scratch_shapes=[pltpu.VMEM((V7X_LANES, width), F32)] * 2,
        compiler_params=pltpu.CompilerParams(dimension_semantics=("arbitrary",),
                                             vmem_limit_bytes=limit),
        name="attn_sample",
    )(slopes, q, kn, vn, kst, vst)


def _ffn(x, attn_o, conv_o, wo, g2, wg, wu, wd, gf, final_norm, tm):
    rows, d_model = x.shape
    d_ff = wg.shape[1]
    row_blk = lambda width: pl.BlockSpec((tm, width), lambda i: (i, 0))
    limit = _vmem_limit(
        [2 * _nbytes((tm, d_model), F32), _nbytes((tm, attn_o.shape[1]), attn_o.dtype),
         _nbytes((tm, conv_o.shape[1]), conv_o.dtype)],
        [_nbytes(w.shape, BF16) for w in (wo, wg, wu, wd)],
        [4 * _nbytes((tm, d_model), F32), 3 * _nbytes((tm, d_ff), F32)])
    return pl.pallas_call(
        functools.partial(_ffn_kernel, final_norm=final_norm),
        grid=(rows // tm,),
        in_specs=[row_blk(d_model), row_blk(attn_o.shape[1]), row_blk(conv_o.shape[1]),
                  _resident(wo.shape), _resident((1, d_model)), _resident(wg.shape),
                  _resident(wu.shape), _resident(wd.shape), _resident((1, d_model))],
        out_specs=row_blk(d_model),
        out_shape=jax.ShapeDtypeStruct((rows, d_model), F32),
        compiler_params=pltpu.CompilerParams(dimension_semantics=("arbitrary",),
                                             vmem_limit_bytes=limit),
        name="ffn",
    )(x, attn_o, conv_o, wo, g2, wg, wu, wd, gf)


def _alibi_slopes():
    return jnp.exp2(-8.0 * jnp.arange(1, N_HEADS + 1, dtype=F32) / N_HEADS)


def kernel(x_prompt, x_sample, state_attn_k, state_attn_v, state_conv, norm_mix_g, w_in, w_conv,
           w_out, norm_ffn_g, w_gate, w_up, w_down, norm_final_g):
    depth = w_in.shape[0]
    batch, seq, d_model = x_prompt.shape
    dec_batch, dec_seq, _ = x_sample.shape
    d_conv = w_conv.shape[2]
    n_past = state_attn_k.shape[2]
    assert n_past >= max(WINDOWS) and seq <= max(WINDOWS) and seq % TQ == 0
    assert CONV_WIDTH - 1 <= dec_seq <= V7X_LANES
    slopes = _alibi_slopes()
    gf = norm_final_g.reshape(1, d_model)

    yp = x_prompt.reshape(batch * seq, d_model)
    ys = x_sample.reshape(dec_batch * dec_seq, d_model)
    outs = [[] for _ in range(6)]
    for layer in range(depth):
        g1 = norm_mix_g[layer].reshape(1, d_model)
        g2 = norm_ffn_g[layer].reshape(1, d_model)
        w_qkvt = w_in[layer][:, :3 * D_ATTN].T.astype(BF16)
        w_cc = w_in[layer][:, 3 * D_ATTN:].astype(BF16)
        weights = [w.astype(BF16) for w in (w_out[layer], w_gate[layer], w_up[layer], w_down[layer])]
        wc = w_conv[layer]

        qt, kt, vt, conv_o, u_last = _inproj_prompt(yp, g1, w_cc, w_qkvt, wc, seq)
        attn_o = _attn_prompt(slopes, qt, kt, vt)
        y_layer = _ffn(yp, attn_o, conv_o, weights[0], g2, *weights[1:], gf,
                       layer == depth - 1, _row_tile(seq, largest=FFN_ROWS))
        to_heads = lambda a: a.reshape(batch, N_HEADS, HEAD_DIM, seq).transpose(0, 3, 1, 2)
        outs[0].append(to_heads(kt))
        outs[1].append(to_heads(vt))
        outs[2].append(u_last[:, V7X_SUBLANES - (CONV_WIDTH - 1):, :])
        yp = y_layer

        st = state_conv[layer]
        prev1 = jnp.pad(st[:, 1:2], ((0, 0), (0, dec_seq - 1), (0, 0)))
        prev2 = jnp.pad(st, ((0, 0), (0, dec_seq - 2), (0, 0)))
        qs, ks, vs, conv_s, u_s = _inproj_sample(
            ys, g1, w_cc, w_qkvt, wc, prev1.reshape(-1, d_conv), prev2.reshape(-1, d_conv), dec_seq)
        as3 = lambda a: a.reshape(dec_batch, dec_seq, D_ATTN)
        feature_major = lambda a: a.transpose(0, 2, 3, 1).reshape(dec_batch, D_ATTN, n_past)
        attn_s = _attn_sample(slopes, as3(qs), as3(ks), as3(vs),
                              feature_major(state_attn_k[layer]),
                              feature_major(state_attn_v[layer]))
        ys = _ffn(ys, attn_s.reshape(-1, D_ATTN), conv_s, weights[0], g2, *weights[1:], gf,
                  layer == depth - 1, ys.shape[0])
        outs[3].append(ks.reshape(dec_batch, dec_seq, N_HEADS, HEAD_DIM))
        outs[4].append(vs.reshape(dec_batch, dec_seq, N_HEADS, HEAD_DIM))
        outs[5].append(u_s.reshape(dec_batch, dec_seq, d_conv)[:, dec_seq - (CONV_WIDTH - 1):])

    y_prompt = yp.reshape(batch, seq, d_model)
    y_sample = ys.reshape(dec_batch, dec_seq, d_model)
    new_k_p, new_v_p, new_c_p, new_k_s, new_v_s, new_c_s = [jnp.stack(o) for o in outs]
    return (y_prompt, y_sample, new_k_p, new_v_p, new_c_p, new_k_s, new_v_s, new_c_s)
```

```python
import functools

import jax
import jax.numpy as jnp
import numpy as np
from jax import lax
from jax.experimental import pallas as pl
from jax.experimental.pallas import tpu as pltpu

F32 = jnp.float32
BF16 = jnp.bfloat16

HEAD_DIM = 64
N_HEADS = 8
D_ATTN = N_HEADS * HEAD_DIM
CONV_WIDTH = 3
WINDOWS = (128, 512, 2048)
DILATIONS = (1, 4, 16)
RMS_EPS = 1e-6
ATTN_SCALE = HEAD_DIM ** -0.5
LOG2_E = 1.4426950408889634
NEG_INF = -1e30

V7X_LANES = 128
V7X_SUBLANES = 8
V7X_VMEM_BYTES = 64 * 1024 * 1024

HEADS_PER_STEP = V7X_LANES // HEAD_DIM
TQ = 256
TK = 512
ATTN_STREAMS = 4
ATTN_LOOKAHEAD = 1
ROW_GROUPS = 4
FFN_ROWS = 1024
INPROJ_ROW_GROUPS = 8
INPROJ_ROWS = 2048


def _rms(x, g):
    y = x * lax.rsqrt(jnp.mean(x * x, axis=-1, keepdims=True) + RMS_EPS)
    return y * g


def _branch_count(delta):
    nonneg = delta >= 0
    c = jnp.zeros(delta.shape, F32)
    for w, d in zip(WINDOWS, DILATIONS):
        hit = nonneg & ((delta & (d - 1)) == 0) & (delta <= w)
        c = c + hit.astype(F32)
    return c


def _dot_nt(a, b):
    return lax.dot_general(a, b, (((1,), (1,)), ((), ())), preferred_element_type=F32)


def _project(x, g, wcc_ref):
    d_conv = wcc_ref.shape[1] // 3
    xn = _rms(x, g).astype(BF16)
    zc = jnp.dot(xn, wcc_ref[...], preferred_element_type=F32)
    return xn, zc[:, 0:d_conv], zc[:, d_conv:2 * d_conv], zc[:, 2 * d_conv:3 * d_conv]


def _conv3(wc, u2, u1, u0):
    acc = wc[0:1, :] * u2
    acc = acc + wc[1:2, :] * u1
    return acc + wc[2:3, :] * u0


def _inproj_prompt_kernel(x_ref, g_ref, wcc_ref, wqkvt_ref, wc_ref, qt_ref, kt_ref, vt_ref, co_ref,
                          ulast_ref, uext_ref):
    tm = x_ref.shape[0]
    j = pl.program_id(1)
    tiles_per_seq = pl.num_programs(1)

    @pl.when(j == 0)
    def _():
        uext_ref[0:V7X_SUBLANES, :] = jnp.zeros((V7X_SUBLANES, uext_ref.shape[1]), F32)

    rg = tm // INPROJ_ROW_GROUPS
    for r0 in range(0, tm, rg):
        xn, hc, gb, gc = _project(x_ref[r0:r0 + rg, :], g_ref[...], wcc_ref)
        qkvt = _dot_nt(wqkvt_ref[...], xn)
        qt_ref[0, :, r0:r0 + rg] = (qkvt[0:D_ATTN, :] * (ATTN_SCALE * LOG2_E)).astype(qt_ref.dtype)
        kt_ref[0, :, r0:r0 + rg] = qkvt[D_ATTN:2 * D_ATTN, :]
        vt_ref[0, :, r0:r0 + rg] = qkvt[2 * D_ATTN:3 * D_ATTN, :]
        u = gc * hc
        uext_ref[V7X_SUBLANES:V7X_SUBLANES + rg, :] = u
        u1 = uext_ref[V7X_SUBLANES - 1:V7X_SUBLANES - 1 + rg, :]
        u2 = uext_ref[V7X_SUBLANES - 2:V7X_SUBLANES - 2 + rg, :]
        co_ref[r0:r0 + rg, :] = (gb * _conv3(wc_ref[...], u2, u1, u)).astype(co_ref.dtype)
        tail = u[rg - V7X_SUBLANES:rg, :]
        uext_ref[0:V7X_SUBLANES, :] = tail

    @pl.when(j == tiles_per_seq - 1)
    def _():
        ulast_ref[0] = tail


def _inproj_sample_kernel(x_ref, g_ref, wcc_ref, wqkv_ref, wc_ref, p1_ref, p2_ref, q_ref, k_ref,
                          v_ref, co_ref, u_ref, uext_ref, *, dec_seq):
    rows = x_ref.shape[0]
    xn, hc, gb, gc = _project(x_ref[...], g_ref[...], wcc_ref)
    qkv = jnp.dot(xn, wqkv_ref[...], preferred_element_type=F32)
    q_ref[...] = qkv[:, 0:D_ATTN] * ATTN_SCALE
    k_ref[...] = qkv[:, D_ATTN:2 * D_ATTN]
    v_ref[...] = qkv[:, 2 * D_ATTN:3 * D_ATTN]
    u = gc * hc
    u_ref[...] = u
    uext_ref[0:V7X_SUBLANES, :] = jnp.zeros((V7X_SUBLANES, u.shape[1]), F32)
    uext_ref[V7X_SUBLANES:V7X_SUBLANES + rows, :] = u
    t = lax.broadcasted_iota(jnp.int32, u.shape, 0) % dec_seq
    u1 = jnp.where(t >= 1, uext_ref[V7X_SUBLANES - 1:V7X_SUBLANES - 1 + rows, :], p1_ref[...])
    u2 = jnp.where(t >= 2, uext_ref[V7X_SUBLANES - 2:V7X_SUBLANES - 2 + rows, :], p2_ref[...])
    co_ref[...] = (gb * _conv3(wc_ref[...], u2, u1, u)).astype(co_ref.dtype)


def _attn_tables(seq):
    x = np.arange(seq, dtype=np.int32)[:, None]
    i = np.arange(TQ, dtype=np.int32)[None, :]
    delta = i + (seq - TQ) - x
    c = np.zeros(delta.shape, np.float32)
    for w, d in zip(WINDOWS, DILATIONS):
        c += (delta >= 0) & (delta % d == 0) & (delta <= w)
    logc = np.where(c > 0.0, np.log2(np.maximum(c, 1.0)), NEG_INF).astype(np.float32)
    return delta.astype(np.float32), logc


def _attn_prompt_kernel(slopes_ref, dist_ref, logc_ref, qt_ref, kt_ref, vt_ref, o_ref,
                        kb_ref, vtb_ref, bm_ref):
    seq = qt_ref.shape[2]
    pair = pl.program_id(0)
    feature = lax.broadcasted_iota(jnp.int32, (V7X_LANES, 1), 0)
    first_head = feature < HEAD_DIM

    @pl.when(pl.program_id(1) == 0)
    def _():
        for h in range(HEADS_PER_STEP):
            slope = slopes_ref[HEADS_PER_STEP * pair + h]
            bm_ref[h] = logc_ref[...] - (slope * LOG2_E) * dist_ref[...]

    kb_ref[...] = kt_ref[0].T.astype(BF16)
    vtb_ref[...] = vt_ref[0].astype(BF16)

    def score_half(st, chunk):
        k0, n = chunk
        h, row0 = st["h"], seq - (st["q0"] + TQ)
        s = (jnp.dot(kb_ref[k0:k0 + n, :], st["q"], preferred_element_type=F32)
             + bm_ref[h, row0 + k0:row0 + k0 + n, :])
        c8 = jnp.max(s.reshape(n // V7X_SUBLANES, V7X_SUBLANES, TQ), axis=0)
        m_new = jnp.maximum(st["m"], jnp.max(c8, axis=0, keepdims=True))
        pending = dict(k0=k0, n=n, s=s, m=m_new, alpha=jnp.exp2(st["m"] - m_new))
        st["m"] = m_new
        return pending

    def value_half(st, pending):
        h, k0, n = st["h"], pending["k0"], pending["n"]
        p = jnp.exp2(pending["s"] - pending["m"])
        st["l8"] = (pending["alpha"] * st["l8"]
                    + jnp.sum(p.reshape(n // V7X_SUBLANES, V7X_SUBLANES, TQ), axis=0))
        vt_h = vtb_ref[h * HEAD_DIM:(h + 1) * HEAD_DIM, k0:k0 + n]
        st["acc"] = pending["alpha"] * st["acc"] + jnp.dot(vt_h, p.astype(BF16),
                                                           preferred_element_type=F32)

    def open_block(q0):
        qt = qt_ref[0, :, q0:q0 + TQ]
        zero = jnp.zeros_like(qt)
        chunks = [(max(k1 - TK, 0), min(TK, k1)) for k1 in range(q0 + TQ, 0, -TK)]
        block = dict(q0=q0, left=HEADS_PER_STEP * len(chunks))
        block["streams"] = [
            dict(block=block, h=h, q0=q0, q=qh, chunks=chunks, m=jnp.full((1, TQ), NEG_INF, F32),
                 l8=jnp.zeros((V7X_SUBLANES, TQ), F32), acc=jnp.zeros((HEAD_DIM, TQ), F32))
            for h, qh in enumerate([jnp.where(first_head, qt, zero), jnp.where(first_head, zero, qt)])]
        return block

    def retire(st, pending):
        value_half(st, pending)
        block = st["block"]
        block["left"] -= 1
        if block["left"] == 0:
            outs = [s["acc"] / jnp.sum(s["l8"], axis=0, keepdims=True) for s in block["streams"]]
            q0 = block["q0"]
            o_ref[q0:q0 + TQ, :] = jnp.concatenate(outs, axis=0).T.astype(o_ref.dtype)

    blocks_per_group = ATTN_STREAMS // HEADS_PER_STEP
    waiting = []
    for g0 in range(0, seq, TQ * blocks_per_group):
        streams = [st for q0 in range(g0, g0 + TQ * blocks_per_group, TQ)
                   for st in open_block(q0)["streams"]]
        for t in range(max(len(st["chunks"]) for st in streams)):
            live = [(st, st["chunks"][t]) for st in streams if t < len(st["chunks"])]
            due = waiting.pop(0) if len(waiting) >= ATTN_LOOKAHEAD else []
            issued = []
            for i, (st, chunk) in enumerate(live):
                issued.append((st, score_half(st, chunk)))
                if i < len(due):
                    retire(*due[i])
            for st, pending in due[len(live):]:
                retire(st, pending)
            waiting.append(issued)
    for step in waiting:
        for st, pending in step:
            retire(st, pending)


def _attn_sample_kernel(slopes_ref, q_ref, kn_ref, vn_ref, kst_ref, vst_ref, o_ref,
                        knew_ref, vnew_ref):
    dec_seq = q_ref.shape[1]
    n_past = kst_ref.shape[2]
    width = q_ref.shape[2]
    n_rows = dec_seq * N_HEADS

    q = q_ref[0]
    head_of_lane = lax.broadcasted_iota(jnp.int32, (N_HEADS, width), 1) // HEAD_DIM
    head_of_row = lax.broadcasted_iota(jnp.int32, (N_HEADS, width), 0)
    head_mask = head_of_lane == head_of_row
    wt = jnp.concatenate(
        [jnp.where(head_mask, jnp.broadcast_to(q[t:t + 1, :], (N_HEADS, width)), 0.0)
         for t in range(dec_seq)], axis=0).astype(BF16)

    knew_ref[...] = jnp.zeros(knew_ref.shape, F32)
    vnew_ref[...] = jnp.zeros(vnew_ref.shape, F32)
    knew_ref[0:dec_seq, :] = kn_ref[0]
    vnew_ref[0:dec_seq, :] = vn_ref[0]

    s_past = jnp.dot(wt, kst_ref[0].astype(BF16), preferred_element_type=F32)
    s_new = _dot_nt(wt, knew_ref[...].astype(BF16))

    row = lax.broadcasted_iota(jnp.int32, (n_rows, 1), 0)
    step = row // N_HEADS
    slope_col = jnp.zeros((n_rows, 1), F32)
    for h in range(N_HEADS):
        slope_col = jnp.where(row % N_HEADS == h, slopes_ref[h], slope_col)

    def weigh(s, key_pos):
        delta = n_past + step - key_pos
        c = _branch_count(delta)
        s = jnp.where(c > 0.0, s - slope_col * delta.astype(F32), NEG_INF)
        return s, c

    s_past, c_past = weigh(s_past, lax.broadcasted_iota(jnp.int32, (1, n_past), 1))
    s_new, c_new = weigh(s_new, n_past + lax.broadcasted_iota(jnp.int32, (1, V7X_LANES), 1))
    m = jnp.maximum(jnp.max(s_past, axis=-1, keepdims=True),
                    jnp.max(s_new, axis=-1, keepdims=True))
    p_past = c_past * jnp.exp(s_past - m)
    p_new = c_new * jnp.exp(s_new - m)
    l = jnp.sum(p_past, axis=-1, keepdims=True) + jnp.sum(p_new, axis=-1, keepdims=True)
    o_all = (_dot_nt(p_past.astype(BF16), vst_ref[0].astype(BF16))
             + jnp.dot(p_new.astype(BF16), vnew_ref[...].astype(BF16),
                       preferred_element_type=F32)) / l
    outs = []
    for t in range(dec_seq):
        blk = o_all[t * N_HEADS:(t + 1) * N_HEADS, :]
        outs.append(jnp.sum(jnp.where(head_mask, blk, 0.0), axis=0, keepdims=True))
    o_ref[0] = jnp.concatenate(outs, axis=0).astype(o_ref.dtype)


def _ffn_kernel(x_ref, a_ref, c_ref, wo_ref, g2_ref, wg_ref, wu_ref, wd_ref, gf_ref, y_ref, *,
                final_norm):
    tm = x_ref.shape[0]
    rg = tm // ROW_GROUPS if tm % (ROW_GROUPS * V7X_LANES) == 0 else tm
    groups = [slice(r0, r0 + rg) for r0 in range(0, tm, rg)]
    hs = []
    for rows in groups:
        mix = jnp.concatenate([a_ref[rows, :].astype(BF16), c_ref[rows, :].astype(BF16)], axis=-1)
        hs.append(x_ref[rows, :] + jnp.dot(mix, wo_ref[...], preferred_element_type=F32))
    acts = []
    for h in hs:
        hn = _rms(h, g2_ref[...]).astype(BF16)
        gate = jnp.dot(hn, wg_ref[...], preferred_element_type=F32)
        up = jnp.dot(hn, wu_ref[...], preferred_element_type=F32)
        acts.append(((gate * (1.0 / (1.0 + jnp.exp(-gate)))) * up).astype(BF16))
    for rows, h, act in zip(groups, hs, acts):
        y = h + jnp.dot(act, wd_ref[...], preferred_element_type=F32)
        y_ref[rows, :] = _rms(y, gf_ref[...]) if final_norm else y


def _nbytes(shape, dtype):
    n = 1
    for s in shape:
        n *= s
    return n * jnp.dtype(dtype).itemsize


def _vmem_limit(pipelined, resident, temporaries):
    need = 2 * sum(pipelined) + sum(resident) + sum(temporaries)
    return min(V7X_VMEM_BYTES, need + need // 4)


def _resident(shape):
    zeros = (0,) * len(shape)
    return pl.BlockSpec(shape, lambda *_: zeros, pipeline_mode=pl.Buffered(1))


def _row_tile(rows, largest=512):
    for tm in (largest, 512, 256, 128):
        if rows % tm == 0:
            return tm
    raise ValueError(f"row count {rows} is not a multiple of 128")


def _inproj_prompt(x, g, w_cc, w_qkvt, wc, seq):
    rows, d_model = x.shape
    d_conv = wc.shape[1]
    tm = _row_tile(seq, largest=INPROJ_ROWS)
    tiles_per_seq = seq // tm
    n_seq = rows // seq
    row_blk = lambda width: pl.BlockSpec((tm, width), lambda b, j: (b * tiles_per_seq + j, 0))
    col_blk = pl.BlockSpec((1, D_ATTN, tm), lambda b, j: (b, 0, j))
    limit = _vmem_limit(
        [_nbytes((tm, d_model), F32), _nbytes((D_ATTN, tm), BF16), 2 * _nbytes((D_ATTN, tm), F32),
         _nbytes((tm, d_conv), BF16)],
        [_nbytes(w_cc.shape, BF16), _nbytes(w_qkvt.shape, BF16)],
        [2 * _nbytes((tm, d_model), F32), 2 * _nbytes((tm, w_cc.shape[1]), F32),
         2 * _nbytes((w_qkvt.shape[0], tm), F32), 4 * _nbytes((tm, d_conv), F32)])
    return pl.pallas_call(
        _inproj_prompt_kernel,
        grid=(n_seq, tiles_per_seq),
        in_specs=[row_blk(d_model), _resident((1, d_model)), _resident(w_cc.shape),
                  _resident(w_qkvt.shape), _resident(wc.shape)],
        out_specs=[col_blk, col_blk, col_blk, row_blk(d_conv),
                   pl.BlockSpec((1, V7X_SUBLANES, d_conv), lambda b, j: (b, 0, 0))],
        out_shape=[jax.ShapeDtypeStruct((n_seq, D_ATTN, seq), BF16),
                   jax.ShapeDtypeStruct((n_seq, D_ATTN, seq), F32),
                   jax.ShapeDtypeStruct((n_seq, D_ATTN, seq), F32),
                   jax.ShapeDtypeStruct((rows, d_conv), BF16),
                   jax.ShapeDtypeStruct((n_seq, V7X_SUBLANES, d_conv), F32)],
        scratch_shapes=[pltpu.VMEM((tm + V7X_SUBLANES, d_conv), F32)],
        compiler_params=pltpu.CompilerParams(dimension_semantics=("arbitrary", "arbitrary"),
                                             vmem_limit_bytes=limit),
        name="inproj_prompt",
    )(x, g, w_cc, w_qkvt, wc)


def _inproj_sample(x, g, w_cc, w_qkv, wc, prev1, prev2, dec_seq):
    rows, d_model = x.shape
    d_conv = wc.shape[1]
    full = lambda shape: pl.BlockSpec(shape, lambda i: (0,) * len(shape))
    return pl.pallas_call(
        functools.partial(_inproj_sample_kernel, dec_seq=dec_seq),
        grid=(1,),
        in_specs=[full(x.shape), full((1, d_model)), full(w_cc.shape), full(w_qkv.shape),
                  full(wc.shape), full(prev1.shape), full(prev2.shape)],
        out_specs=[full((rows, D_ATTN))] * 3 + [full((rows, d_conv))] * 2,
        out_shape=[jax.ShapeDtypeStruct((rows, D_ATTN), F32)] * 3
        + [jax.ShapeDtypeStruct((rows, d_conv), BF16), jax.ShapeDtypeStruct((rows, d_conv), F32)],
        scratch_shapes=[pltpu.VMEM((rows + V7X_SUBLANES, d_conv), F32)],
        compiler_params=pltpu.CompilerParams(dimension_semantics=("arbitrary",)),
        name="inproj_sample",
    )(x, g, w_cc, w_qkv, wc, prev1, prev2)


def _attn_prompt(slopes, qt, kt, vt):
    n_seq, _, seq = qt.shape
    rows = n_seq * seq
    o_blk = pl.BlockSpec((seq, V7X_LANES), lambda p, b: (b, p))
    kv_blk = pl.BlockSpec((1, V7X_LANES, seq), lambda p, b: (b, p, 0))
    dist, logc = _attn_tables(seq)
    table = _nbytes((TQ, seq), F32)
    limit = _vmem_limit(
        [2 * _nbytes((seq, V7X_LANES), F32), 2 * _nbytes((seq, V7X_LANES), BF16)],
        [2 * _nbytes((seq, V7X_LANES), BF16), (2 + HEADS_PER_STEP) * table],
        [3 * HEADS_PER_STEP * table])
    return pl.pallas_call(
        _attn_prompt_kernel,
        grid=(N_HEADS // HEADS_PER_STEP, n_seq),
        in_specs=[pl.BlockSpec(memory_space=pltpu.SMEM), _resident((seq, TQ)),
                  _resident((seq, TQ)), kv_blk, kv_blk, kv_blk],
        out_specs=o_blk,
        out_shape=jax.ShapeDtypeStruct((rows, D_ATTN), BF16),
        scratch_shapes=[pltpu.VMEM((seq, V7X_LANES), BF16), pltpu.VMEM((V7X_LANES, seq), BF16),
                        pltpu.VMEM((HEADS_PER_STEP, seq, TQ), F32)],
        compiler_params=pltpu.CompilerParams(dimension_semantics=("arbitrary", "arbitrary"),
                                             vmem_limit_bytes=limit),
        name="attn_prompt",
    )(slopes, dist, logc, qt, kt, vt)


def _attn_sample(slopes, q, kn, vn, kst, vst):
    n_seq, dec_seq, width = q.shape
    n_past = kst.shape[2]
    new_blk = pl.BlockSpec((1, dec_seq, width), lambda b: (b, 0, 0))
    past_blk = pl.BlockSpec((1, width, n_past), lambda b: (b, 0, 0))
    limit = _vmem_limit(
        [2 * _nbytes((n_past, width), F32)], [],
        [2 * _nbytes((n_past, width), BF16), 8 * _nbytes((dec_seq * N_HEADS, n_past), F32)])
    return pl.pallas_call(
        _attn_sample_kernel,
        grid=(n_seq,),
        in_specs=[pl.BlockSpec(memory_space=pltpu.SMEM), new_blk, new_blk, new_blk,
                  past_blk, past_blk],
        out_specs=new_blk,
        out_shape=jax.ShapeDtypeStruct((n_seq, dec_seq, width), F32),
        scratch_shapes=[pltpu.VMEM((V7X_LANES, width), F32)] * 2,
        compiler_params=pltpu.CompilerParams(dimension_semantics=("arbitrary",),
                                             vmem_limit_bytes=limit),
        name="attn_sample",
    )(slopes, q, kn, vn, kst, vst)


def _ffn(x, attn_o, conv_o, wo, g2, wg, wu, wd, gf, final_norm, tm):
    rows, d_model = x.shape
    d_ff = wg.shape[1]
    row_blk = lambda width: pl.BlockSpec((tm, width), lambda i: (i, 0))
    limit = _vmem_limit(
        [2 * _nbytes((tm, d_model), F32), _nbytes((tm, attn_o.shape[1]), attn_o.dtype),
         _nbytes((tm, conv_o.shape[1]), conv_o.dtype)],
        [_nbytes(w.shape, BF16) for w in (wo, wg, wu, wd)],
        [4 * _nbytes((tm, d_model), F32), 3 * _nbytes((tm, d_ff), F32)])
    return pl.pallas_call(
        functools.partial(_ffn_kernel, final_norm=final_norm),
        grid=(rows // tm,),
        in_specs=[row_blk(d_model), row_blk(attn_o.shape[1]), row_blk(conv_o.shape[1]),
                  _resident(wo.shape), _resident((1, d_model)), _resident(wg.shape),
                  _resident(wu.shape), _resident(wd.shape), _resident((1, d_model))],
        out_specs=row_blk(d_model),
        out_shape=jax.ShapeDtypeStruct((rows, d_model), F32),
        compiler_params=pltpu.CompilerParams(dimension_semantics=("arbitrary",),
                                             vmem_limit_bytes=limit),
        name="ffn",
    )(x, attn_o, conv_o, wo, g2, wg, wu, wd, gf)


def _alibi_slopes():
    return jnp.exp2(-8.0 * jnp.arange(1, N_HEADS + 1, dtype=F32) / N_HEADS)


def kernel(x_prompt, x_sample, state_attn_k, state_attn_v, state_conv, norm_mix_g, w_in, w_conv,
           w_out, norm_ffn_g, w_gate, w_up, w_down, norm_final_g):
    depth = w_in.shape[0]
    batch, seq, d_model = x_prompt.shape
    dec_batch, dec_seq, _ = x_sample.shape
    d_conv = w_conv.shape[2]
    n_past = state_attn_k.shape[2]
    assert n_past >= max(WINDOWS) and seq <= max(WINDOWS) and seq % TQ == 0
    assert CONV_WIDTH - 1 <= dec_seq <= V7X_LANES
    slopes = _alibi_slopes()
    gf = norm_final_g.reshape(1, d_model)

    yp = x_prompt.reshape(batch * seq, d_model)
    ys = x_sample.reshape(dec_batch * dec_seq, d_model)
    outs = [[] for _ in range(6)]
    for layer in range(depth):
        g1 = norm_mix_g[layer].reshape(1, d_model)
        g2 = norm_ffn_g[layer].reshape(1, d_model)
        w_in_bf = w_in[layer].astype(BF16)
        w_qkv = w_in_bf[:, :3 * D_ATTN]
        w_cc = w_in_bf[:, 3 * D_ATTN:]
        weights = [w.astype(BF16) for w in (w_out[layer], w_gate[layer], w_up[layer], w_down[layer])]
        wc = w_conv[layer]

        qt, kt, vt, conv_o, u_last = _inproj_prompt(yp, g1, w_cc, w_qkv.T, wc, seq)
        attn_o = _attn_prompt(slopes, qt, kt, vt)
        y_layer = _ffn(yp, attn_o, conv_o, weights[0], g2, *weights[1:], gf,
                       layer == depth - 1, _row_tile(seq, largest=FFN_ROWS))
        to_heads = lambda a: a.reshape(batch, N_HEADS, HEAD_DIM, seq).transpose(0, 3, 1, 2)
        outs[0].append(to_heads(kt))
        outs[1].append(to_heads(vt))
        outs[2].append(u_last[:, V7X_SUBLANES - (CONV_WIDTH - 1):, :])
        yp = y_layer

        st = state_conv[layer]
        prev1 = jnp.pad(st[:, 1:2], ((0, 0), (0, dec_seq - 1), (0, 0)))
        prev2 = jnp.pad(st, ((0, 0), (0, dec_seq - 2), (0, 0)))
        qs, ks, vs, conv_s, u_s = _inproj_sample(
            ys, g1, w_cc, w_qkv, wc, prev1.reshape(-1, d_conv), prev2.reshape(-1, d_conv), dec_seq)
        as3 = lambda a: a.reshape(dec_batch, dec_seq, D_ATTN)
        feature_major = lambda a: a.transpose(0, 2, 3, 1).reshape(dec_batch, D_ATTN, n_past)
        attn_s = _attn_sample(slopes, as3(qs), as3(ks), as3(vs),
                              feature_major(state_attn_k[layer]),
                              feature_major(state_attn_v[layer]))
        ys = _ffn(ys, attn_s.reshape(-1, D_ATTN), conv_s, weights[0], g2, *weights[1:], gf,
                  layer == depth - 1, ys.shape[0])
        outs[3].append(ks.reshape(dec_batch, dec_seq, N_HEADS, HEAD_DIM))
        outs[4].append(vs.reshape(dec_batch, dec_seq, N_HEADS, HEAD_DIM))
        outs[5].append(u_s.reshape(dec_batch, dec_seq, d_conv)[:, dec_seq - (CONV_WIDTH - 1):])

    y_prompt = yp.reshape(batch, seq, d_model)
    y_sample = ys.reshape(dec_batch, dec_seq, d_model)
    new_k_p, new_v_p, new_c_p, new_k_s, new_v_s, new_c_s = [jnp.stack(o) for o in outs]
    return (y_prompt, y_sample, new_k_p, new_v_p, new_c_p, new_k_s, new_v_s, new_c_s)
```

```python
import functools

import jax
import jax.numpy as jnp
import numpy as np
from jax import lax
from jax.experimental import pallas as pl
from jax.experimental.pallas import tpu as pltpu

F32 = jnp.float32
BF16 = jnp.bfloat16

HEAD_DIM = 64
N_HEADS = 8
D_ATTN = N_HEADS * HEAD_DIM
CONV_WIDTH = 3
WINDOWS = (128, 512, 2048)
DILATIONS = (1, 4, 16)
RMS_EPS = 1e-6
ATTN_SCALE = HEAD_DIM ** -0.5
LOG2_E = 1.4426950408889634
NEG_INF = -1e30

V7X_LANES = 128
V7X_SUBLANES = 8
V7X_VMEM_BYTES = 64 * 1024 * 1024

HEADS_PER_STEP = V7X_LANES // HEAD_DIM
TQ = 256
HALF_TQ = TQ // 2
TK = 512
ATTN_STREAMS = 4
ATTN_LOOKAHEAD = 1
ROW_GROUPS = 4
FFN_ROWS = 1024
INPROJ_ROW_GROUPS = 8
INPROJ_ROWS = 2048


def _rms(x, g):
    y = x * lax.rsqrt(jnp.mean(x * x, axis=-1, keepdims=True) + RMS_EPS)
    return y * g


def _branch_count(delta):
    nonneg = delta >= 0
    c = jnp.zeros(delta.shape, F32)
    for w, d in zip(WINDOWS, DILATIONS):
        hit = nonneg & ((delta & (d - 1)) == 0) & (delta <= w)
        c = c + hit.astype(F32)
    return c


def _dot_nt(a, b):
    return lax.dot_general(a, b, (((1,), (1,)), ((), ())), preferred_element_type=F32)


def _project(x, g, wcc_ref):
    d_conv = wcc_ref.shape[1] // 3
    xn = _rms(x, g).astype(BF16)
    zc = jnp.dot(xn, wcc_ref[...], preferred_element_type=F32)
    return xn, zc[:, 0:d_conv], zc[:, d_conv:2 * d_conv], zc[:, 2 * d_conv:3 * d_conv]


def _conv3(wc, u2, u1, u0):
    acc = wc[0:1, :] * u2
    acc = acc + wc[1:2, :] * u1
    return acc + wc[2:3, :] * u0


def _inproj_prompt_kernel(x_ref, g_ref, wcc_ref, wqkvt_ref, wc_ref, qt_ref, kt_ref, vt_ref, co_ref,
                          ulast_ref, uext_ref):
    tm = x_ref.shape[0]
    j = pl.program_id(1)
    tiles_per_seq = pl.num_programs(1)

    @pl.when(j == 0)
    def _():
        uext_ref[0:V7X_SUBLANES, :] = jnp.zeros((V7X_SUBLANES, uext_ref.shape[1]), F32)

    rg = tm // INPROJ_ROW_GROUPS
    for r0 in range(0, tm, rg):
        xn, hc, gb, gc = _project(x_ref[r0:r0 + rg, :], g_ref[...], wcc_ref)
        qkvt = _dot_nt(wqkvt_ref[...], xn)
        qt_ref[0, :, r0:r0 + rg] = (qkvt[0:D_ATTN, :] * (ATTN_SCALE * LOG2_E)).astype(qt_ref.dtype)
        kt_ref[0, :, r0:r0 + rg] = qkvt[D_ATTN:2 * D_ATTN, :]
        vt_ref[0, :, r0:r0 + rg] = qkvt[2 * D_ATTN:3 * D_ATTN, :]
        u = gc * hc
        uext_ref[V7X_SUBLANES:V7X_SUBLANES + rg, :] = u
        u1 = uext_ref[V7X_SUBLANES - 1:V7X_SUBLANES - 1 + rg, :]
        u2 = uext_ref[V7X_SUBLANES - 2:V7X_SUBLANES - 2 + rg, :]
        co_ref[r0:r0 + rg, :] = (gb * _conv3(wc_ref[...], u2, u1, u)).astype(co_ref.dtype)
        tail = u[rg - V7X_SUBLANES:rg, :]
        uext_ref[0:V7X_SUBLANES, :] = tail

    @pl.when(j == tiles_per_seq - 1)
    def _():
        ulast_ref[0] = tail


def _inproj_sample_kernel(x_ref, g_ref, wcc_ref, wqkv_ref, wc_ref, p1_ref, p2_ref, q_ref, k_ref,
                          v_ref, co_ref, u_ref, uext_ref, *, dec_seq):
    rows = x_ref.shape[0]
    xn, hc, gb, gc = _project(x_ref[...], g_ref[...], wcc_ref)
    qkv = jnp.dot(xn, wqkv_ref[...], preferred_element_type=F32)
    q_ref[...] = qkv[:, 0:D_ATTN] * ATTN_SCALE
    k_ref[...] = qkv[:, D_ATTN:2 * D_ATTN]
    v_ref[...] = qkv[:, 2 * D_ATTN:3 * D_ATTN]
    u = gc * hc
    u_ref[...] = u
    uext_ref[0:V7X_SUBLANES, :] = jnp.zeros((V7X_SUBLANES, u.shape[1]), F32)
    uext_ref[V7X_SUBLANES:V7X_SUBLANES + rows, :] = u
    t = lax.broadcasted_iota(jnp.int32, u.shape, 0) % dec_seq
    u1 = jnp.where(t >= 1, uext_ref[V7X_SUBLANES - 1:V7X_SUBLANES - 1 + rows, :], p1_ref[...])
    u2 = jnp.where(t >= 2, uext_ref[V7X_SUBLANES - 2:V7X_SUBLANES - 2 + rows, :], p2_ref[...])
    co_ref[...] = (gb * _conv3(wc_ref[...], u2, u1, u)).astype(co_ref.dtype)


def _attn_tables(seq):
    x = np.arange(seq, dtype=np.int32)[:, None]
    i = np.arange(TQ, dtype=np.int32)[None, :]
    delta = i + (seq - TQ) - x
    c = np.zeros(delta.shape, np.float32)
    for w, d in zip(WINDOWS, DILATIONS):
        c += (delta >= 0) & (delta % d == 0) & (delta <= w)
    logc = np.where(c > 0.0, np.log2(np.maximum(c, 1.0)), NEG_INF).astype(np.float32)
    return delta.astype(np.float32), logc


def _attn_prompt_kernel(slopes_ref, dist_ref, logc_ref, qt_ref, kt_ref, vt_ref, o_ref,
                        kb_ref, vtb_ref, bm_ref):
    seq = qt_ref.shape[2]
    pair = pl.program_id(0)
    feature = lax.broadcasted_iota(jnp.int32, (V7X_LANES, 1), 0)
    first_head = feature < HEAD_DIM

    @pl.when(pl.program_id(1) == 0)
    def _():
        for h in range(HEADS_PER_STEP):
            slope = slopes_ref[HEADS_PER_STEP * pair + h]
            bm_ref[h] = logc_ref[...] - (slope * LOG2_E) * dist_ref[...]

    kb_ref[...] = kt_ref[0].T.astype(BF16)
    vtb_ref[...] = vt_ref[0].astype(BF16)

    def score_half(st, chunk):
        k0, n = chunk
        h, row0 = st["h"], seq - (st["q0"] + TQ)
        s = (jnp.dot(kb_ref[k0:k0 + n, :], st["q"], preferred_element_type=F32)
             + bm_ref[h, row0 + k0:row0 + k0 + n, :])
        tail = HALF_TQ if k0 + n == st["q0"] + TQ else 0
        main = s[0:n - tail, :]
        c8 = jnp.max(main.reshape((n - tail) // V7X_SUBLANES, V7X_SUBLANES, TQ), axis=0)
        s_tail = None
        if tail:
            s_tail = s[n - tail:n, HALF_TQ:TQ]
            t8 = jnp.max(s_tail.reshape(tail // V7X_SUBLANES, V7X_SUBLANES, HALF_TQ), axis=0)
            c8 = jnp.concatenate([c8[:, 0:HALF_TQ], jnp.maximum(c8[:, HALF_TQ:TQ], t8)], axis=1)
        m_new = jnp.maximum(st["m"], jnp.max(c8, axis=0, keepdims=True))
        pending = dict(k0=k0, n=n, s=main, s_tail=s_tail, m=m_new,
                       alpha=jnp.exp2(st["m"] - m_new))
        st["m"] = m_new
        return pending

    def value_half(st, pending):
        h, k0, n = st["h"], pending["k0"], pending["n"]
        p = jnp.exp2(pending["s"] - pending["m"])
        l8 = jnp.sum(p.reshape(p.shape[0] // V7X_SUBLANES, V7X_SUBLANES, TQ), axis=0)
        p = p.astype(BF16)
        if pending["s_tail"] is not None:
            p_tail = jnp.exp2(pending["s_tail"] - pending["m"][:, HALF_TQ:TQ])
            t8 = jnp.sum(p_tail.reshape(HALF_TQ // V7X_SUBLANES, V7X_SUBLANES, HALF_TQ), axis=0)
            l8 = jnp.concatenate([l8[:, 0:HALF_TQ], l8[:, HALF_TQ:TQ] + t8], axis=1)
            p_tail = jnp.concatenate([jnp.zeros((HALF_TQ, HALF_TQ), BF16), p_tail.astype(BF16)],
                                     axis=1)
            p = jnp.concatenate([p, p_tail], axis=0)
        st["l8"] = pending["alpha"] * st["l8"] + l8
        vt_h = vtb_ref[h * HEAD_DIM:(h + 1) * HEAD_DIM, k0:k0 + n]
        st["acc"] = pending["alpha"] * st["acc"] + jnp.dot(vt_h, p, preferred_element_type=F32)

    def open_block(q0):
        qt = qt_ref[0, :, q0:q0 + TQ]
        zero = jnp.zeros_like(qt)
        chunks = [(max(k1 - TK, 0), min(TK, k1)) for k1 in range(q0 + TQ, 0, -TK)]
        block = dict(q0=q0, left=HEADS_PER_STEP * len(chunks))
        block["streams"] = [
            dict(block=block, h=h, q0=q0, q=qh, chunks=chunks, m=jnp.full((1, TQ), NEG_INF, F32),
                 l8=jnp.zeros((V7X_SUBLANES, TQ), F32), acc=jnp.zeros((HEAD_DIM, TQ), F32))
            for h, qh in enumerate([jnp.where(first_head, qt, zero), jnp.where(first_head, zero, qt)])]
        return block

    def retire(st, pending):
        value_half(st, pending)
        block = st["block"]
        block["left"] -= 1
        if block["left"] == 0:
            outs = [s["acc"] / jnp.sum(s["l8"], axis=0, keepdims=True) for s in block["streams"]]
            q0 = block["q0"]
            o_ref[q0:q0 + TQ, :] = jnp.concatenate(outs, axis=0).T.astype(o_ref.dtype)

    blocks_per_group = ATTN_STREAMS // HEADS_PER_STEP
    waiting = []
    for g0 in range(0, seq, TQ * blocks_per_group):
        streams = [st for q0 in range(g0, g0 + TQ * blocks_per_group, TQ)
                   for st in open_block(q0)["streams"]]
        for t in range(max(len(st["chunks"]) for st in streams)):
            live = [(st, st["chunks"][t]) for st in streams if t < len(st["chunks"])]
            due = waiting.pop(0) if len(waiting) >= ATTN_LOOKAHEAD else []
            issued = []
            for i, (st, chunk) in enumerate(live):
                issued.append((st, score_half(st, chunk)))
                if i < len(due):
                    retire(*due[i])
            for st, pending in due[len(live):]:
                retire(st, pending)
            waiting.append(issued)
    for step in waiting:
        for st, pending in step:
            retire(st, pending)


def _attn_sample_kernel(slopes_ref, q_ref, kn_ref, vn_ref, kst_ref, vst_ref, o_ref,
                        knew_ref, vnew_ref):
    dec_seq = q_ref.shape[1]
    n_past = kst_ref.shape[2]
    width = q_ref.shape[2]
    n_rows = dec_seq * N_HEADS

    q = q_ref[0]
    head_of_lane = lax.broadcasted_iota(jnp.int32, (N_HEADS, width), 1) // HEAD_DIM
    head_of_row = lax.broadcasted_iota(jnp.int32, (N_HEADS, width), 0)
    head_mask = head_of_lane == head_of_row
    wt = jnp.concatenate(
        [jnp.where(head_mask, jnp.broadcast_to(q[t:t + 1, :], (N_HEADS, width)), 0.0)
         for t in range(dec_seq)], axis=0).astype(BF16)

    knew_ref[...] = jnp.zeros(knew_ref.shape, F32)
    vnew_ref[...] = jnp.zeros(vnew_ref.shape, F32)
    knew_ref[0:dec_seq, :] = kn_ref[0]
    vnew_ref[0:dec_seq, :] = vn_ref[0]

    s_past = jnp.dot(wt, kst_ref[0].astype(BF16), preferred_element_type=F32)
    s_new = _dot_nt(wt, knew_ref[...].astype(BF16))

    row = lax.broadcasted_iota(jnp.int32, (n_rows, 1), 0)
    step = row // N_HEADS
    slope_col = jnp.zeros((n_rows, 1), F32)
    for h in range(N_HEADS):
        slope_col = jnp.where(row % N_HEADS == h, slopes_ref[h], slope_col)

    def weigh(s, key_pos):
        delta = n_past + step - key_pos
        c = _branch_count(delta)
        s = jnp.where(c > 0.0, s - slope_col * delta.astype(F32), NEG_INF)
        return s, c

    s_past, c_past = weigh(s_past, lax.broadcasted_iota(jnp.int32, (1, n_past), 1))
    s_new, c_new = weigh(s_new, n_past + lax.broadcasted_iota(jnp.int32, (1, V7X_LANES), 1))
    m = jnp.maximum(jnp.max(s_past, axis=-1, keepdims=True),
                    jnp.max(s_new, axis=-1, keepdims=True))
    p_past = c_past * jnp.exp(s_past - m)
    p_new = c_new * jnp.exp(s_new - m)
    l = jnp.sum(p_past, axis=-1, keepdims=True) + jnp.sum(p_new, axis=-1, keepdims=True)
    o_all = (_dot_nt(p_past.astype(BF16), vst_ref[0].astype(BF16))
             + jnp.dot(p_new.astype(BF16), vnew_ref[...].astype(BF16),
                       preferred_element_type=F32)) / l
    outs = []
    for t in range(dec_seq):
        blk = o_all[t * N_HEADS:(t + 1) * N_HEADS, :]
        outs.append(jnp.sum(jnp.where(head_mask, blk, 0.0), axis=0, keepdims=True))
    o_ref[0] = jnp.concatenate(outs, axis=0).astype(o_ref.dtype)


def _ffn_kernel(x_ref, a_ref, c_ref, wo_ref, g2_ref, wg_ref, wu_ref, wd_ref, gf_ref, y_ref, *,
                final_norm):
    tm = x_ref.shape[0]
    rg = tm // ROW_GROUPS if tm % (ROW_GROUPS * V7X_LANES) == 0 else tm
    groups = [slice(r0, r0 + rg) for r0 in range(0, tm, rg)]
    hs = []
    for rows in groups:
        mix = jnp.concatenate([a_ref[rows, :].astype(BF16), c_ref[rows, :].astype(BF16)], axis=-1)
        hs.append(x_ref[rows, :] + jnp.dot(mix, wo_ref[...], preferred_element_type=F32))
    acts = []
    for h in hs:
        hn = _rms(h, g2_ref[...]).astype(BF16)
        gate = jnp.dot(hn, wg_ref[...], preferred_element_type=F32)
        up = jnp.dot(hn, wu_ref[...], preferred_element_type=F32)
        acts.append(((gate * (1.0 / (1.0 + jnp.exp(-gate)))) * up).astype(BF16))
    for rows, h, act in zip(groups, hs, acts):
        y = h + jnp.dot(act, wd_ref[...], preferred_element_type=F32)
        y_ref[rows, :] = _rms(y, gf_ref[...]) if final_norm else y


def _nbytes(shape, dtype):
    n = 1
    for s in shape:
        n *= s
    return n * jnp.dtype(dtype).itemsize


def _vmem_limit(pipelined, resident, temporaries):
    need = 2 * sum(pipelined) + sum(resident) + sum(temporaries)
    return min(V7X_VMEM_BYTES, need + need // 4)


def _resident(shape):
    zeros = (0,) * len(shape)
    return pl.BlockSpec(shape, lambda *_: zeros, pipeline_mode=pl.Buffered(1))


def _row_tile(rows, largest=512):
    for tm in (largest, 512, 256, 128):
        if rows % tm == 0:
            return tm
    raise ValueError(f"row count {rows} is not a multiple of 128")


def _inproj_prompt(x, g, w_cc, w_qkvt, wc, seq):
    rows, d_model = x.shape
    d_conv = wc.shape[1]
    tm = _row_tile(seq, largest=INPROJ_ROWS)
    tiles_per_seq = seq // tm
    n_seq = rows // seq
    row_blk = lambda width: pl.BlockSpec((tm, width), lambda b, j: (b * tiles_per_seq + j, 0))
    col_blk = pl.BlockSpec((1, D_ATTN, tm), lambda b, j: (b, 0, j))
    limit = _vmem_limit(
        [_nbytes((tm, d_model), F32), _nbytes((D_ATTN, tm), BF16), 2 * _nbytes((D_ATTN, tm), F32),
         _nbytes((tm, d_conv), BF16)],
        [_nbytes(w_cc.shape, BF16), _nbytes(w_qkvt.shape, BF16)],
        [2 * _nbytes((tm, d_model), F32), 2 * _nbytes((tm, w_cc.shape[1]), F32),
         2 * _nbytes((w_qkvt.shape[0], tm), F32), 4 * _nbytes((tm, d_conv), F32)])
    return pl.pallas_call(
        _inproj_prompt_kernel,
        grid=(n_seq, tiles_per_seq),
        in_specs=[row_blk(d_model), _resident((1, d_model)), _resident(w_cc.shape),
                  _resident(w_qkvt.shape), _resident(wc.shape)],
        out_specs=[col_blk, col_blk, col_blk, row_blk(d_conv),
                   pl.BlockSpec((1, V7X_SUBLANES, d_conv), lambda b, j: (b, 0, 0))],
        out_shape=[jax.ShapeDtypeStruct((n_seq, D_ATTN, seq), BF16),
                   jax.ShapeDtypeStruct((n_seq, D_ATTN, seq), F32),
                   jax.ShapeDtypeStruct((n_seq, D_ATTN, seq), F32),
                   jax.ShapeDtypeStruct((rows, d_conv), BF16),
                   jax.ShapeDtypeStruct((n_seq, V7X_SUBLANES, d_conv), F32)],
        scratch_shapes=[pltpu.VMEM((tm + V7X_SUBLANES, d_conv), F32)],
        compiler_params=pltpu.CompilerParams(dimension_semantics=("arbitrary", "arbitrary"),
                                             vmem_limit_bytes=limit),
        name="inproj_prompt",
    )(x, g, w_cc, w_qkvt, wc)


def _inproj_sample(x, g, w_cc, w_qkv, wc, prev1, prev2, dec_seq):
    rows, d_model = x.shape
    d_conv = wc.shape[1]
    full = lambda shape: pl.BlockSpec(shape, lambda i: (0,) * len(shape))
    return pl.pallas_call(
        functools.partial(_inproj_sample_kernel, dec_seq=dec_seq),
        grid=(1,),
        in_specs=[full(x.shape), full((1, d_model)), full(w_cc.shape), full(w_qkv.shape),
                  full(wc.shape), full(prev1.shape), full(prev2.shape)],
        out_specs=[full((rows, D_ATTN))] * 3 + [full((rows, d_conv))] * 2,
        out_shape=[jax.ShapeDtypeStruct((rows, D_ATTN), F32)] * 3
        + [jax.ShapeDtypeStruct((rows, d_conv), BF16), jax.ShapeDtypeStruct((rows, d_conv), F32)],
        scratch_shapes=[pltpu.VMEM((rows + V7X_SUBLANES, d_conv), F32)],
        compiler_params=pltpu.CompilerParams(dimension_semantics=("arbitrary",)),
        name="inproj_sample",
    )(x, g, w_cc, w_qkv, wc, prev1, prev2)


def _attn_prompt(slopes, qt, kt, vt):
    n_seq, _, seq = qt.shape
    rows = n_seq * seq
    o_blk = pl.BlockSpec((seq, V7X_LANES), lambda p, b: (b, p))
    kv_blk = pl.BlockSpec((1, V7X_LANES, seq), lambda p, b: (b, p, 0))
    dist, logc = _attn_tables(seq)
    table = _nbytes((TQ, seq), F32)
    limit = _vmem_limit(
        [2 * _nbytes((seq, V7X_LANES), F32), 2 * _nbytes((seq, V7X_LANES), BF16)],
        [2 * _nbytes((seq, V7X_LANES), BF16), (2 + HEADS_PER_STEP) * table],
        [3 * HEADS_PER_STEP * table])
    return pl.pallas_call(
        _attn_prompt_kernel,
        grid=(N_HEADS // HEADS_PER_STEP, n_seq),
        in_specs=[pl.BlockSpec(memory_space=pltpu.SMEM), _resident((seq, TQ)),
                  _resident((seq, TQ)), kv_blk, kv_blk, kv_blk],
        out_specs=o_blk,
        out_shape=jax.ShapeDtypeStruct((rows, D_ATTN), BF16),
        scratch_shapes=[pltpu.VMEM((seq, V7X_LANES), BF16), pltpu.VMEM((V7X_LANES, seq), BF16),
                        pltpu.VMEM((HEADS_PER_STEP, seq, TQ), F32)],
        compiler_params=pltpu.CompilerParams(dimension_semantics=("arbitrary", "arbitrary"),
                                             vmem_limit_bytes=limit),
        name="attn_prompt",
    )(slopes, dist, logc, qt, kt, vt)


def _attn_sample(slopes, q, kn, vn, kst, vst):
    n_seq, dec_seq, width = q.shape
    n_past = kst.shape[2]
    new_blk = pl.BlockSpec((1, dec_seq, width), lambda b: (b, 0, 0))
    past_blk = pl.BlockSpec((1, width, n_past), lambda b: (b, 0, 0))
    limit = _vmem_limit(
        [2 * _nbytes((n_past, width), F32)], [],
        [2 * _nbytes((n_past, width), BF16), 8 * _nbytes((dec_seq * N_HEADS, n_past), F32)])
    return pl.pallas_call(
        _attn_sample_kernel,
        grid=(n_seq,),
        in_specs=[pl.BlockSpec(memory_space=pltpu.SMEM), new_blk, new_blk, new_blk,
                  past_blk, past_blk],
        out_specs=new_blk,
        out_shape=jax.ShapeDtypeStruct((n_seq, dec_seq, width), F32),
        scratch_shapes=[pltpu.VMEM((V7X_LANES, width), F32)] * 2,
        compiler_params=pltpu.CompilerParams(dimension_semantics=("arbitrary",),
                                             vmem_limit_bytes=limit),
        name="attn_sample",
    )(slopes, q, kn, vn, kst, vst)


def _ffn(x, attn_o, conv_o, wo, g2, wg, wu, wd, gf, final_norm, tm):
    rows, d_model = x.shape
    d_ff = wg.shape[1]
    row_blk = lambda width: pl.BlockSpec((tm, width), lambda i: (i, 0))
    limit = _vmem_limit(
        [2 * _nbytes((tm, d_model), F32), _nbytes((tm, attn_o.shape[1]), attn_o.dtype),
         _nbytes((tm, conv_o.shape[1]), conv_o.dtype)],
        [_nbytes(w.shape, BF16) for w in (wo, wg, wu, wd)],
        [4 * _nbytes((tm, d_model), F32), 3 * _nbytes((tm, d_ff), F32)])
    return pl.pallas_call(
        functools.partial(_ffn_kernel, final_norm=final_norm),
        grid=(rows // tm,),
        in_specs=[row_blk(d_model), row_blk(attn_o.shape[1]), row_blk(conv_o.shape[1]),
                  _resident(wo.shape), _resident((1, d_model)), _resident(wg.shape),
                  _resident(wu.shape), _resident(wd.shape), _resident((1, d_model))],
        out_specs=row_blk(d_model),
        out_shape=jax.ShapeDtypeStruct((rows, d_model), F32),
        compiler_params=pltpu.CompilerParams(dimension_semantics=("arbitrary",),
                                             vmem_limit_bytes=limit),
        name="ffn",
    )(x, attn_o, conv_o, wo, g2, wg, wu, wd, gf)


def _alibi_slopes():
    return jnp.exp2(-8.0 * jnp.arange(1, N_HEADS + 1, dtype=F32) / N_HEADS)


def kernel(x_prompt, x_sample, state_attn_k, state_attn_v, state_conv, norm_mix_g, w_in, w_conv,
           w_out, norm_ffn_g, w_gate, w_up, w_down, norm_final_g):
    depth = w_in.shape[0]
    batch, seq, d_model = x_prompt.shape
    dec_batch, dec_seq, _ = x_sample.shape
    d_conv = w_conv.shape[2]
    n_past = state_attn_k.shape[2]
    assert n_past >= max(WINDOWS) and seq <= max(WINDOWS) and seq % TQ == 0
    assert CONV_WIDTH - 1 <= dec_seq <= V7X_LANES
    slopes = _alibi_slopes()
    gf = norm_final_g.reshape(1, d_model)

    yp = x_prompt.reshape(batch * seq, d_model)
    ys = x_sample.reshape(dec_batch * dec_seq, d_model)
    outs = [[] for _ in range(6)]
    for layer in range(depth):
        g1 = norm_mix_g[layer].reshape(1, d_model)
        g2 = norm_ffn_g[layer].reshape(1, d_model)
        w_in_bf = w_in[layer].astype(BF16)
        w_qkv = w_in_bf[:, :3 * D_ATTN]
        w_cc = w_in_bf[:, 3 * D_ATTN:]
        weights = [w.astype(BF16) for w in (w_out[layer], w_gate[layer], w_up[layer], w_down[layer])]
        wc = w_conv[layer]

        qt, kt, vt, conv_o, u_last = _inproj_prompt(yp, g1, w_cc, w_qkv.T, wc, seq)
        attn_o = _attn_prompt(slopes, qt, kt, vt)
        y_layer = _ffn(yp, attn_o, conv_o, weights[0], g2, *weights[1:], gf,
                       layer == depth - 1, _row_tile(seq, largest=FFN_ROWS))
        to_heads = lambda a: a.reshape(batch, N_HEADS, HEAD_DIM, seq).transpose(0, 3, 1, 2)
        outs[0].append(to_heads(kt))
        outs[1].append(to_heads(vt))
        outs[2].append(u_last[:, V7X_SUBLANES - (CONV_WIDTH - 1):, :])
        yp = y_layer

        st = state_conv[layer]
        prev1 = jnp.pad(st[:, 1:2], ((0, 0), (0, dec_seq - 1), (0, 0)))
        prev2 = jnp.pad(st, ((0, 0), (0, dec_seq - 2), (0, 0)))
        qs, ks, vs, conv_s, u_s = _inproj_sample(
            ys, g1, w_cc, w_qkv, wc, prev1.reshape(-1, d_conv), prev2.reshape(-1, d_conv), dec_seq)
        as3 = lambda a: a.reshape(dec_batch, dec_seq, D_ATTN)
        feature_major = lambda a: a.transpose(0, 2, 3, 1).reshape(dec_batch, D_ATTN, n_past)
        attn_s = _attn_sample(slopes, as3(qs), as3(ks), as3(vs),
                              feature_major(state_attn_k[layer]),
                              feature_major(state_attn_v[layer]))
        ys = _ffn(ys, attn_s.reshape(-1, D_ATTN), conv_s, weights[0], g2, *weights[1:], gf,
                  layer == depth - 1, ys.shape[0])
        outs[3].append(ks.reshape(dec_batch, dec_seq, N_HEADS, HEAD_DIM))
        outs[4].append(vs.reshape(dec_batch, dec_seq, N_HEADS, HEAD_DIM))
        outs[5].append(u_s.reshape(dec_batch, dec_seq, d_conv)[:, dec_seq - (CONV_WIDTH - 1):])

    y_prompt = yp.reshape(batch, seq, d_model)
    y_sample = ys.reshape(dec_batch, dec_seq, d_model)
    new_k_p, new_v_p, new_c_p, new_k_s, new_v_s, new_c_s = [jnp.stack(o) for o in outs]
    return (y_prompt, y_sample, new_k_p, new_v_p, new_c_p, new_k_s, new_v_s, new_c_s)
```

```python
import functools

import jax
import jax.numpy as jnp
import numpy as np
from jax import lax
from jax.experimental import pallas as pl
from jax.experimental.pallas import tpu as pltpu

F32 = jnp.float32
BF16 = jnp.bfloat16

HEAD_DIM = 64
N_HEADS = 8
D_ATTN = N_HEADS * HEAD_DIM
CONV_WIDTH = 3
WINDOWS = (128, 512, 2048)
DILATIONS = (1, 4, 16)
RMS_EPS = 1e-6
ATTN_SCALE = HEAD_DIM ** -0.5
LOG2_E = 1.4426950408889634
NEG_INF = -1e30

V7X_LANES = 128
V7X_SUBLANES = 8
V7X_VMEM_BYTES = 64 * 1024 * 1024

HEADS_PER_STEP = V7X_LANES // HEAD_DIM
TQ = 256
HALF_TQ = TQ // 2
TK = 512
ATTN_STREAMS = 4
ATTN_LOOKAHEAD = 1
ROW_GROUPS = 4
FFN_ROWS = 1024
INPROJ_ROW_GROUPS = 8
INPROJ_ROWS = 2048
STATE_BUFFERS = 3


def _rms(x, g):
    y = x * lax.rsqrt(jnp.mean(x * x, axis=-1, keepdims=True) + RMS_EPS)
    return y * g


def _branch_count(delta):
    nonneg = delta >= 0
    c = jnp.zeros(delta.shape, F32)
    for w, d in zip(WINDOWS, DILATIONS):
        hit = nonneg & ((delta & (d - 1)) == 0) & (delta <= w)
        c = c + hit.astype(F32)
    return c


def _dot_nt(a, b):
    return lax.dot_general(a, b, (((1,), (1,)), ((), ())), preferred_element_type=F32)


def _project(x, g, wcc_ref):
    d_conv = wcc_ref.shape[1] // 3
    xn = _rms(x, g).astype(BF16)
    zc = jnp.dot(xn, wcc_ref[...], preferred_element_type=F32)
    return xn, zc[:, 0:d_conv], zc[:, d_conv:2 * d_conv], zc[:, 2 * d_conv:3 * d_conv]


def _conv3(wc, u2, u1, u0):
    acc = wc[0:1, :] * u2
    acc = acc + wc[1:2, :] * u1
    return acc + wc[2:3, :] * u0


def _inproj_prompt_kernel(x_ref, g_ref, wcc_ref, wqkvt_ref, wc_ref, qt_ref, kt_ref, vt_ref, co_ref,
                          ulast_ref, uext_ref):
    tm = x_ref.shape[0]
    j = pl.program_id(1)
    tiles_per_seq = pl.num_programs(1)

    @pl.when(j == 0)
    def _():
        uext_ref[0:V7X_SUBLANES, :] = jnp.zeros((V7X_SUBLANES, uext_ref.shape[1]), F32)

    rg = tm // INPROJ_ROW_GROUPS
    for r0 in range(0, tm, rg):
        xn, hc, gb, gc = _project(x_ref[r0:r0 + rg, :], g_ref[...], wcc_ref)
        qkvt = _dot_nt(wqkvt_ref[...], xn)
        qt_ref[0, :, r0:r0 + rg] = (qkvt[0:D_ATTN, :] * (ATTN_SCALE * LOG2_E)).astype(qt_ref.dtype)
        kt_ref[0, :, r0:r0 + rg] = qkvt[D_ATTN:2 * D_ATTN, :]
        vt_ref[0, :, r0:r0 + rg] = qkvt[2 * D_ATTN:3 * D_ATTN, :]
        u = gc * hc
        uext_ref[V7X_SUBLANES:V7X_SUBLANES + rg, :] = u
        u1 = uext_ref[V7X_SUBLANES - 1:V7X_SUBLANES - 1 + rg, :]
        u2 = uext_ref[V7X_SUBLANES - 2:V7X_SUBLANES - 2 + rg, :]
        co_ref[r0:r0 + rg, :] = (gb * _conv3(wc_ref[...], u2, u1, u)).astype(co_ref.dtype)
        tail = u[rg - V7X_SUBLANES:rg, :]
        uext_ref[0:V7X_SUBLANES, :] = tail

    @pl.when(j == tiles_per_seq - 1)
    def _():
        ulast_ref[0] = tail


def _inproj_sample_kernel(x_ref, g_ref, wcc_ref, wqkv_ref, wc_ref, p1_ref, p2_ref, q_ref, k_ref,
                          v_ref, co_ref, u_ref, uext_ref, *, dec_seq):
    rows = x_ref.shape[0]
    xn, hc, gb, gc = _project(x_ref[...], g_ref[...], wcc_ref)
    qkv = jnp.dot(xn, wqkv_ref[...], preferred_element_type=F32)
    q_ref[...] = qkv[:, 0:D_ATTN] * ATTN_SCALE
    k_ref[...] = qkv[:, D_ATTN:2 * D_ATTN]
    v_ref[...] = qkv[:, 2 * D_ATTN:3 * D_ATTN]
    u = gc * hc
    u_ref[...] = u
    uext_ref[0:V7X_SUBLANES, :] = jnp.zeros((V7X_SUBLANES, u.shape[1]), F32)
    uext_ref[V7X_SUBLANES:V7X_SUBLANES + rows, :] = u
    t = lax.broadcasted_iota(jnp.int32, u.shape, 0) % dec_seq
    u1 = jnp.where(t >= 1, uext_ref[V7X_SUBLANES - 1:V7X_SUBLANES - 1 + rows, :], p1_ref[...])
    u2 = jnp.where(t >= 2, uext_ref[V7X_SUBLANES - 2:V7X_SUBLANES - 2 + rows, :], p2_ref[...])
    co_ref[...] = (gb * _conv3(wc_ref[...], u2, u1, u)).astype(co_ref.dtype)


def _attn_tables(seq):
    x = np.arange(seq, dtype=np.int32)[:, None]
    i = np.arange(TQ, dtype=np.int32)[None, :]
    delta = i + (seq - TQ) - x
    c = np.zeros(delta.shape, np.float32)
    for w, d in zip(WINDOWS, DILATIONS):
        c += (delta >= 0) & (delta % d == 0) & (delta <= w)
    logc = np.where(c > 0.0, np.log2(np.maximum(c, 1.0)), NEG_INF).astype(np.float32)
    return delta.astype(np.float32), logc


def _attn_prompt_kernel(slopes_ref, dist_ref, logc_ref, qt_ref, kt_ref, vt_ref, o_ref,
                        kb_ref, vtb_ref, bm_ref):
    seq = qt_ref.shape[2]
    pair = pl.program_id(0)
    feature = lax.broadcasted_iota(jnp.int32, (V7X_LANES, 1), 0)
    first_head = feature < HEAD_DIM

    @pl.when(pl.program_id(1) == 0)
    def _():
        for h in range(HEADS_PER_STEP):
            slope = slopes_ref[HEADS_PER_STEP * pair + h]
            bm_ref[h] = logc_ref[...] - (slope * LOG2_E) * dist_ref[...]

    kb_ref[...] = kt_ref[0].T.astype(BF16)
    vtb_ref[...] = vt_ref[0].astype(BF16)

    def score_half(st, chunk):
        k0, n = chunk
        h, row0 = st["h"], seq - (st["q0"] + TQ)
        s = (jnp.dot(kb_ref[k0:k0 + n, :], st["q"], preferred_element_type=F32)
             + bm_ref[h, row0 + k0:row0 + k0 + n, :])
        tail = HALF_TQ if k0 + n == st["q0"] + TQ else 0
        main = s[0:n - tail, :]
        c8 = jnp.max(main.reshape((n - tail) // V7X_SUBLANES, V7X_SUBLANES, TQ), axis=0)
        s_tail = None
        if tail:
            s_tail = s[n - tail:n, HALF_TQ:TQ]
            t8 = jnp.max(s_tail.reshape(tail // V7X_SUBLANES, V7X_SUBLANES, HALF_TQ), axis=0)
            c8 = jnp.concatenate([c8[:, 0:HALF_TQ], jnp.maximum(c8[:, HALF_TQ:TQ], t8)], axis=1)
        m_new = jnp.maximum(st["m"], jnp.max(c8, axis=0, keepdims=True))
        pending = dict(k0=k0, n=n, s=main, s_tail=s_tail, m=m_new,
                       alpha=jnp.exp2(st["m"] - m_new))
        st["m"] = m_new
        return pending

    def value_half(st, pending):
        h, k0, n = st["h"], pending["k0"], pending["n"]
        p = jnp.exp2(pending["s"] - pending["m"])
        l8 = jnp.sum(p.reshape(p.shape[0] // V7X_SUBLANES, V7X_SUBLANES, TQ), axis=0)
        p = p.astype(BF16)
        if pending["s_tail"] is not None:
            p_tail = jnp.exp2(pending["s_tail"] - pending["m"][:, HALF_TQ:TQ])
            t8 = jnp.sum(p_tail.reshape(HALF_TQ // V7X_SUBLANES, V7X_SUBLANES, HALF_TQ), axis=0)
            l8 = jnp.concatenate([l8[:, 0:HALF_TQ], l8[:, HALF_TQ:TQ] + t8], axis=1)
            p_tail = jnp.concatenate([jnp.zeros((HALF_TQ, HALF_TQ), BF16), p_tail.astype(BF16)],
                                     axis=1)
            p = jnp.concatenate([p, p_tail], axis=0)
        st["l8"] = pending["alpha"] * st["l8"] + l8
        vt_h = vtb_ref[h * HEAD_DIM:(h + 1) * HEAD_DIM, k0:k0 + n]
        st["acc"] = pending["alpha"] * st["acc"] + jnp.dot(vt_h, p, preferred_element_type=F32)

    def open_block(q0):
        qt = qt_ref[0, :, q0:q0 + TQ]
        zero = jnp.zeros_like(qt)
        chunks = [(max(k1 - TK, 0), min(TK, k1)) for k1 in range(q0 + TQ, 0, -TK)]
        block = dict(q0=q0, left=HEADS_PER_STEP * len(chunks))
        block["streams"] = [
            dict(block=block, h=h, q0=q0, q=qh, chunks=chunks, m=jnp.full((1, TQ), NEG_INF, F32),
                 l8=jnp.zeros((V7X_SUBLANES, TQ), F32), acc=jnp.zeros((HEAD_DIM, TQ), F32))
            for h, qh in enumerate([jnp.where(first_head, qt, zero), jnp.where(first_head, zero, qt)])]
        return block

    def retire(st, pending):
        value_half(st, pending)
        block = st["block"]
        block["left"] -= 1
        if block["left"] == 0:
            outs = [s["acc"] / jnp.sum(s["l8"], axis=0, keepdims=True) for s in block["streams"]]
            q0 = block["q0"]
            o_ref[q0:q0 + TQ, :] = jnp.concatenate(outs, axis=0).T.astype(o_ref.dtype)

    blocks_per_group = ATTN_STREAMS // HEADS_PER_STEP
    waiting = []
    for g0 in range(0, seq, TQ * blocks_per_group):
        streams = [st for q0 in range(g0, g0 + TQ * blocks_per_group, TQ)
                   for st in open_block(q0)["streams"]]
        for t in range(max(len(st["chunks"]) for st in streams)):
            live = [(st, st["chunks"][t]) for st in streams if t < len(st["chunks"])]
            due = waiting.pop(0) if len(waiting) >= ATTN_LOOKAHEAD else []
            issued = []
            for i, (st, chunk) in enumerate(live):
                issued.append((st, score_half(st, chunk)))
                if i < len(due):
                    retire(*due[i])
            for st, pending in due[len(live):]:
                retire(st, pending)
            waiting.append(issued)
    for step in waiting:
        for st, pending in step:
            retire(st, pending)


def _attn_sample_kernel(slopes_ref, q_ref, kn_ref, vn_ref, kst_hbm, vst_hbm, o_ref,
                        knew_ref, vnew_ref, kbuf_ref, vbuf_ref, sem_ref):
    dec_seq = q_ref.shape[1]
    n_seq, _, n_past = kst_hbm.shape
    width = q_ref.shape[2]
    n_rows = dec_seq * N_HEADS
    b = pl.program_id(0)

    def state_copies(seq_idx, slot):
        return (pltpu.make_async_copy(kst_hbm.at[seq_idx], kbuf_ref.at[slot], sem_ref.at[0, slot]),
                pltpu.make_async_copy(vst_hbm.at[seq_idx], vbuf_ref.at[slot], sem_ref.at[1, slot]))

    @pl.when(b == 0)
    def _():
        for ahead in range(STATE_BUFFERS - 1):
            for copy in state_copies(ahead, ahead):
                copy.start()

    nxt = b + (STATE_BUFFERS - 1)

    @pl.when(nxt < n_seq)
    def _():
        for copy in state_copies(nxt, nxt % STATE_BUFFERS):
            copy.start()

    slot = b % STATE_BUFFERS
    for copy in state_copies(b, slot):
        copy.wait()
    kst_ref = kbuf_ref.at[slot]
    vst_ref = vbuf_ref.at[slot]

    q = q_ref[0]
    head_of_lane = lax.broadcasted_iota(jnp.int32, (N_HEADS, width), 1) // HEAD_DIM
    head_of_row = lax.broadcasted_iota(jnp.int32, (N_HEADS, width), 0)
    head_mask = head_of_lane == head_of_row
    wt = jnp.concatenate(
        [jnp.where(head_mask, jnp.broadcast_to(q[t:t + 1, :], (N_HEADS, width)), 0.0)
         for t in range(dec_seq)], axis=0).astype(BF16)

    knew_ref[...] = jnp.zeros(knew_ref.shape, F32)
    vnew_ref[...] = jnp.zeros(vnew_ref.shape, F32)
    knew_ref[0:dec_seq, :] = kn_ref[0]
    vnew_ref[0:dec_seq, :] = vn_ref[0]

    s_past = jnp.dot(wt, kst_ref[...].astype(BF16), preferred_element_type=F32)
    s_new = _dot_nt(wt, knew_ref[...].astype(BF16))

    row = lax.broadcasted_iota(jnp.int32, (n_rows, 1), 0)
    step = row // N_HEADS
    slope_col = jnp.zeros((n_rows, 1), F32)
    for h in range(N_HEADS):
        slope_col = jnp.where(row % N_HEADS == h, slopes_ref[h], slope_col)

    def weigh(s, key_pos):
        delta = n_past + step - key_pos
        c = _branch_count(delta)
        s = jnp.where(c > 0.0, s - slope_col * delta.astype(F32), NEG_INF)
        return s, c

    s_past, c_past = weigh(s_past, lax.broadcasted_iota(jnp.int32, (1, n_past), 1))
    s_new, c_new = weigh(s_new, n_past + lax.broadcasted_iota(jnp.int32, (1, V7X_LANES), 1))
    m = jnp.maximum(jnp.max(s_past, axis=-1, keepdims=True),
                    jnp.max(s_new, axis=-1, keepdims=True))
    p_past = c_past * jnp.exp(s_past - m)
    p_new = c_new * jnp.exp(s_new - m)
    l = jnp.sum(p_past, axis=-1, keepdims=True) + jnp.sum(p_new, axis=-1, keepdims=True)
    o_all = (_dot_nt(p_past.astype(BF16), vst_ref[...].astype(BF16))
             + jnp.dot(p_new.astype(BF16), vnew_ref[...].astype(BF16),
                       preferred_element_type=F32)) / l
    outs = []
    for t in range(dec_seq):
        blk = o_all[t * N_HEADS:(t + 1) * N_HEADS, :]
        outs.append(jnp.sum(jnp.where(head_mask, blk, 0.0), axis=0, keepdims=True))
    o_ref[0] = jnp.concatenate(outs, axis=0).astype(o_ref.dtype)


def _ffn_kernel(x_ref, a_ref, c_ref, wo_ref, g2_ref, wg_ref, wu_ref, wd_ref, gf_ref, y_ref, *,
                final_norm):
    tm = x_ref.shape[0]
    rg = tm // ROW_GROUPS if tm % (ROW_GROUPS * V7X_LANES) == 0 else tm
    groups = [slice(r0, r0 + rg) for r0 in range(0, tm, rg)]
    hs = []
    for rows in groups:
        mix = jnp.concatenate([a_ref[rows, :].astype(BF16), c_ref[rows, :].astype(BF16)], axis=-1)
        hs.append(x_ref[rows, :] + jnp.dot(mix, wo_ref[...], preferred_element_type=F32))
    acts = []
    for h in hs:
        hn = _rms(h, g2_ref[...]).astype(BF16)
        gate = jnp.dot(hn, wg_ref[...], preferred_element_type=F32)
        up = jnp.dot(hn, wu_ref[...], preferred_element_type=F32)
        acts.append(((gate * (1.0 / (1.0 + jnp.exp(-gate)))) * up).astype(BF16))
    for rows, h, act in zip(groups, hs, acts):
        y = h + jnp.dot(act, wd_ref[...], preferred_element_type=F32)
        y_ref[rows, :] = _rms(y, gf_ref[...]) if final_norm else y


def _nbytes(shape, dtype):
    n = 1
    for s in shape:
        n *= s
    return n * jnp.dtype(dtype).itemsize


def _vmem_limit(pipelined, resident, temporaries):
    need = 2 * sum(pipelined) + sum(resident) + sum(temporaries)
    return min(V7X_VMEM_BYTES, need + need // 4)


def _resident(shape):
    zeros = (0,) * len(shape)
    return pl.BlockSpec(shape, lambda *_: zeros, pipeline_mode=pl.Buffered(1))


def _row_tile(rows, largest=512):
    for tm in (largest, 512, 256, 128):
        if rows % tm == 0:
            return tm
    raise ValueError(f"row count {rows} is not a multiple of 128")


def _inproj_prompt(x, g, w_cc, w_qkvt, wc, seq):
    rows, d_model = x.shape
    d_conv = wc.shape[1]
    tm = _row_tile(seq, largest=INPROJ_ROWS)
    tiles_per_seq = seq // tm
    n_seq = rows // seq
    row_blk = lambda width: pl.BlockSpec((tm, width), lambda b, j: (b * tiles_per_seq + j, 0))
    col_blk = pl.BlockSpec((1, D_ATTN, tm), lambda b, j: (b, 0, j))
    limit = _vmem_limit(
        [_nbytes((tm, d_model), F32), _nbytes((D_ATTN, tm), BF16), 2 * _nbytes((D_ATTN, tm), F32),
         _nbytes((tm, d_conv), BF16)],
        [_nbytes(w_cc.shape, BF16), _nbytes(w_qkvt.shape, BF16)],
        [2 * _nbytes((tm, d_model), F32), 2 * _nbytes((tm, w_cc.shape[1]), F32),
         2 * _nbytes((w_qkvt.shape[0], tm), F32), 4 * _nbytes((tm, d_conv), F32)])
    return pl.pallas_call(
        _inproj_prompt_kernel,
        grid=(n_seq, tiles_per_seq),
        in_specs=[row_blk(d_model), _resident((1, d_model)), _resident(w_cc.shape),
                  _resident(w_qkvt.shape), _resident(wc.shape)],
        out_specs=[col_blk, col_blk, col_blk, row_blk(d_conv),
                   pl.BlockSpec((1, V7X_SUBLANES, d_conv), lambda b, j: (b, 0, 0))],
        out_shape=[jax.ShapeDtypeStruct((n_seq, D_ATTN, seq), BF16),
                   jax.ShapeDtypeStruct((n_seq, D_ATTN, seq), F32),
                   jax.ShapeDtypeStruct((n_seq, D_ATTN, seq), F32),
                   jax.ShapeDtypeStruct((rows, d_conv), BF16),
                   jax.ShapeDtypeStruct((n_seq, V7X_SUBLANES, d_conv), F32)],
        scratch_shapes=[pltpu.VMEM((tm + V7X_SUBLANES, d_conv), F32)],
        compiler_params=pltpu.CompilerParams(dimension_semantics=("arbitrary", "arbitrary"),
                                             vmem_limit_bytes=limit),
        name="inproj_prompt",
    )(x, g, w_cc, w_qkvt, wc)


def _inproj_sample(x, g, w_cc, w_qkv, wc, prev1, prev2, dec_seq):
    rows, d_model = x.shape
    d_conv = wc.shape[1]
    full = lambda shape: pl.BlockSpec(shape, lambda i: (0,) * len(shape))
    return pl.pallas_call(
        functools.partial(_inproj_sample_kernel, dec_seq=dec_seq),
        grid=(1,),
        in_specs=[full(x.shape), full((1, d_model)), full(w_cc.shape), full(w_qkv.shape),
                  full(wc.shape), full(prev1.shape), full(prev2.shape)],
        out_specs=[full((rows, D_ATTN))] * 3 + [full((rows, d_conv))] * 2,
        out_shape=[jax.ShapeDtypeStruct((rows, D_ATTN), F32)] * 3
        + [jax.ShapeDtypeStruct((rows, d_conv), BF16), jax.ShapeDtypeStruct((rows, d_conv), F32)],
        scratch_shapes=[pltpu.VMEM((rows + V7X_SUBLANES, d_conv), F32)],
        compiler_params=pltpu.CompilerParams(dimension_semantics=("arbitrary",)),
        name="inproj_sample",
    )(x, g, w_cc, w_qkv, wc, prev1, prev2)


def _attn_prompt(slopes, qt, kt, vt):
    n_seq, _, seq = qt.shape
    rows = n_seq * seq
    o_blk = pl.BlockSpec((seq, V7X_LANES), lambda p, b: (b, p))
    kv_blk = pl.BlockSpec((1, V7X_LANES, seq), lambda p, b: (b, p, 0))
    dist, logc = _attn_tables(seq)
    table = _nbytes((TQ, seq), F32)
    limit = _vmem_limit(
        [2 * _nbytes((seq, V7X_LANES), F32), 2 * _nbytes((seq, V7X_LANES), BF16)],
        [2 * _nbytes((seq, V7X_LANES), BF16), (2 + HEADS_PER_STEP) * table],
        [3 * HEADS_PER_STEP * table])
    return pl.pallas_call(
        _attn_prompt_kernel,
        grid=(N_HEADS // HEADS_PER_STEP, n_seq),
        in_specs=[pl.BlockSpec(memory_space=pltpu.SMEM), _resident((seq, TQ)),
                  _resident((seq, TQ)), kv_blk, kv_blk, kv_blk],
        out_specs=o_blk,
        out_shape=jax.ShapeDtypeStruct((rows, D_ATTN), BF16),
        scratch_shapes=[pltpu.VMEM((seq, V7X_LANES), BF16), pltpu.VMEM((V7X_LANES, seq), BF16),
                        pltpu.VMEM((HEADS_PER_STEP, seq, TQ), F32)],
        compiler_params=pltpu.CompilerParams(dimension_semantics=("arbitrary", "arbitrary"),
                                             vmem_limit_bytes=limit),
        name="attn_prompt",
    )(slopes, dist, logc, qt, kt, vt)


def _attn_sample(slopes, q, kn, vn, kst, vst):
    n_seq, dec_seq, width = q.shape
    n_past = kst.shape[2]
    new_blk = pl.BlockSpec((1, dec_seq, width), lambda b: (b, 0, 0))
    past_blk = pl.BlockSpec(memory_space=pl.ANY)
    limit = _vmem_limit(
        [], [2 * STATE_BUFFERS * _nbytes((n_past, width), F32)],
        [2 * _nbytes((n_past, width), BF16), 8 * _nbytes((dec_seq * N_HEADS, n_past), F32)])
    return pl.pallas_call(
        _attn_sample_kernel,
        grid=(n_seq,),
        in_specs=[pl.BlockSpec(memory_space=pltpu.SMEM), new_blk, new_blk, new_blk,
                  past_blk, past_blk],
        out_specs=new_blk,
        out_shape=jax.ShapeDtypeStruct((n_seq, dec_seq, width), F32),
        scratch_shapes=[pltpu.VMEM((V7X_LANES, width), F32)] * 2
        + [pltpu.VMEM((STATE_BUFFERS, width, n_past), F32)] * 2
        + [pltpu.SemaphoreType.DMA((2, STATE_BUFFERS))],
        compiler_params=pltpu.CompilerParams(dimension_semantics=("arbitrary",),
                                             vmem_limit_bytes=limit),
        name="attn_sample",
    )(slopes, q, kn, vn, kst, vst)


def _ffn(x, attn_o, conv_o, wo, g2, wg, wu, wd, gf, final_norm, tm):
    rows, d_model = x.shape
    d_ff = wg.shape[1]
    row_blk = lambda width: pl.BlockSpec((tm, width), lambda i: (i, 0))
    limit = _vmem_limit(
        [2 * _nbytes((tm, d_model), F32), _nbytes((tm, attn_o.shape[1]), attn_o.dtype),
         _nbytes((tm, conv_o.shape[1]), conv_o.dtype)],
        [_nbytes(w.shape, BF16) for w in (wo, wg, wu, wd)],
        [4 * _nbytes((tm, d_model), F32), 3 * _nbytes((tm, d_ff), F32)])
    return pl.pallas_call(
        functools.partial(_ffn_kernel, final_norm=final_norm),
        grid=(rows // tm,),
        in_specs=[row_blk(d_model), row_blk(attn_o.shape[1]), row_blk(conv_o.shape[1]),
                  _resident(wo.shape), _resident((1, d_model)), _resident(wg.shape),
                  _resident(wu.shape), _resident(wd.shape), _resident((1, d_model))],
        out_specs=row_blk(d_model),
        out_shape=jax.ShapeDtypeStruct((rows, d_model), F32),
        compiler_params=pltpu.CompilerParams(dimension_semantics=("arbitrary",),
                                             vmem_limit_bytes=limit),
        name="ffn",
    )(x, attn_o, conv_o, wo, g2, wg, wu, wd, gf)


def _alibi_slopes():
    return jnp.exp2(-8.0 * jnp.arange(1, N_HEADS + 1, dtype=F32) / N_HEADS)


def kernel(x_prompt, x_sample, state_attn_k, state_attn_v, state_conv, norm_mix_g, w_in, w_conv,
           w_out, norm_ffn_g, w_gate, w_up, w_down, norm_final_g):
    depth = w_in.shape[0]
    batch, seq, d_model = x_prompt.shape
    dec_batch, dec_seq, _ = x_sample.shape
    d_conv = w_conv.shape[2]
    n_past = state_attn_k.shape[2]
    assert n_past >= max(WINDOWS) and seq <= max(WINDOWS) and seq % TQ == 0
    assert CONV_WIDTH - 1 <= dec_seq <= V7X_LANES
    slopes = _alibi_slopes()
    gf = norm_final_g.reshape(1, d_model)

    yp = x_prompt.reshape(batch * seq, d_model)
    ys = x_sample.reshape(dec_batch * dec_seq, d_model)
    outs = [[] for _ in range(6)]
    for layer in range(depth):
        g1 = norm_mix_g[layer].reshape(1, d_model)
        g2 = norm_ffn_g[layer].reshape(1, d_model)
        w_in_bf = w_in[layer].astype(BF16)
        w_qkv = w_in_bf[:, :3 * D_ATTN]
        w_cc = w_in_bf[:, 3 * D_ATTN:]
        weights = [w.astype(BF16) for w in (w_out[layer], w_gate[layer], w_up[layer], w_down[layer])]
        wc = w_conv[layer]

        qt, kt, vt, conv_o, u_last = _inproj_prompt(yp, g1, w_cc, w_qkv.T, wc, seq)
        attn_o = _attn_prompt(slopes, qt, kt, vt)
        y_layer = _ffn(yp, attn_o, conv_o, weights[0], g2, *weights[1:], gf,
                       layer == depth - 1, _row_tile(seq, largest=FFN_ROWS))
        to_heads = lambda a: a.reshape(batch, N_HEADS, HEAD_DIM, seq).transpose(0, 3, 1, 2)
        outs[0].append(to_heads(kt))
        outs[1].append(to_heads(vt))
        outs[2].append(u_last[:, V7X_SUBLANES - (CONV_WIDTH - 1):, :])
        yp = y_layer

        st = state_conv[layer]
        prev1 = jnp.pad(st[:, 1:2], ((0, 0), (0, dec_seq - 1), (0, 0)))
        prev2 = jnp.pad(st, ((0, 0), (0, dec_seq - 2), (0, 0)))
        qs, ks, vs, conv_s, u_s = _inproj_sample(
            ys, g1, w_cc, w_qkv, wc, prev1.reshape(-1, d_conv), prev2.reshape(-1, d_conv), dec_seq)
        as3 = lambda a: a.reshape(dec_batch, dec_seq, D_ATTN)
        feature_major = lambda a: a.transpose(0, 2, 3, 1).reshape(dec_batch, D_ATTN, n_past)
        attn_s = _attn_sample(slopes, as3(qs), as3(ks), as3(vs),
                              feature_major(state_attn_k[layer]),
                              feature_major(state_attn_v[layer]))
        ys = _ffn(ys, attn_s.reshape(-1, D_ATTN), conv_s, weights[0], g2, *weights[1:], gf,
                  layer == depth - 1, ys.shape[0])
        outs[3].append(ks.reshape(dec_batch, dec_seq, N_HEADS, HEAD_DIM))
        outs[4].append(vs.reshape(dec_batch, dec_seq, N_HEADS, HEAD_DIM))
        outs[5].append(u_s.reshape(dec_batch, dec_seq, d_conv)[:, dec_seq - (CONV_WIDTH - 1):])

    y_prompt = yp.reshape(batch, seq, d_model)
    y_sample = ys.reshape(dec_batch, dec_seq, d_model)
    new_k_p, new_v_p, new_c_p, new_k_s, new_v_s, new_c_s = [jnp.stack(o) for o in outs]
    return (y_prompt, y_sample, new_k_p, new_v_p, new_c_p, new_k_s, new_v_s, new_c_s)
```

```python
import functools

import jax
import jax.numpy as jnp
import numpy as np
from jax import lax
from jax.experimental import pallas as pl
from jax.experimental.pallas import tpu as pltpu

F32 = jnp.float32
BF16 = jnp.bfloat16

HEAD_DIM = 64
N_HEADS = 8
D_ATTN = N_HEADS * HEAD_DIM
CONV_WIDTH = 3
WINDOWS = (128, 512, 2048)
DILATIONS = (1, 4, 16)
RMS_EPS = 1e-6
ATTN_SCALE = HEAD_DIM ** -0.5
LOG2_E = 1.4426950408889634
NEG_INF = -1e30

V7X_LANES = 128
V7X_SUBLANES = 8
V7X_VMEM_BYTES = 64 * 1024 * 1024

HEADS_PER_STEP = V7X_LANES // HEAD_DIM
TQ = 256
HALF_TQ = TQ // 2
TK = 512
ATTN_STREAMS = 4
ATTN_LOOKAHEAD = 1
ROW_GROUPS = 4
FFN_ROWS = 1024
INPROJ_ROW_GROUPS = 8
INPROJ_ROWS = 2048
STATE_BUFFERS = 3


def _rms(x, g):
    y = x * lax.rsqrt(jnp.mean(x * x, axis=-1, keepdims=True) + RMS_EPS)
    return y * g


def _branch_count(delta):
    nonneg = delta >= 0
    c = jnp.zeros(delta.shape, F32)
    for w, d in zip(WINDOWS, DILATIONS):
        hit = nonneg & ((delta & (d - 1)) == 0) & (delta <= w)
        c = c + hit.astype(F32)
    return c


def _dot_nt(a, b):
    return lax.dot_general(a, b, (((1,), (1,)), ((), ())), preferred_element_type=F32)


def _project(x, g, wcc_ref):
    d_conv = wcc_ref.shape[1] // 3
    xn = _rms(x, g).astype(BF16)
    zc = jnp.dot(xn, wcc_ref[...], preferred_element_type=F32)
    return xn, zc[:, 0:d_conv], zc[:, d_conv:2 * d_conv], zc[:, 2 * d_conv:3 * d_conv]


def _conv3(wc, u2, u1, u0):
    acc = wc[0:1, :] * u2
    acc = acc + wc[1:2, :] * u1
    return acc + wc[2:3, :] * u0


def _inproj_prompt_kernel(x_ref, g_ref, wcc_ref, wqkvt_ref, wc_ref, qt_ref, kt_ref, vt_ref, co_ref,
                          ulast_ref, uext_ref):
    tm = x_ref.shape[0]
    j = pl.program_id(1)
    tiles_per_seq = pl.num_programs(1)

    @pl.when(j == 0)
    def _():
        uext_ref[0:V7X_SUBLANES, :] = jnp.zeros((V7X_SUBLANES, uext_ref.shape[1]), F32)

    rg = tm // INPROJ_ROW_GROUPS
    for r0 in range(0, tm, rg):
        xn, hc, gb, gc = _project(x_ref[r0:r0 + rg, :], g_ref[...], wcc_ref)
        qkvt = _dot_nt(wqkvt_ref[...], xn)
        qt_ref[0, :, r0:r0 + rg] = (qkvt[0:D_ATTN, :] * (ATTN_SCALE * LOG2_E)).astype(qt_ref.dtype)
        kt_ref[0, :, r0:r0 + rg] = qkvt[D_ATTN:2 * D_ATTN, :]
        vt_ref[0, :, r0:r0 + rg] = qkvt[2 * D_ATTN:3 * D_ATTN, :]
        u = gc * hc
        uext_ref[V7X_SUBLANES:V7X_SUBLANES + rg, :] = u
        u1 = uext_ref[V7X_SUBLANES - 1:V7X_SUBLANES - 1 + rg, :]
        u2 = uext_ref[V7X_SUBLANES - 2:V7X_SUBLANES - 2 + rg, :]
        co_ref[r0:r0 + rg, :] = (gb * _conv3(wc_ref[...], u2, u1, u)).astype(co_ref.dtype)
        tail = u[rg - V7X_SUBLANES:rg, :]
        uext_ref[0:V7X_SUBLANES, :] = tail

    @pl.when(j == tiles_per_seq - 1)
    def _():
        ulast_ref[0] = tail


def _inproj_sample_kernel(x_ref, g_ref, wcc_ref, wqkv_ref, wc_ref, p1_ref, p2_ref, q_ref, k_ref,
                          v_ref, co_ref, u_ref, uext_ref, *, dec_seq):
    rows = x_ref.shape[0]
    xn, hc, gb, gc = _project(x_ref[...], g_ref[...], wcc_ref)
    qkv = jnp.dot(xn, wqkv_ref[...], preferred_element_type=F32)
    q_ref[...] = qkv[:, 0:D_ATTN] * ATTN_SCALE
    k_ref[...] = qkv[:, D_ATTN:2 * D_ATTN]
    v_ref[...] = qkv[:, 2 * D_ATTN:3 * D_ATTN]
    u = gc * hc
    u_ref[...] = u
    uext_ref[0:V7X_SUBLANES, :] = jnp.zeros((V7X_SUBLANES, u.shape[1]), F32)
    uext_ref[V7X_SUBLANES:V7X_SUBLANES + rows, :] = u
    t = lax.broadcasted_iota(jnp.int32, u.shape, 0) % dec_seq
    u1 = jnp.where(t >= 1, uext_ref[V7X_SUBLANES - 1:V7X_SUBLANES - 1 + rows, :], p1_ref[...])
    u2 = jnp.where(t >= 2, uext_ref[V7X_SUBLANES - 2:V7X_SUBLANES - 2 + rows, :], p2_ref[...])
    co_ref[...] = (gb * _conv3(wc_ref[...], u2, u1, u)).astype(co_ref.dtype)


def _attn_tables(seq):
    x = np.arange(seq, dtype=np.int32)[:, None]
    i = np.arange(TQ, dtype=np.int32)[None, :]
    delta = i + (seq - TQ) - x
    c = np.zeros(delta.shape, np.float32)
    for w, d in zip(WINDOWS, DILATIONS):
        c += (delta >= 0) & (delta % d == 0) & (delta <= w)
    logc = np.where(c > 0.0, np.log2(np.maximum(c, 1.0)), NEG_INF).astype(np.float32)
    return delta.astype(np.float32), logc


def _attn_prompt_kernel(slopes_ref, dist_ref, logc_ref, qt_ref, kt_ref, vt_ref, o_ref,
                        kb_ref, vtb_ref, bm_ref):
    seq = qt_ref.shape[2]
    pair = pl.program_id(0)
    feature = lax.broadcasted_iota(jnp.int32, (V7X_LANES, 1), 0)
    first_head = feature < HEAD_DIM

    @pl.when(pl.program_id(1) == 0)
    def _():
        for h in range(HEADS_PER_STEP):
            slope = slopes_ref[HEADS_PER_STEP * pair + h]
            bm_ref[h] = logc_ref[...] - (slope * LOG2_E) * dist_ref[...]

    kb_ref[...] = kt_ref[0].T.astype(BF16)
    vtb_ref[...] = vt_ref[0].astype(BF16)

    def score_half(st, chunk):
        k0, n = chunk
        h, row0 = st["h"], seq - (st["q0"] + TQ)
        s = (jnp.dot(kb_ref[k0:k0 + n, :], st["q"], preferred_element_type=F32)
             + bm_ref[h, row0 + k0:row0 + k0 + n, :])
        tail = HALF_TQ if k0 + n == st["q0"] + TQ else 0
        main = s[0:n - tail, :]
        c8 = jnp.max(main.reshape((n - tail) // V7X_SUBLANES, V7X_SUBLANES, TQ), axis=0)
        s_tail = None
        if tail:
            s_tail = s[n - tail:n, HALF_TQ:TQ]
            t8 = jnp.max(s_tail.reshape(tail // V7X_SUBLANES, V7X_SUBLANES, HALF_TQ), axis=0)
            c8 = jnp.concatenate([c8[:, 0:HALF_TQ], jnp.maximum(c8[:, HALF_TQ:TQ], t8)], axis=1)
        m_new = jnp.maximum(st["m"], jnp.max(c8, axis=0, keepdims=True))
        pending = dict(k0=k0, n=n, s=main, s_tail=s_tail, m=m_new,
                       alpha=jnp.exp2(st["m"] - m_new))
        st["m"] = m_new
        return pending

    def value_half(st, pending):
        h, k0, n = st["h"], pending["k0"], pending["n"]
        p = jnp.exp2(pending["s"] - pending["m"])
        l8 = jnp.sum(p.reshape(p.shape[0] // V7X_SUBLANES, V7X_SUBLANES, TQ), axis=0)
        p = p.astype(BF16)
        if pending["s_tail"] is not None:
            p_tail = jnp.exp2(pending["s_tail"] - pending["m"][:, HALF_TQ:TQ])
            t8 = jnp.sum(p_tail.reshape(HALF_TQ // V7X_SUBLANES, V7X_SUBLANES, HALF_TQ), axis=0)
            l8 = jnp.concatenate([l8[:, 0:HALF_TQ], l8[:, HALF_TQ:TQ] + t8], axis=1)
            p_tail = jnp.concatenate([jnp.zeros((HALF_TQ, HALF_TQ), BF16), p_tail.astype(BF16)],
                                     axis=1)
            p = jnp.concatenate([p, p_tail], axis=0)
        st["l8"] = pending["alpha"] * st["l8"] + l8
        vt_h = vtb_ref[h * HEAD_DIM:(h + 1) * HEAD_DIM, k0:k0 + n]
        st["acc"] = pending["alpha"] * st["acc"] + jnp.dot(vt_h, p, preferred_element_type=F32)

    def open_block(q0):
        qt = qt_ref[0, :, q0:q0 + TQ]
        zero = jnp.zeros_like(qt)
        chunks = [(max(k1 - TK, 0), min(TK, k1)) for k1 in range(q0 + TQ, 0, -TK)]
        block = dict(q0=q0, left=HEADS_PER_STEP * len(chunks))
        block["streams"] = [
            dict(block=block, h=h, q0=q0, q=qh, chunks=chunks, m=jnp.full((1, TQ), NEG_INF, F32),
                 l8=jnp.zeros((V7X_SUBLANES, TQ), F32), acc=jnp.zeros((HEAD_DIM, TQ), F32))
            for h, qh in enumerate([jnp.where(first_head, qt, zero), jnp.where(first_head, zero, qt)])]
        return block

    def retire(st, pending):
        value_half(st, pending)
        block = st["block"]
        block["left"] -= 1
        if block["left"] == 0:
            outs = [s["acc"] / jnp.sum(s["l8"], axis=0, keepdims=True) for s in block["streams"]]
            q0 = block["q0"]
            o_ref[q0:q0 + TQ, :] = jnp.concatenate(outs, axis=0).T.astype(o_ref.dtype)

    blocks_per_group = ATTN_STREAMS // HEADS_PER_STEP
    waiting = []
    for g0 in range(0, seq, TQ * blocks_per_group):
        streams = [st for q0 in range(g0, g0 + TQ * blocks_per_group, TQ)
                   for st in open_block(q0)["streams"]]
        for t in range(max(len(st["chunks"]) for st in streams)):
            live = [(st, st["chunks"][t]) for st in streams if t < len(st["chunks"])]
            due = waiting.pop(0) if len(waiting) >= ATTN_LOOKAHEAD else []
            issued = []
            for i, (st, chunk) in enumerate(live):
                issued.append((st, score_half(st, chunk)))
                if i < len(due):
                    retire(*due[i])
            for st, pending in due[len(live):]:
                retire(st, pending)
            waiting.append(issued)
    for step in waiting:
        for st, pending in step:
            retire(st, pending)


def _attn_sample_kernel(slopes_ref, q_ref, kn_ref, vn_ref, kst_hbm, vst_hbm, o_ref,
                        knew_ref, vnew_ref, kbuf_ref, vbuf_ref, sem_ref):
    dec_seq = q_ref.shape[1]
    n_seq, _, n_past = kst_hbm.shape
    width = q_ref.shape[2]
    n_rows = dec_seq * N_HEADS
    b = pl.program_id(0)

    def state_copies(seq_idx, slot):
        return (pltpu.make_async_copy(kst_hbm.at[seq_idx], kbuf_ref.at[slot], sem_ref.at[0, slot]),
                pltpu.make_async_copy(vst_hbm.at[seq_idx], vbuf_ref.at[slot], sem_ref.at[1, slot]))

    q = q_ref[0]
    head_of_lane = lax.broadcasted_iota(jnp.int32, (N_HEADS, width), 1) // HEAD_DIM
    head_of_row = lax.broadcasted_iota(jnp.int32, (N_HEADS, width), 0)
    head_mask = head_of_lane == head_of_row
    wt = jnp.concatenate(
        [jnp.where(head_mask, jnp.broadcast_to(q[t:t + 1, :], (N_HEADS, width)), 0.0)
         for t in range(dec_seq)], axis=0).astype(BF16)

    knew_ref[...] = jnp.zeros(knew_ref.shape, F32)
    vnew_ref[...] = jnp.zeros(vnew_ref.shape, F32)
    knew_ref[0:dec_seq, :] = kn_ref[0]
    vnew_ref[0:dec_seq, :] = vn_ref[0]

    row = lax.broadcasted_iota(jnp.int32, (n_rows, 1), 0)
    step = row // N_HEADS
    slope_col = jnp.zeros((n_rows, 1), F32)
    for h in range(N_HEADS):
        slope_col = jnp.where(row % N_HEADS == h, slopes_ref[h], slope_col)

    def weights(key_pos):
        delta = n_past + step - key_pos
        return _branch_count(delta), slope_col * delta.astype(F32)

    def weigh(s, c, bias):
        return jnp.where(c > 0.0, s - bias, NEG_INF)

    c_past, bias_past = weights(lax.broadcasted_iota(jnp.int32, (1, n_past), 1))
    c_new, bias_new = weights(n_past + lax.broadcasted_iota(jnp.int32, (1, V7X_LANES), 1))
    s_new = weigh(_dot_nt(wt, knew_ref[...].astype(BF16)), c_new, bias_new)
    m_new = jnp.max(s_new, axis=-1, keepdims=True)

    @pl.when(b == 0)
    def _():
        for ahead in range(STATE_BUFFERS - 1):
            for copy in state_copies(ahead, ahead):
                copy.start()

    nxt = b + (STATE_BUFFERS - 1)

    @pl.when(nxt < n_seq)
    def _():
        for copy in state_copies(nxt, nxt % STATE_BUFFERS):
            copy.start()

    slot = b % STATE_BUFFERS
    k_copy, v_copy = state_copies(b, slot)
    k_copy.wait()
    s_past = jnp.dot(wt, kbuf_ref[slot].astype(BF16), preferred_element_type=F32)
    s_past = weigh(s_past, c_past, bias_past)
    m = jnp.maximum(jnp.max(s_past, axis=-1, keepdims=True), m_new)
    p_past = c_past * jnp.exp(s_past - m)
    p_new = c_new * jnp.exp(s_new - m)
    l = jnp.sum(p_past, axis=-1, keepdims=True) + jnp.sum(p_new, axis=-1, keepdims=True)
    v_copy.wait()
    o_all = (_dot_nt(p_past.astype(BF16), vbuf_ref[slot].astype(BF16))
             + jnp.dot(p_new.astype(BF16), vnew_ref[...].astype(BF16),
                       preferred_element_type=F32)) / l
    outs = []
    for t in range(dec_seq):
        blk = o_all[t * N_HEADS:(t + 1) * N_HEADS, :]
        outs.append(jnp.sum(jnp.where(head_mask, blk, 0.0), axis=0, keepdims=True))
    o_ref[0] = jnp.concatenate(outs, axis=0).astype(o_ref.dtype)


def _ffn_kernel(x_ref, a_ref, c_ref, wo_ref, g2_ref, wg_ref, wu_ref, wd_ref, gf_ref, y_ref, *,
                final_norm):
    tm = x_ref.shape[0]
    rg = tm // ROW_GROUPS if tm % (ROW_GROUPS * V7X_LANES) == 0 else tm
    groups = [slice(r0, r0 + rg) for r0 in range(0, tm, rg)]
    hs = []
    for rows in groups:
        mix = jnp.concatenate([a_ref[rows, :].astype(BF16), c_ref[rows, :].astype(BF16)], axis=-1)
        hs.append(x_ref[rows, :] + jnp.dot(mix, wo_ref[...], preferred_element_type=F32))
    acts = []
    for h in hs:
        hn = _rms(h, g2_ref[...]).astype(BF16)
        gate = jnp.dot(hn, wg_ref[...], preferred_element_type=F32)
        up = jnp.dot(hn, wu_ref[...], preferred_element_type=F32)
        acts.append(((gate * (1.0 / (1.0 + jnp.exp(-gate)))) * up).astype(BF16))
    for rows, h, act in zip(groups, hs, acts):
        y = h + jnp.dot(act, wd_ref[...], preferred_element_type=F32)
        y_ref[rows, :] = _rms(y, gf_ref[...]) if final_norm else y


def _nbytes(shape, dtype):
    n = 1
    for s in shape:
        n *= s
    return n * jnp.dtype(dtype).itemsize


def _vmem_limit(pipelined, resident, temporaries):
    need = 2 * sum(pipelined) + sum(resident) + sum(temporaries)
    return min(V7X_VMEM_BYTES, need + need // 4)


def _resident(shape):
    zeros = (0,) * len(shape)
    return pl.BlockSpec(shape, lambda *_: zeros, pipeline_mode=pl.Buffered(1))


def _row_tile(rows, largest=512):
    for tm in (largest, 512, 256, 128):
        if rows % tm == 0:
            return tm
    raise ValueError(f"row count {rows} is not a multiple of 128")


def _inproj_prompt(x, g, w_cc, w_qkvt, wc, seq):
    rows, d_model = x.shape
    d_conv = wc.shape[1]
    tm = _row_tile(seq, largest=INPROJ_ROWS)
    tiles_per_seq = seq // tm
    n_seq = rows // seq
    row_blk = lambda width: pl.BlockSpec((tm, width), lambda b, j: (b * tiles_per_seq + j, 0))
    col_blk = pl.BlockSpec((1, D_ATTN, tm), lambda b, j: (b, 0, j))
    limit = _vmem_limit(
        [_nbytes((tm, d_model), F32), _nbytes((D_ATTN, tm), BF16), 2 * _nbytes((D_ATTN, tm), F32),
         _nbytes((tm, d_conv), BF16)],
        [_nbytes(w_cc.shape, BF16), _nbytes(w_qkvt.shape, BF16)],
        [2 * _nbytes((tm, d_model), F32), 2 * _nbytes((tm, w_cc.shape[1]), F32),
         2 * _nbytes((w_qkvt.shape[0], tm), F32), 4 * _nbytes((tm, d_conv), F32)])
    return pl.pallas_call(
        _inproj_prompt_kernel,
        grid=(n_seq, tiles_per_seq),
        in_specs=[row_blk(d_model), _resident((1, d_model)), _resident(w_cc.shape),
                  _resident(w_qkvt.shape), _resident(wc.shape)],
        out_specs=[col_blk, col_blk, col_blk, row_blk(d_conv),
                   pl.BlockSpec((1, V7X_SUBLANES, d_conv), lambda b, j: (b, 0, 0))],
        out_shape=[jax.ShapeDtypeStruct((n_seq, D_ATTN, seq), BF16),
                   jax.ShapeDtypeStruct((n_seq, D_ATTN, seq), F32),
                   jax.ShapeDtypeStruct((n_seq, D_ATTN, seq), F32),
                   jax.ShapeDtypeStruct((rows, d_conv), BF16),
                   jax.ShapeDtypeStruct((n_seq, V7X_SUBLANES, d_conv), F32)],
        scratch_shapes=[pltpu.VMEM((tm + V7X_SUBLANES, d_conv), F32)],
        compiler_params=pltpu.CompilerParams(dimension_semantics=("arbitrary", "arbitrary"),
                                             vmem_limit_bytes=limit),
        name="inproj_prompt",
    )(x, g, w_cc, w_qkvt, wc)


def _inproj_sample(x, g, w_cc, w_qkv, wc, prev1, prev2, dec_seq):
    rows, d_model = x.shape
    d_conv = wc.shape[1]
    full = lambda shape: pl.BlockSpec(shape, lambda i: (0,) * len(shape))
    return pl.pallas_call(
        functools.partial(_inproj_sample_kernel, dec_seq=dec_seq),
        grid=(1,),
        in_specs=[full(x.shape), full((1, d_model)), full(w_cc.shape), full(w_qkv.shape),
                  full(wc.shape), full(prev1.shape), full(prev2.shape)],
        out_specs=[full((rows, D_ATTN))] * 3 + [full((rows, d_conv))] * 2,
        out_shape=[jax.ShapeDtypeStruct((rows, D_ATTN), F32)] * 3
        + [jax.ShapeDtypeStruct((rows, d_conv), BF16), jax.ShapeDtypeStruct((rows, d_conv), F32)],
        scratch_shapes=[pltpu.VMEM((rows + V7X_SUBLANES, d_conv), F32)],
        compiler_params=pltpu.CompilerParams(dimension_semantics=("arbitrary",)),
        name="inproj_sample",
    )(x, g, w_cc, w_qkv, wc, prev1, prev2)


def _attn_prompt(slopes, qt, kt, vt):
    n_seq, _, seq = qt.shape
    rows = n_seq * seq
    o_blk = pl.BlockSpec((seq, V7X_LANES), lambda p, b: (b, p))
    kv_blk = pl.BlockSpec((1, V7X_LANES, seq), lambda p, b: (b, p, 0))
    dist, logc = _attn_tables(seq)
    table = _nbytes((TQ, seq), F32)
    limit = _vmem_limit(
        [2 * _nbytes((seq, V7X_LANES), F32), 2 * _nbytes((seq, V7X_LANES), BF16)],
        [2 * _nbytes((seq, V7X_LANES), BF16), (2 + HEADS_PER_STEP) * table],
        [3 * HEADS_PER_STEP * table])
    return pl.pallas_call(
        _attn_prompt_kernel,
        grid=(N_HEADS // HEADS_PER_STEP, n_seq),
        in_specs=[pl.BlockSpec(memory_space=pltpu.SMEM), _resident((seq, TQ)),
                  _resident((seq, TQ)), kv_blk, kv_blk, kv_blk],
        out_specs=o_blk,
        out_shape=jax.ShapeDtypeStruct((rows, D_ATTN), BF16),
        scratch_shapes=[pltpu.VMEM((seq, V7X_LANES), BF16), pltpu.VMEM((V7X_LANES, seq), BF16),
                        pltpu.VMEM((HEADS_PER_STEP, seq, TQ), F32)],
        compiler_params=pltpu.CompilerParams(dimension_semantics=("arbitrary", "arbitrary"),
                                             vmem_limit_bytes=limit),
        name="attn_prompt",
    )(slopes, dist, logc, qt, kt, vt)


def _attn_sample(slopes, q, kn, vn, kst, vst):
    n_seq, dec_seq, width = q.shape
    n_past = kst.shape[2]
    new_blk = pl.BlockSpec((1, dec_seq, width), lambda b: (b, 0, 0))
    past_blk = pl.BlockSpec(memory_space=pl.ANY)
    limit = _vmem_limit(
        [], [2 * STATE_BUFFERS * _nbytes((n_past, width), F32)],
        [2 * _nbytes((n_past, width), BF16), 8 * _nbytes((dec_seq * N_HEADS, n_past), F32)])
    return pl.pallas_call(
        _attn_sample_kernel,
        grid=(n_seq,),
        in_specs=[pl.BlockSpec(memory_space=pltpu.SMEM), new_blk, new_blk, new_blk,
                  past_blk, past_blk],
        out_specs=new_blk,
        out_shape=jax.ShapeDtypeStruct((n_seq, dec_seq, width), F32),
        scratch_shapes=[pltpu.VMEM((V7X_LANES, width), F32)] * 2
        + [pltpu.VMEM((STATE_BUFFERS, width, n_past), F32)] * 2
        + [pltpu.SemaphoreType.DMA((2, STATE_BUFFERS))],
        compiler_params=pltpu.CompilerParams(dimension_semantics=("arbitrary",),
                                             vmem_limit_bytes=limit),
        name="attn_sample",
    )(slopes, q, kn, vn, kst, vst)


def _ffn(x, attn_o, conv_o, wo, g2, wg, wu, wd, gf, final_norm, tm):
    rows, d_model = x.shape
    d_ff = wg.shape[1]
    row_blk = lambda width: pl.BlockSpec((tm, width), lambda i: (i, 0))
    limit = _vmem_limit(
        [2 * _nbytes((tm, d_model), F32), _nbytes((tm, attn_o.shape[1]), attn_o.dtype),
         _nbytes((tm, conv_o.shape[1]), conv_o.dtype)],
        [_nbytes(w.shape, BF16) for w in (wo, wg, wu, wd)],
        [4 * _nbytes((tm, d_model), F32), 3 * _nbytes((tm, d_ff), F32)])
    return pl.pallas_call(
        functools.partial(_ffn_kernel, final_norm=final_norm),
        grid=(rows // tm,),
        in_specs=[row_blk(d_model), row_blk(attn_o.shape[1]), row_blk(conv_o.shape[1]),
                  _resident(wo.shape), _resident((1, d_model)), _resident(wg.shape),
                  _resident(wu.shape), _resident(wd.shape), _resident((1, d_model))],
        out_specs=row_blk(d_model),
        out_shape=jax.ShapeDtypeStruct((rows, d_model), F32),
        compiler_params=pltpu.CompilerParams(dimension_semantics=("arbitrary",),
                                             vmem_limit_bytes=limit),
        name="ffn",
    )(x, attn_o, conv_o, wo, g2, wg, wu, wd, gf)


def _alibi_slopes():
    return jnp.exp2(-8.0 * jnp.arange(1, N_HEADS + 1, dtype=F32) / N_HEADS)


def kernel(x_prompt, x_sample, state_attn_k, state_attn_v, state_conv, norm_mix_g, w_in, w_conv,
           w_out, norm_ffn_g, w_gate, w_up, w_down, norm_final_g):
    depth = w_in.shape[0]
    batch, seq, d_model = x_prompt.shape
    dec_batch, dec_seq, _ = x_sample.shape
    d_conv = w_conv.shape[2]
    n_past = state_attn_k.shape[2]
    assert n_past >= max(WINDOWS) and seq <= max(WINDOWS) and seq % TQ == 0
    assert CONV_WIDTH - 1 <= dec_seq <= V7X_LANES
    slopes = _alibi_slopes()
    gf = norm_final_g.reshape(1, d_model)

    yp = x_prompt.reshape(batch * seq, d_model)
    ys = x_sample.reshape(dec_batch * dec_seq, d_model)
    outs = [[] for _ in range(6)]
    for layer in range(depth):
        g1 = norm_mix_g[layer].reshape(1, d_model)
        g2 = norm_ffn_g[layer].reshape(1, d_model)
        w_in_bf = w_in[layer].astype(BF16)
        w_qkv = w_in_bf[:, :3 * D_ATTN]
        w_cc = w_in_bf[:, 3 * D_ATTN:]
        weights = [w.astype(BF16) for w in (w_out[layer], w_gate[layer], w_up[layer], w_down[layer])]
        wc = w_conv[layer]

        qt, kt, vt, conv_o, u_last = _inproj_prompt(yp, g1, w_cc, w_qkv.T, wc, seq)
        attn_o = _attn_prompt(slopes, qt, kt, vt)
        y_layer = _ffn(yp, attn_o, conv_o, weights[0], g2, *weights[1:], gf,
                       layer == depth - 1, _row_tile(seq, largest=FFN_ROWS))
        to_heads = lambda a: a.reshape(batch, N_HEADS, HEAD_DIM, seq).transpose(0, 3, 1, 2)
        outs[0].append(to_heads(kt))
        outs[1].append(to_heads(vt))
        outs[2].append(u_last[:, V7X_SUBLANES - (CONV_WIDTH - 1):, :])
        yp = y_layer

        st = state_conv[layer]
        prev1 = jnp.pad(st[:, 1:2], ((0, 0), (0, dec_seq - 1), (0, 0)))
        prev2 = jnp.pad(st, ((0, 0), (0, dec_seq - 2), (0, 0)))
        qs, ks, vs, conv_s, u_s = _inproj_sample(
            ys, g1, w_cc, w_qkv, wc, prev1.reshape(-1, d_conv), prev2.reshape(-1, d_conv), dec_seq)
        as3 = lambda a: a.reshape(dec_batch, dec_seq, D_ATTN)
        feature_major = lambda a: a.transpose(0, 2, 3, 1).reshape(dec_batch, D_ATTN, n_past)
        attn_s = _attn_sample(slopes, as3(qs), as3(ks), as3(vs),
                              feature_major(state_attn_k[layer]),
                              feature_major(state_attn_v[layer]))
        ys = _ffn(ys, attn_s.reshape(-1, D_ATTN), conv_s, weights[0], g2, *weights[1:], gf,
                  layer == depth - 1, ys.shape[0])
        outs[3].append(ks.reshape(dec_batch, dec_seq, N_HEADS, HEAD_DIM))
        outs[4].append(vs.reshape(dec_batch, dec_seq, N_HEADS, HEAD_DIM))
        outs[5].append(u_s.reshape(dec_batch, dec_seq, d_conv)[:, dec_seq - (CONV_WIDTH - 1):])

    y_prompt = yp.reshape(batch, seq, d_model)
    y_sample = ys.reshape(dec_batch, dec_seq, d_model)
    new_k_p, new_v_p, new_c_p, new_k_s, new_v_s, new_c_s = [jnp.stack(o) for o in outs]
    return (y_prompt, y_sample, new_k_p, new_v_p, new_c_p, new_k_s, new_v_s, new_c_s)
```

```python
import functools

import jax
import jax.numpy as jnp
import numpy as np
from jax import lax
from jax.experimental import pallas as pl
from jax.experimental.pallas import tpu as pltpu

F32 = jnp.float32
BF16 = jnp.bfloat16

HEAD_DIM = 64
N_HEADS = 8
D_ATTN = N_HEADS * HEAD_DIM
CONV_WIDTH = 3
WINDOWS = (128, 512, 2048)
DILATIONS = (1, 4, 16)
RMS_EPS = 1e-6
ATTN_SCALE = HEAD_DIM ** -0.5
LOG2_E = 1.4426950408889634
NEG_INF = -1e30

V7X_LANES = 128
V7X_SUBLANES = 8
V7X_VMEM_BYTES = 64 * 1024 * 1024

HEADS_PER_STEP = V7X_LANES // HEAD_DIM
TQ = 256
HALF_TQ = TQ // 2
TK = 512
ATTN_STREAMS = 4
ATTN_LOOKAHEAD = 1
ROW_GROUPS = 4
FFN_ROWS = 1024
INPROJ_ROW_GROUPS = 8
INPROJ_ROWS = 2048
STATE_BUFFERS = 3


def _rms(x, g):
    y = x * lax.rsqrt(jnp.mean(x * x, axis=-1, keepdims=True) + RMS_EPS)
    return y * g


def _branch_count(delta):
    nonneg = delta >= 0
    c = jnp.zeros(delta.shape, F32)
    for w, d in zip(WINDOWS, DILATIONS):
        hit = nonneg & ((delta & (d - 1)) == 0) & (delta <= w)
        c = c + hit.astype(F32)
    return c


def _dot_nt(a, b):
    return lax.dot_general(a, b, (((1,), (1,)), ((), ())), preferred_element_type=F32)


def _project(x, g, wcc_ref):
    d_conv = wcc_ref.shape[1] // 3
    xn = _rms(x, g).astype(BF16)
    zc = jnp.dot(xn, wcc_ref[...], preferred_element_type=F32)
    return xn, zc[:, 0:d_conv], zc[:, d_conv:2 * d_conv], zc[:, 2 * d_conv:3 * d_conv]


def _conv3(wc, u2, u1, u0):
    acc = wc[0:1, :] * u2
    acc = acc + wc[1:2, :] * u1
    return acc + wc[2:3, :] * u0


def _inproj_prompt_kernel(x_ref, g_ref, wcc_ref, wqkvt_ref, wc_ref, qt_ref, kt_ref, vt_ref, co_ref,
                          ulast_ref, uext_ref):
    tm = x_ref.shape[0]
    j = pl.program_id(1)
    tiles_per_seq = pl.num_programs(1)

    @pl.when(j == 0)
    def _():
        uext_ref[0:V7X_SUBLANES, :] = jnp.zeros((V7X_SUBLANES, uext_ref.shape[1]), F32)

    rg = tm // INPROJ_ROW_GROUPS
    for r0 in range(0, tm, rg):
        xn, hc, gb, gc = _project(x_ref[r0:r0 + rg, :], g_ref[...], wcc_ref)
        qkvt = _dot_nt(wqkvt_ref[...], xn)
        qt_ref[0, :, r0:r0 + rg] = (qkvt[0:D_ATTN, :] * (ATTN_SCALE * LOG2_E)).astype(qt_ref.dtype)
        kt_ref[0, :, r0:r0 + rg] = qkvt[D_ATTN:2 * D_ATTN, :]
        vt_ref[0, :, r0:r0 + rg] = qkvt[2 * D_ATTN:3 * D_ATTN, :]
        u = gc * hc
        uext_ref[V7X_SUBLANES:V7X_SUBLANES + rg, :] = u
        u1 = uext_ref[V7X_SUBLANES - 1:V7X_SUBLANES - 1 + rg, :]
        u2 = uext_ref[V7X_SUBLANES - 2:V7X_SUBLANES - 2 + rg, :]
        co_ref[r0:r0 + rg, :] = (gb * _conv3(wc_ref[...], u2, u1, u)).astype(co_ref.dtype)
        tail = u[rg - V7X_SUBLANES:rg, :]
        uext_ref[0:V7X_SUBLANES, :] = tail

    @pl.when(j == tiles_per_seq - 1)
    def _():
        ulast_ref[0] = tail


def _inproj_sample_kernel(x_ref, g_ref, wcc_ref, wqkv_ref, wc_ref, p1_ref, p2_ref, q_ref, k_ref,
                          v_ref, co_ref, u_ref, uext_ref, *, dec_seq):
    rows = x_ref.shape[0]
    xn, hc, gb, gc = _project(x_ref[...], g_ref[...], wcc_ref)
    qkv = jnp.dot(xn, wqkv_ref[...], preferred_element_type=F32)
    q_ref[...] = qkv[:, 0:D_ATTN] * ATTN_SCALE
    k_ref[...] = qkv[:, D_ATTN:2 * D_ATTN]
    v_ref[...] = qkv[:, 2 * D_ATTN:3 * D_ATTN]
    u = gc * hc
    u_ref[...] = u
    uext_ref[0:V7X_SUBLANES, :] = jnp.zeros((V7X_SUBLANES, u.shape[1]), F32)
    uext_ref[V7X_SUBLANES:V7X_SUBLANES + rows, :] = u
    t = lax.broadcasted_iota(jnp.int32, u.shape, 0) % dec_seq
    u1 = jnp.where(t >= 1, uext_ref[V7X_SUBLANES - 1:V7X_SUBLANES - 1 + rows, :], p1_ref[...])
    u2 = jnp.where(t >= 2, uext_ref[V7X_SUBLANES - 2:V7X_SUBLANES - 2 + rows, :], p2_ref[...])
    co_ref[...] = (gb * _conv3(wc_ref[...], u2, u1, u)).astype(co_ref.dtype)


def _attn_tables(seq):
    x = np.arange(seq, dtype=np.int32)[:, None]
    i = np.arange(TQ, dtype=np.int32)[None, :]
    delta = i + (seq - TQ) - x
    c = np.zeros(delta.shape, np.float32)
    for w, d in zip(WINDOWS, DILATIONS):
        c += (delta >= 0) & (delta % d == 0) & (delta <= w)
    logc = np.where(c > 0.0, np.log2(np.maximum(c, 1.0)), NEG_INF).astype(np.float32)
    return delta.astype(np.float32), logc


def _attn_prompt_kernel(slopes_ref, dist_ref, logc_ref, qt_ref, kt_ref, vt_ref, o_ref,
                        kb_ref, vtb_ref, bm_ref):
    seq = qt_ref.shape[2]
    pair = pl.program_id(0)
    feature = lax.broadcasted_iota(jnp.int32, (V7X_LANES, 1), 0)
    first_head = feature < HEAD_DIM

    @pl.when(pl.program_id(1) == 0)
    def _():
        for h in range(HEADS_PER_STEP):
            slope = slopes_ref[HEADS_PER_STEP * pair + h]
            bm_ref[h] = logc_ref[...] - (slope * LOG2_E) * dist_ref[...]

    kb_ref[...] = kt_ref[0].T.astype(BF16)
    vtb_ref[...] = vt_ref[0].astype(BF16)

    def score_half(st, chunk):
        k0, n = chunk
        h, row0 = st["h"], seq - (st["q0"] + TQ)
        s = (jnp.dot(kb_ref[k0:k0 + n, :], st["q"], preferred_element_type=F32)
             + bm_ref[h, row0 + k0:row0 + k0 + n, :])
        tail = HALF_TQ if k0 + n == st["q0"] + TQ else 0
        main = s[0:n - tail, :]
        c8 = jnp.max(main.reshape((n - tail) // V7X_SUBLANES, V7X_SUBLANES, TQ), axis=0)
        s_tail = None
        if tail:
            s_tail = s[n - tail:n, HALF_TQ:TQ]
            t8 = jnp.max(s_tail.reshape(tail // V7X_SUBLANES, V7X_SUBLANES, HALF_TQ), axis=0)
            c8 = jnp.concatenate([c8[:, 0:HALF_TQ], jnp.maximum(c8[:, HALF_TQ:TQ], t8)], axis=1)
        m_new = jnp.maximum(st["m"], jnp.max(c8, axis=0, keepdims=True))
        pending = dict(k0=k0, n=n, s=main, s_tail=s_tail, m=m_new,
                       alpha=jnp.exp2(st["m"] - m_new))
        st["m"] = m_new
        return pending

    def value_half(st, pending):
        h, k0, n = st["h"], pending["k0"], pending["n"]
        p = jnp.exp2(pending["s"] - pending["m"])
        l8 = jnp.sum(p.reshape(p.shape[0] // V7X_SUBLANES, V7X_SUBLANES, TQ), axis=0)
        p = p.astype(BF16)
        if pending["s_tail"] is not None:
            p_tail = jnp.exp2(pending["s_tail"] - pending["m"][:, HALF_TQ:TQ])
            t8 = jnp.sum(p_tail.reshape(HALF_TQ // V7X_SUBLANES, V7X_SUBLANES, HALF_TQ), axis=0)
            l8 = jnp.concatenate([l8[:, 0:HALF_TQ], l8[:, HALF_TQ:TQ] + t8], axis=1)
            p_tail = jnp.concatenate([jnp.zeros((HALF_TQ, HALF_TQ), BF16), p_tail.astype(BF16)],
                                     axis=1)
            p = jnp.concatenate([p, p_tail], axis=0)
        st["l8"] = pending["alpha"] * st["l8"] + l8
        vt_h = vtb_ref[h * HEAD_DIM:(h + 1) * HEAD_DIM, k0:k0 + n]
        st["acc"] = pending["alpha"] * st["acc"] + jnp.dot(vt_h, p, preferred_element_type=F32)

    def open_block(q0):
        qt = qt_ref[0, :, q0:q0 + TQ]
        zero = jnp.zeros_like(qt)
        chunks = [(max(k1 - TK, 0), min(TK, k1)) for k1 in range(q0 + TQ, 0, -TK)]
        block = dict(q0=q0, left=HEADS_PER_STEP * len(chunks))
        block["streams"] = [
            dict(block=block, h=h, q0=q0, q=qh, chunks=chunks, m=jnp.full((1, TQ), NEG_INF, F32),
                 l8=jnp.zeros((V7X_SUBLANES, TQ), F32), acc=jnp.zeros((HEAD_DIM, TQ), F32))
            for h, qh in enumerate([jnp.where(first_head, qt, zero), jnp.where(first_head, zero, qt)])]
        return block

    def retire(st, pending):
        value_half(st, pending)
        block = st["block"]
        block["left"] -= 1
        if block["left"] == 0:
            outs = [s["acc"] / jnp.sum(s["l8"], axis=0, keepdims=True) for s in block["streams"]]
            q0 = block["q0"]
            o_ref[q0:q0 + TQ, :] = jnp.concatenate(outs, axis=0).T.astype(o_ref.dtype)

    blocks_per_group = ATTN_STREAMS // HEADS_PER_STEP
    waiting = []
    for g0 in range(0, seq, TQ * blocks_per_group):
        streams = [st for q0 in range(g0, g0 + TQ * blocks_per_group, TQ)
                   for st in open_block(q0)["streams"]]
        for t in range(max(len(st["chunks"]) for st in streams)):
            live = [(st, st["chunks"][t]) for st in streams if t < len(st["chunks"])]
            due = waiting.pop(0) if len(waiting) >= ATTN_LOOKAHEAD else []
            issued = []
            for i, (st, chunk) in enumerate(live):
                issued.append((st, score_half(st, chunk)))
                if i < len(due):
                    retire(*due[i])
            for st, pending in due[len(live):]:
                retire(st, pending)
            waiting.append(issued)
    for step in waiting:
        for st, pending in step:
            retire(st, pending)


def _attn_sample_kernel(slopes_ref, q_ref, kn_ref, vn_ref, kst_hbm, vst_hbm, o_ref,
                        knew_ref, vnew_ref, kbuf_ref, vbuf_ref, sem_ref):
    dec_seq = q_ref.shape[1]
    n_seq, _, n_past = kst_hbm.shape
    width = q_ref.shape[2]
    n_rows = dec_seq * N_HEADS
    b = pl.program_id(0)

    def state_copies(seq_idx, slot):
        return (pltpu.make_async_copy(kst_hbm.at[seq_idx], kbuf_ref.at[slot], sem_ref.at[0, slot]),
                pltpu.make_async_copy(vst_hbm.at[seq_idx], vbuf_ref.at[slot], sem_ref.at[1, slot]))

    q = q_ref[0]
    head_of_lane = lax.broadcasted_iota(jnp.int32, (N_HEADS, width), 1) // HEAD_DIM
    head_of_row = lax.broadcasted_iota(jnp.int32, (N_HEADS, width), 0)
    head_mask = head_of_lane == head_of_row
    wt = jnp.concatenate(
        [jnp.where(head_mask, jnp.broadcast_to(q[t:t + 1, :], (N_HEADS, width)), 0.0)
         for t in range(dec_seq)], axis=0).astype(BF16)

    knew_ref[...] = jnp.zeros(knew_ref.shape, F32)
    vnew_ref[...] = jnp.zeros(vnew_ref.shape, F32)
    knew_ref[0:dec_seq, :] = kn_ref[0]
    vnew_ref[0:dec_seq, :] = vn_ref[0]

    row = lax.broadcasted_iota(jnp.int32, (n_rows, 1), 0)
    step = row // N_HEADS
    slope_col = jnp.zeros((n_rows, 1), F32)
    for h in range(N_HEADS):
        slope_col = jnp.where(row % N_HEADS == h, slopes_ref[h], slope_col)

    def weights(key_pos):
        delta = n_past + step - key_pos
        return _branch_count(delta), slope_col * delta.astype(F32)

    def weigh(s, c, bias):
        return jnp.where(c > 0.0, s - bias, NEG_INF)

    c_past, bias_past = weights(lax.broadcasted_iota(jnp.int32, (1, n_past), 1))
    c_new, bias_new = weights(n_past + lax.broadcasted_iota(jnp.int32, (1, V7X_LANES), 1))
    s_new = weigh(_dot_nt(wt, knew_ref[...].astype(BF16)), c_new, bias_new)
    m_new = jnp.max(s_new, axis=-1, keepdims=True)

    @pl.when(b == 0)
    def _():
        for ahead in range(STATE_BUFFERS - 1):
            for copy in state_copies(ahead, ahead):
                copy.start()

    nxt = b + (STATE_BUFFERS - 1)

    @pl.when(nxt < n_seq)
    def _():
        for copy in state_copies(nxt, nxt % STATE_BUFFERS):
            copy.start()

    slot = b % STATE_BUFFERS
    k_copy, v_copy = state_copies(b, slot)
    k_copy.wait()
    s_past = jnp.dot(wt, kbuf_ref[slot].astype(BF16), preferred_element_type=F32)
    s_past = weigh(s_past, c_past, bias_past)
    m = jnp.maximum(jnp.max(s_past, axis=-1, keepdims=True), m_new)
    p_past = c_past * jnp.exp(s_past - m)
    p_new = c_new * jnp.exp(s_new - m)
    l = jnp.sum(p_past, axis=-1, keepdims=True) + jnp.sum(p_new, axis=-1, keepdims=True)
    v_copy.wait()
    o_all = (_dot_nt(p_past.astype(BF16), vbuf_ref[slot].astype(BF16))
             + jnp.dot(p_new.astype(BF16), vnew_ref[...].astype(BF16),
                       preferred_element_type=F32)) / l
    outs = []
    for t in range(dec_seq):
        blk = o_all[t * N_HEADS:(t + 1) * N_HEADS, :]
        outs.append(jnp.sum(jnp.where(head_mask, blk, 0.0), axis=0, keepdims=True))
    o_ref[0] = jnp.concatenate(outs, axis=0).astype(o_ref.dtype)


def _ffn_kernel(x_ref, a_ref, c_ref, wo_ref, g2_ref, wg_ref, wu_ref, wd_ref, gf_ref, y_ref, *,
                final_norm):
    tm = x_ref.shape[0]
    rg = tm // ROW_GROUPS if tm % (ROW_GROUPS * V7X_LANES) == 0 else tm
    groups = [slice(r0, r0 + rg) for r0 in range(0, tm, rg)]
    hs = []
    for rows in groups:
        mix = jnp.concatenate([a_ref[rows, :].astype(BF16), c_ref[rows, :].astype(BF16)], axis=-1)
        hs.append(x_ref[rows, :] + jnp.dot(mix, wo_ref[...], preferred_element_type=F32))
    acts = []
    for h in hs:
        hn = _rms(h, g2_ref[...]).astype(BF16)
        gate = jnp.dot(hn, wg_ref[...], preferred_element_type=F32)
        up = jnp.dot(hn, wu_ref[...], preferred_element_type=F32)
        acts.append(((gate * (1.0 / (1.0 + jnp.exp(-gate)))) * up).astype(BF16))
    for rows, h, act in zip(groups, hs, acts):
        y = h + jnp.dot(act, wd_ref[...], preferred_element_type=F32)
        y_ref[rows, :] = _rms(y, gf_ref[...]) if final_norm else y


def _nbytes(shape, dtype):
    n = 1
    for s in shape:
        n *= s
    return n * jnp.dtype(dtype).itemsize


def _vmem_limit(pipelined, resident, temporaries):
    need = 2 * sum(pipelined) + sum(resident) + sum(temporaries)
    return min(V7X_VMEM_BYTES, need + need // 4)


def _resident(shape):
    zeros = (0,) * len(shape)
    return pl.BlockSpec(shape, lambda *_: zeros, pipeline_mode=pl.Buffered(1))


def _row_tile(rows, largest=512):
    for tm in (largest, 512, 256, 128):
        if rows % tm == 0:
            return tm
    raise ValueError(f"row count {rows} is not a multiple of 128")


def _inproj_prompt(x, g, w_cc, w_qkvt, wc, seq):
    rows, d_model = x.shape
    d_conv = wc.shape[1]
    tm = _row_tile(seq, largest=INPROJ_ROWS)
    tiles_per_seq = seq // tm
    n_seq = rows // seq
    row_blk = lambda width: pl.BlockSpec((tm, width), lambda b, j: (b * tiles_per_seq + j, 0))
    col_blk = pl.BlockSpec((1, D_ATTN, tm), lambda b, j: (b, 0, j))
    limit = _vmem_limit(
        [_nbytes((tm, d_model), F32), _nbytes((D_ATTN, tm), BF16), 2 * _nbytes((D_ATTN, tm), F32),
         _nbytes((tm, d_conv), BF16)],
        [_nbytes(w_cc.shape, BF16), _nbytes(w_qkvt.shape, BF16)],
        [2 * _nbytes((tm, d_model), F32), 2 * _nbytes((tm, w_cc.shape[1]), F32),
         2 * _nbytes((w_qkvt.shape[0], tm), F32), 4 * _nbytes((tm, d_conv), F32)])
    return pl.pallas_call(
        _inproj_prompt_kernel,
        grid=(n_seq, tiles_per_seq),
        in_specs=[row_blk(d_model), _resident((1, d_model)), _resident(w_cc.shape),
                  _resident(w_qkvt.shape), _resident(wc.shape)],
        out_specs=[col_blk, col_blk, col_blk, row_blk(d_conv),
                   pl.BlockSpec((1, V7X_SUBLANES, d_conv), lambda b, j: (b, 0, 0))],
        out_shape=[jax.ShapeDtypeStruct((n_seq, D_ATTN, seq), BF16),
                   jax.ShapeDtypeStruct((n_seq, D_ATTN, seq), F32),
                   jax.ShapeDtypeStruct((n_seq, D_ATTN, seq), F32),
                   jax.ShapeDtypeStruct((rows, d_conv), BF16),
                   jax.ShapeDtypeStruct((n_seq, V7X_SUBLANES, d_conv), F32)],
        scratch_shapes=[pltpu.VMEM((tm + V7X_SUBLANES, d_conv), F32)],
        compiler_params=pltpu.CompilerParams(dimension_semantics=("parallel", "arbitrary"),
                                             vmem_limit_bytes=limit),
        name="inproj_prompt",
    )(x, g, w_cc, w_qkvt, wc)


def _inproj_sample(x, g, w_cc, w_qkv, wc, prev1, prev2, dec_seq):
    rows, d_model = x.shape
    d_conv = wc.shape[1]
    full = lambda shape: pl.BlockSpec(shape, lambda i: (0,) * len(shape))
    return pl.pallas_call(
        functools.partial(_inproj_sample_kernel, dec_seq=dec_seq),
        grid=(1,),
        in_specs=[full(x.shape), full((1, d_model)), full(w_cc.shape), full(w_qkv.shape),
                  full(wc.shape), full(prev1.shape), full(prev2.shape)],
        out_specs=[full((rows, D_ATTN))] * 3 + [full((rows, d_conv))] * 2,
        out_shape=[jax.ShapeDtypeStruct((rows, D_ATTN), F32)] * 3
        + [jax.ShapeDtypeStruct((rows, d_conv), BF16), jax.ShapeDtypeStruct((rows, d_conv), F32)],
        scratch_shapes=[pltpu.VMEM((rows + V7X_SUBLANES, d_conv), F32)],
        compiler_params=pltpu.CompilerParams(dimension_semantics=("arbitrary",)),
        name="inproj_sample",
    )(x, g, w_cc, w_qkv, wc, prev1, prev2)


def _attn_prompt(slopes, qt, kt, vt):
    n_seq, _, seq = qt.shape
    rows = n_seq * seq
    o_blk = pl.BlockSpec((seq, V7X_LANES), lambda p, b: (b, p))
    kv_blk = pl.BlockSpec((1, V7X_LANES, seq), lambda p, b: (b, p, 0))
    dist, logc = _attn_tables(seq)
    table = _nbytes((TQ, seq), F32)
    limit = _vmem_limit(
        [2 * _nbytes((seq, V7X_LANES), F32), 2 * _nbytes((seq, V7X_LANES), BF16)],
        [2 * _nbytes((seq, V7X_LANES), BF16), (2 + HEADS_PER_STEP) * table],
        [3 * HEADS_PER_STEP * table])
    return pl.pallas_call(
        _attn_prompt_kernel,
        grid=(N_HEADS // HEADS_PER_STEP, n_seq),
        in_specs=[pl.BlockSpec(memory_space=pltpu.SMEM), _resident((seq, TQ)),
                  _resident((seq, TQ)), kv_blk, kv_blk, kv_blk],
        out_specs=o_blk,
        out_shape=jax.ShapeDtypeStruct((rows, D_ATTN), BF16),
        scratch_shapes=[pltpu.VMEM((seq, V7X_LANES), BF16), pltpu.VMEM((V7X_LANES, seq), BF16),
                        pltpu.VMEM((HEADS_PER_STEP, seq, TQ), F32)],
        compiler_params=pltpu.CompilerParams(dimension_semantics=("parallel", "arbitrary"),
                                             vmem_limit_bytes=limit),
        name="attn_prompt",
    )(slopes, dist, logc, qt, kt, vt)


def _attn_sample(slopes, q, kn, vn, kst, vst):
    n_seq, dec_seq, width = q.shape
    n_past = kst.shape[2]
    new_blk = pl.BlockSpec((1, dec_seq, width), lambda b: (b, 0, 0))
    past_blk = pl.BlockSpec(memory_space=pl.ANY)
    limit = _vmem_limit(
        [], [2 * STATE_BUFFERS * _nbytes((n_past, width), F32)],
        [2 * _nbytes((n_past, width), BF16), 8 * _nbytes((dec_seq * N_HEADS, n_past), F32)])
    return pl.pallas_call(
        _attn_sample_kernel,
        grid=(n_seq,),
        in_specs=[pl.BlockSpec(memory_space=pltpu.SMEM), new_blk, new_blk, new_blk,
                  past_blk, past_blk],
        out_specs=new_blk,
        out_shape=jax.ShapeDtypeStruct((n_seq, dec_seq, width), F32),
        scratch_shapes=[pltpu.VMEM((V7X_LANES, width), F32)] * 2
        + [pltpu.VMEM((STATE_BUFFERS, width, n_past), F32)] * 2
        + [pltpu.SemaphoreType.DMA((2, STATE_BUFFERS))],
        compiler_params=pltpu.CompilerParams(dimension_semantics=("arbitrary",),
                                             vmem_limit_bytes=limit),
        name="attn_sample",
    )(slopes, q, kn, vn, kst, vst)


def _ffn(x, attn_o, conv_o, wo, g2, wg, wu, wd, gf, final_norm, tm):
    rows, d_model = x.shape
    d_ff = wg.shape[1]
    row_blk = lambda width: pl.BlockSpec((tm, width), lambda i: (i, 0))
    limit = _vmem_limit(
        [2 * _nbytes((tm, d_model), F32), _nbytes((tm, attn_o.shape[1]), attn_o.dtype),
         _nbytes((tm, conv_o.shape[1]), conv_o.dtype)],
        [_nbytes(w.shape, BF16) for w in (wo, wg, wu, wd)],
        [4 * _nbytes((tm, d_model), F32), 3 * _nbytes((tm, d_ff), F32)])
    return pl.pallas_call(
        functools.partial(_ffn_kernel, final_norm=final_norm),
        grid=(rows // tm,),
        in_specs=[row_blk(d_model), row_blk(attn_o.shape[1]), row_blk(conv_o.shape[1]),
                  _resident(wo.shape), _resident((1, d_model)), _resident(wg.shape),
                  _resident(wu.shape), _resident(wd.shape), _resident((1, d_model))],
        out_specs=row_blk(d_model),
        out_shape=jax.ShapeDtypeStruct((rows, d_model), F32),
        compiler_params=pltpu.CompilerParams(dimension_semantics=("parallel",),
                                             vmem_limit_bytes=limit),
        name="ffn",
    )(x, attn_o, conv_o, wo, g2, wg, wu, wd, gf)


def _alibi_slopes():
    return jnp.exp2(-8.0 * jnp.arange(1, N_HEADS + 1, dtype=F32) / N_HEADS)


def kernel(x_prompt, x_sample, state_attn_k, state_attn_v, state_conv, norm_mix_g, w_in, w_conv,
           w_out, norm_ffn_g, w_gate, w_up, w_down, norm_final_g):
    depth = w_in.shape[0]
    batch, seq, d_model = x_prompt.shape
    dec_batch, dec_seq, _ = x_sample.shape
    d_conv = w_conv.shape[2]
    n_past = state_attn_k.shape[2]
    assert n_past >= max(WINDOWS) and seq <= max(WINDOWS) and seq % TQ == 0
    assert CONV_WIDTH - 1 <= dec_seq <= V7X_LANES
    slopes = _alibi_slopes()
    gf = norm_final_g.reshape(1, d_model)

    yp = x_prompt.reshape(batch * seq, d_model)
    ys = x_sample.reshape(dec_batch * dec_seq, d_model)
    outs = [[] for _ in range(6)]
    for layer in range(depth):
        g1 = norm_mix_g[layer].reshape(1, d_model)
        g2 = norm_ffn_g[layer].reshape(1, d_model)
        w_in_bf = w_in[layer].astype(BF16)
        w_qkv = w_in_bf[:, :3 * D_ATTN]
        w_cc = w_in_bf[:, 3 * D_ATTN:]
        weights = [w.astype(BF16) for w in (w_out[layer], w_gate[layer], w_up[layer], w_down[layer])]
        wc = w_conv[layer]

        qt, kt, vt, conv_o, u_last = _inproj_prompt(yp, g1, w_cc, w_qkv.T, wc, seq)
        attn_o = _attn_prompt(slopes, qt, kt, vt)
        y_layer = _ffn(yp, attn_o, conv_o, weights[0], g2, *weights[1:], gf,
                       layer == depth - 1, _row_tile(seq, largest=FFN_ROWS))
        to_heads = lambda a: a.reshape(batch, N_HEADS, HEAD_DIM, seq).transpose(0, 3, 1, 2)
        outs[0].append(to_heads(kt))
        outs[1].append(to_heads(vt))
        outs[2].append(u_last[:, V7X_SUBLANES - (CONV_WIDTH - 1):, :])
        yp = y_layer

        st = state_conv[layer]
        prev1 = jnp.pad(st[:, 1:2], ((0, 0), (0, dec_seq - 1), (0, 0)))
        prev2 = jnp.pad(st, ((0, 0), (0, dec_seq - 2), (0, 0)))
        qs, ks, vs, conv_s, u_s = _inproj_sample(
            ys, g1, w_cc, w_qkv, wc, prev1.reshape(-1, d_conv), prev2.reshape(-1, d_conv), dec_seq)
        as3 = lambda a: a.reshape(dec_batch, dec_seq, D_ATTN)
        feature_major = lambda a: a.transpose(0, 2, 3, 1).reshape(dec_batch, D_ATTN, n_past)
        attn_s = _attn_sample(slopes, as3(qs), as3(ks), as3(vs),
                              feature_major(state_attn_k[layer]),
                              feature_major(state_attn_v[layer]))
        ys = _ffn(ys, attn_s.reshape(-1, D_ATTN), conv_s, weights[0], g2, *weights[1:], gf,
                  layer == depth - 1, ys.shape[0])
        outs[3].append(ks.reshape(dec_batch, dec_seq, N_HEADS, HEAD_DIM))
        outs[4].append(vs.reshape(dec_batch, dec_seq, N_HEADS, HEAD_DIM))
        outs[5].append(u_s.reshape(dec_batch, dec_seq, d_conv)[:, dec_seq - (CONV_WIDTH - 1):])

    y_prompt = yp.reshape(batch, seq, d_model)
    y_sample = ys.reshape(dec_batch, dec_seq, d_model)
    new_k_p, new_v_p, new_c_p, new_k_s, new_v_s, new_c_s = [jnp.stack(o) for o in outs]
    return (y_prompt, y_sample, new_k_p, new_v_p, new_c_p, new_k_s, new_v_s, new_c_s)
```
